```python
import math
import jax
import jax.numpy as jnp
from jax import lax
import numpy as np

D_MODEL = 1024
BATCH = 8
SEQ = 2048
DEPTH = 1
DEC_BATCH = 128
DEC_SEQ = 4
PAST_LEN = 2048
PAGE_SIZE = 128

D_MIX = D_MODEL
D_ATT = D_MIX // 2
D_RWKV = D_MIX - D_ATT
A_HEADS = 4
A_DH = D_ATT // A_HEADS // 2
A_DK = 2 * A_DH
A_DV = 2 * A_DH
R_HEAD = 64
R_HEADS = D_RWKV // R_HEAD
LORA_W = 64
LORA_A = 64
LORA_G = 128
N_Q = A_HEADS * A_DK
N_K = A_HEADS * A_DK
N_V = A_HEADS * A_DV
N_SHIFT = 3 * D_RWKV + LORA_W + LORA_A + LORA_G
N_IN = N_Q + N_K + N_V + N_SHIFT
IN_SPLITS = (N_Q, N_Q + N_K, N_Q + N_K + N_V)
SHIFT_SPLITS = (D_RWKV, 2 * D_RWKV, 3 * D_RWKV, 3 * D_RWKV + LORA_W, 3 * D_RWKV + LORA_W + LORA_A)
D_FF = 2752
CONV_W = 3
D_PLE = 256
Q_BLOCK = 128
NORM_EPS = 1e-6
GN_EPS = 64e-5
ATT_SCALE = A_DH ** -0.5

kernel_name = 'hymba_diffattn_rwkv7_convffn_step'


def _rmsnorm(x, g):
    xf = x.astype(jnp.float32)
    y = xf * lax.rsqrt(jnp.mean(xf * xf, axis=-1, keepdims=True) + NORM_EPS)
    return (y * g.astype(jnp.float32)).astype(x.dtype)


def _diff_lambda(lq1, lk1, lq2, lk2, lam_init):
    f = lambda a: a.astype(jnp.float32)
    return jnp.exp(jnp.sum(f(lq1) * f(lk1))) - jnp.exp(jnp.sum(f(lq2) * f(lk2))) + lam_init


def _diff_attn_prompt(q, k, v, lam):
    b, s = q.shape[0], q.shape[1]
    nb = s // Q_BLOCK
    qb = jnp.moveaxis(q.reshape(b, nb, Q_BLOCK, A_HEADS, 2, A_DH), 1, 0)
    kpos = jnp.arange(s)

    def one_block(args):
        qi, i = args
        sc = jnp.einsum('bqhcd,bkhcd->bhcqk', qi, k).astype(jnp.float32) * ATT_SCALE
        qpos = i * Q_BLOCK + jnp.arange(Q_BLOCK)
        sc = jnp.where(kpos[None, :] <= qpos[:, None], sc, -jnp.inf)
        p = jax.nn.softmax(sc, axis=-1)
        wgt = p[:, :, 0] - lam * p[:, :, 1]
        return jnp.einsum('bhqk,bkhd->bqhd', wgt.astype(v.dtype), v)

    o = lax.map(one_block, (qb, jnp.arange(nb)))
    return jnp.moveaxis(o, 0, 1).reshape(b, s, A_HEADS, A_DV)


def _diff_attn_sample(q, k, v, k_past, v_past, lam):
    t = q.shape[1]
    n_past = k_past.shape[1]
    sp = jnp.einsum('bqhcd,bkhcd->bhcqk', q, k_past).astype(jnp.float32) * ATT_SCALE
    sn = jnp.einsum('bqhcd,bkhcd->bhcqk', q, k).astype(jnp.float32) * ATT_SCALE
    sn = jnp.where(jnp.tril(jnp.ones((t, t), dtype=bool)), sn, -jnp.inf)
    p = jax.nn.softmax(jnp.concatenate([sp, sn], axis=-1), axis=-1)
    wgt = (p[:, :, 0] - lam * p[:, :, 1]).astype(v.dtype)
    return (jnp.einsum('bhqk,bkhd->bqhd', wgt[..., :n_past], v_past)
            + jnp.einsum('bhqk,bkhd->bqhd', wgt[..., n_past:], v))


def _wkv_scan(s0, r, w, k, v, kk, a):
    def step(s, inp):
        r_t, w_t, k_t, v_t, kk_t, a_t = inp
        sa = jnp.einsum('bhvk,bhk->bhv', s, -kk_t)
        s = (s * w_t[:, :, None, :] + sa[..., None] * (kk_t * a_t)[:, :, None, :]
             + v_t[..., None] * k_t[:, :, None, :])
        return s, jnp.einsum('bhvk,bhk->bhv', s, r_t)

    xs = tuple(jnp.moveaxis(z, 1, 0) for z in (r, w, k, v, kk, a))
    s_fin, ys = lax.scan(step, s0, xs)
    return jnp.moveaxis(ys, 0, 1), s_fin


def _rwkv_group(pr, prev, wkv0, lp):
    b, t, _ = pr.shape
    f32 = jnp.float32
    shifted = jnp.concatenate([prev[:, None, :].astype(pr.dtype), pr[:, :-1]], axis=1)
    m = pr + (shifted - pr) * lp['mu_shift']
    r, k, v, xw, xa, xg = jnp.split(m, SHIFT_SPLITS, axis=-1)
    w_log = -jax.nn.softplus(-(lp['w0'] + jnp.tanh(xw) @ lp['w2']).astype(f32)) - 0.5
    decay = jnp.exp(-jnp.exp(w_log))
    a = jax.nn.sigmoid((lp['a0'] + xa @ lp['a2']).astype(f32))
    g = (jax.nn.sigmoid(xg) @ lp['g2']).astype(f32)
    hs = lambda z: z.astype(f32).reshape(b, t, R_HEADS, R_HEAD)
    r_h, k_h, v_h, a_h, w_h = hs(r), hs(k), hs(v), hs(a), hs(decay)
    kk = k_h * lp['k_k'].astype(f32).reshape(R_HEADS, R_HEAD)
    kk = kk / jnp.maximum(jnp.linalg.norm(kk, axis=-1, keepdims=True), 1e-12)
    k_h = k_h * (1.0 + (a_h - 1.0) * lp['k_a'].astype(f32).reshape(R_HEADS, R_HEAD))
    y, wkv = _wkv_scan(wkv0.astype(f32), r_h, w_h, k_h, v_h, kk, a_h)
    mean = jnp.mean(y, axis=-1, keepdims=True)
    var = jnp.mean(jnp.square(y - mean), axis=-1, keepdims=True)
    y = ((y - mean) * lax.rsqrt(var + GN_EPS)).reshape(b, t, D_RWKV)
    y = y * lp['ln_x_w'].astype(f32) + lp['ln_x_b'].astype(f32)
    bonus = jnp.sum(r_h * k_h * lp['r_k'].astype(f32), axis=-1, keepdims=True) * v_h
    y = (y + bonus.reshape(b, t, D_RWKV)) * g
    return y.astype(pr.dtype), pr[:, -1], wkv


def _conv_ffn(h, prev, lp):
    t = h.shape[1]
    u = h @ lp['w_up']
    ext = jnp.concatenate([prev.astype(u.dtype), u], axis=1)
    c = lp['conv_b'] + lp['conv_w'][0] * ext[:, 0:t]
    for j in range(1, CONV_W):
        c = c + lp['conv_w'][j] * ext[:, j:j + t]
    gate, val = jnp.split(c, 2, axis=-1)
    out = (jax.nn.gelu(gate, approximate=True) * val) @ lp['w_down']
    return out, ext[:, t:]


def _layer(x, pe, lp, lam_init, attend, shift0, wkv0, conv0):
    b, t, _ = x.shape
    h = _rmsnorm(x, lp['g_mix_pre'])
    proj = h @ lp['w_in']
    q, k, v, pr = jnp.split(proj, IN_SPLITS, axis=-1)
    q = q.reshape(b, t, A_HEADS, 2, A_DH)
    k4 = k.reshape(b, t, A_HEADS, 2, A_DH)
    v = v.reshape(b, t, A_HEADS, A_DV)
    lam = _diff_lambda(lp['lam_q1'], lp['lam_k1'], lp['lam_q2'], lp['lam_k2'], lam_init)
    o_att = attend(q, k4, v, lam)
    o_att = (_rmsnorm(o_att, lp['g_subln']) * (1.0 - lam_init)).reshape(b, t, D_ATT)
    o_rwkv, shift_new, wkv_new = _rwkv_group(pr, shift0, wkv0, lp)
    mix = jnp.concatenate([o_att.astype(x.dtype), o_rwkv], axis=-1) @ lp['w_o']
    x = x + _rmsnorm(mix, lp['g_mix_post'])
    f, conv_new = _conv_ffn(_rmsnorm(x, lp['g_ffn_pre']), conv0, lp)
    x = x + _rmsnorm(f, lp['g_ffn_post'])
    ple = (pe @ lp['w_ple']) * jax.nn.sigmoid(x @ lp['w_ple_gate'])
    x = x + _rmsnorm(ple, lp['g_ple'])
    return x, k.reshape(b, t, A_HEADS, A_DK), v, shift_new, wkv_new, conv_new


def setup_inputs(seed: int = 0) -> dict:
    key = jax.random.key(seed)
    keys = iter(jax.random.split(key, 64))

    def nrm(shape, scale):
        return jax.random.normal(next(keys), shape, jnp.float32) * scale

    def gain(shape):
        return 1.0 + nrm(shape, 0.05)

    L = DEPTH
    n_pages = PAST_LEN // PAGE_SIZE
    n_pool = (DEC_BATCH * n_pages * 5) // 4
    page_table = jax.random.permutation(next(keys), n_pool)[:DEC_BATCH * n_pages]
    page_table = page_table.reshape(DEC_BATCH, n_pages).astype(jnp.int32)
    return {
        'x_prompt': nrm((BATCH, SEQ, D_MODEL), 1.0),
        'x_sample': nrm((DEC_BATCH, DEC_SEQ, D_MODEL), 1.0),
        'p_prompt': nrm((L, BATCH, SEQ, D_PLE), 1.0),
        'p_sample': nrm((L, DEC_BATCH, DEC_SEQ, D_PLE), 1.0),
        'cache_k': nrm((L, n_pool, PAGE_SIZE, A_HEADS, A_DK), 1.0),
        'cache_v': nrm((L, n_pool, PAGE_SIZE, A_HEADS, A_DV), 1.0),
        'page_table': page_table,
        'state_shift': nrm((L, DEC_BATCH, N_SHIFT), 1.0),
        'state_wkv': nrm((L, DEC_BATCH, R_HEADS, R_HEAD, R_HEAD), 0.3),
        'state_conv': nrm((L, DEC_BATCH, CONV_W - 1, 2 * D_FF), 1.0),
        'g_mix_pre': gain((L, D_MODEL)),
        'w_in': nrm((L, D_MODEL, N_IN), D_MODEL ** -0.5),
        'lam_q1': nrm((L, A_DH), 0.1),
        'lam_k1': nrm((L, A_DH), 0.1),
        'lam_q2': nrm((L, A_DH), 0.1),
        'lam_k2': nrm((L, A_DH), 0.1),
        'g_subln': gain((L, A_DV)),
        'mu_shift': jax.random.uniform(next(keys), (L, N_SHIFT), jnp.float32),
        'w0': -3.0 + nrm((L, D_RWKV), 1.0),
        'w2': nrm((L, LORA_W, D_RWKV), 0.5 * LORA_W ** -0.5),
        'a0': nrm((L, D_RWKV), 0.1),
        'a2': nrm((L, LORA_A, D_RWKV), 0.5 * LORA_A ** -0.5),
        'g2': nrm((L, LORA_G, D_RWKV), LORA_G ** -0.5),
        'k_k': 0.85 + nrm((L, D_RWKV), 0.05),
        'k_a': gain((L, D_RWKV)),
        'r_k': nrm((L, R_HEADS, R_HEAD), 0.1),
        'ln_x_w': gain((L, D_RWKV)),
        'ln_x_b': nrm((L, D_RWKV), 0.01),
        'w_o': nrm((L, D_MIX, D_MODEL), D_MIX ** -0.5),
        'g_mix_post': gain((L, D_MODEL)),
        'g_ffn_pre': gain((L, D_MODEL)),
        'w_up': nrm((L, D_MODEL, 2 * D_FF), D_MODEL ** -0.5),
        'conv_w': nrm((L, CONV_W, 2 * D_FF), 0.3).at[:, CONV_W - 1].add(1.0),
        'conv_b': nrm((L, 2 * D_FF), 0.01),
        'w_down': nrm((L, D_FF, D_MODEL), D_FF ** -0.5),
        'g_ffn_post': gain((L, D_MODEL)),
        'w_ple': nrm((L, D_PLE, D_MODEL), D_PLE ** -0.5),
        'w_ple_gate': nrm((L, D_MODEL, D_MODEL), D_MODEL ** -0.5),
        'g_ple': gain((L, D_MODEL)),
    }


def reference(x_prompt, x_sample, p_prompt, p_sample, cache_k, cache_v, page_table,
              state_shift, state_wkv, state_conv, g_mix_pre, w_in, lam_q1, lam_k1,
              lam_q2, lam_k2, g_subln, mu_shift, w0, w2, a0, a2, g2, k_k, k_a, r_k,
              ln_x_w, ln_x_b, w_o, g_mix_post, g_ffn_pre, w_up, conv_w, conv_b, w_down,
              g_ffn_post, w_ple, w_ple_gate, g_ple):
    bp = x_prompt.shape[0]
    bd = x_sample.shape[0]
    n_past = page_table.shape[1] * cache_k.shape[2]
    xp, xs = x_prompt, x_sample
    kp_l, vp_l, ks_l, vs_l = [], [], [], []
    shp_l, wkp_l, cvp_l, shs_l, wks_l, cvs_l = [], [], [], [], [], []
    for l in range(DEPTH):
        lp = {
            'g_mix_pre': g_mix_pre[l], 'w_in': w_in[l], 'lam_q1': lam_q1[l], 'lam_k1': lam_k1[l],
            'lam_q2': lam_q2[l], 'lam_k2': lam_k2[l], 'g_subln': g_subln[l], 'mu_shift': mu_shift[l],
            'w0': w0[l], 'w2': w2[l], 'a0': a0[l], 'a2': a2[l], 'g2': g2[l], 'k_k': k_k[l],
            'k_a': k_a[l], 'r_k': r_k[l], 'ln_x_w': ln_x_w[l], 'ln_x_b': ln_x_b[l], 'w_o': w_o[l],
            'g_mix_post': g_mix_post[l], 'g_ffn_pre': g_ffn_pre[l], 'w_up': w_up[l],
            'conv_w': conv_w[l], 'conv_b': conv_b[l], 'w_down': w_down[l],
            'g_ffn_post': g_ffn_post[l], 'w_ple': w_ple[l], 'w_ple_gate': w_ple_gate[l],
            'g_ple': g_ple[l],
        }
        lam_init = 0.8 - 0.6 * math.exp(-0.3 * l)
        xp, kp, vp, shp, wkp, cvp = _layer(
            xp, p_prompt[l], lp, lam_init, _diff_attn_prompt,
            jnp.zeros((bp, N_SHIFT), xp.dtype),
            jnp.zeros((bp, R_HEADS, R_HEAD, R_HEAD), jnp.float32),
            jnp.zeros((bp, CONV_W - 1, 2 * D_FF), xp.dtype))
        k_past = cache_k[l][page_table].reshape(bd, n_past, A_HEADS, 2, A_DH)
        v_past = cache_v[l][page_table].reshape(bd, n_past, A_HEADS, A_DV)
        attend_s = lambda q, k, v, lam, kpast=k_past, vpast=v_past: _diff_attn_sample(q, k, v, kpast, vpast, lam)
        xs, ks, vs, shs, wks, cvs = _layer(
            xs, p_sample[l], lp, lam_init, attend_s,
            state_shift[l], state_wkv[l], state_conv[l])
        kp_l.append(kp); vp_l.append(vp); ks_l.append(ks); vs_l.append(vs)
        shp_l.append(shp); wkp_l.append(wkp.astype(state_wkv.dtype)); cvp_l.append(cvp)
        shs_l.append(shs); wks_l.append(wks.astype(state_wkv.dtype)); cvs_l.append(cvs)
    return (xp, xs, jnp.stack(kp_l), jnp.stack(vp_l), jnp.stack(ks_l), jnp.stack(vs_l),
            jnp.stack(shp_l), jnp.stack(wkp_l), jnp.stack(cvp_l),
            jnp.stack(shs_l), jnp.stack(wks_l), jnp.stack(cvs_l))
```

```python
import functools
import math

import jax
import jax.numpy as jnp
from jax import lax
from jax.experimental import pallas as pl
from jax.experimental.pallas import tpu as pltpu

F32 = jnp.float32
BF16 = jnp.bfloat16

D_MODEL = 1024
A_HEADS = 4
A_DH = 64
A_DK = 2 * A_DH
A_DV = 2 * A_DH
D_ATT = A_HEADS * A_DK
R_HEAD = 64
R_HEADS = 8
D_RWKV = R_HEADS * R_HEAD
LORA_W = 64
LORA_A = 64
LORA_G = 128
N_SHIFT = 3 * D_RWKV + LORA_W + LORA_A + LORA_G
N_IN = 3 * D_ATT + N_SHIFT
D_FF = 2752
CONV_W = 3
D_PLE = 256
NORM_EPS = 1e-6
GN_EPS = 64e-5
ATT_SCALE = A_DH ** -0.5

LANES = 128
SUBLANES = 8
VMEM_LIMIT_BYTES = 56 * 1024 * 1024

D_FF_PAD = ((D_FF + LANES - 1) // LANES) * LANES
PAIR = 2 * R_HEAD
N_PAIRS = D_RWKV // PAIR
X_COLS = LORA_W + LORA_A + LORA_G
CHUNK = 64


def _params(n_axes):
    return pltpu.CompilerParams(
        dimension_semantics=("arbitrary",) * n_axes,
        vmem_limit_bytes=VMEM_LIMIT_BYTES,
    )


def _const_spec(shape):
    zeros = (0,) * len(shape)
    return pl.BlockSpec(shape, lambda *_: zeros)


def _rms(x, g):
    return x * lax.rsqrt(jnp.mean(x * x, axis=-1, keepdims=True) + NORM_EPS) * g


def _dot(a, b):
    return jnp.dot(a, b, preferred_element_type=F32)


def _dot_nt(a, b):
    return lax.dot_general(a, b, (((1,), (1,)), ((), ())), preferred_element_type=F32)


def _split3(x):
    hi = x.astype(BF16)
    r1 = x - hi.astype(F32)
    mid = r1.astype(BF16)
    lo = (r1 - mid.astype(F32)).astype(BF16)
    return hi, mid, lo


def _dot_sel(x, sel, passes=3):
    out = None
    for part in _split3(x)[:passes]:
        t = _dot(part, sel)
        out = t if out is None else out + t
    return out


def _sel_dot(sel, x, passes=3):
    out = None
    for part in _split3(x)[:passes]:
        t = _dot(sel, part)
        out = t if out is None else out + t
    return out


def _sel_dot_nt(sel, x, passes=3):
    out = None
    for part in _split3(x)[:passes]:
        t = _dot_nt(sel, part)
        out = t if out is None else out + t
    return out


def _iota(shape, dim):
    return lax.broadcasted_iota(jnp.int32, shape, dim)


def _inproj_body(x_ref, g_ref, w_ref, q_ref, k_ref, v_ref, kb_ref, vb_ref, pr_ref):
    h = _rms(x_ref[...], g_ref[...]).astype(BF16)
    q = _dot(h, w_ref[:, 0:D_ATT])
    q_ref[...] = (q * ATT_SCALE).astype(BF16)
    k = _dot(h, w_ref[:, D_ATT:2 * D_ATT])
    k_ref[...] = k
    kb_ref[...] = k.astype(BF16)
    v = _dot(h, w_ref[:, 2 * D_ATT:3 * D_ATT])
    v_ref[...] = v
    vb_ref[...] = v.astype(BF16)
    pr_ref[...] = _dot(h, w_ref[:, 3 * D_ATT:N_IN])


def _inproj(x, g, w_in_bf):
    n = x.shape[0]
    tm = min(512, n)
    row = lambda i: (i, 0)
    outs = (
        jax.ShapeDtypeStruct((n, D_ATT), BF16),
        jax.ShapeDtypeStruct((n, D_ATT), F32),
        jax.ShapeDtypeStruct((n, D_ATT), F32),
        jax.ShapeDtypeStruct((n, D_ATT), BF16),
        jax.ShapeDtypeStruct((n, D_ATT), BF16),
        jax.ShapeDtypeStruct((n, N_SHIFT), F32),
    )
    return pl.pallas_call(
        _inproj_body,
        grid=(n // tm,),
        in_specs=[
            pl.BlockSpec((tm, D_MODEL), row),
            _const_spec((1, D_MODEL)),
            _const_spec((D_MODEL, N_IN)),
        ],
        out_specs=tuple(pl.BlockSpec((tm, s.shape[1]), row) for s in outs),
        out_shape=outs,
        compiler_params=_params(1),
        name="inproj",
    )(x, g, w_in_bf)


def _diff_lambda(lq1, lk1, lq2, lk2, lam_init):
    s1 = jnp.sum(lq1 * lk1, axis=-1, keepdims=True)
    s2 = jnp.sum(lq2 * lk2, axis=-1, keepdims=True)
    return jnp.exp(s1) - jnp.exp(s2) + lam_init


def _stack_maps(q):
    lane = _iota(q.shape, 1)
    zero = jnp.zeros_like(q)
    return jnp.concatenate([jnp.where(lane < A_DH, q, zero), jnp.where(lane >= A_DH, q, zero)], axis=0)


def _subln(o, gsub, lam_init):
    return _rms(o, gsub) * (1.0 - lam_init)


def _attn_prompt_body(lq1, lk1, lq2, lk2, gsub_ref, q_ref, k_ref, v_ref, o_ref, *, tq, tk, lam_init):
    qi = pl.program_id(1)
    lam = _diff_lambda(lq1[...], lk1[...], lq2[...], lk2[...], lam_init)
    n_kv = (qi * tq + tq + tk - 1) // tk
    qpos = qi * tq + _iota((2 * tq, tk), 0) % tq
    kcol = _iota((2 * tq, tk), 1)
    for h in range(A_HEADS):
        cols = slice(h * A_DK, (h + 1) * A_DK)
        qs = _stack_maps(q_ref[:, cols])

        def step(j, carry, qs=qs, cols=cols):
            m, l, acc = carry
            off = pl.multiple_of(j * tk, tk)
            kt = k_ref[pl.ds(off, tk), cols]
            vt = v_ref[pl.ds(off, tk), cols]
            s = _dot_nt(qs, kt)
            s = jnp.where(kcol + j * tk <= qpos, s, -jnp.inf)
            m_new = jnp.maximum(m, jnp.max(s, axis=-1, keepdims=True))
            p = jnp.exp(s - m_new)
            alpha = jnp.exp(m - m_new)
            l = alpha * l + jnp.sum(p, axis=-1, keepdims=True)
            acc = alpha * acc + _dot(p.astype(BF16), vt)
            return m_new, l, acc

        init = (
            jnp.full((2 * tq, 1), -jnp.inf, F32),
            jnp.zeros((2 * tq, 1), F32),
            jnp.zeros((2 * tq, A_DV), F32),
        )
        _, l, acc = lax.fori_loop(0, n_kv, step, init)
        o = acc / l
        o = o[:tq] - lam * o[tq:]
        o_ref[:, cols] = _subln(o, gsub_ref[...], lam_init).astype(o_ref.dtype)


def _attn_prompt(lams, gsub, q, kb, vb, b, s, lam_init):
    tq = min(256, s)
    tk = tq
    nq = s // tq
    lam_spec = _const_spec((1, A_DH))
    return pl.pallas_call(
        functools.partial(_attn_prompt_body, tq=tq, tk=tk, lam_init=lam_init),
        grid=(b, nq),
        in_specs=[
            lam_spec, lam_spec, lam_spec, lam_spec,
            _const_spec((1, A_DV)),
            pl.BlockSpec((tq, D_ATT), lambda bi, qi: (bi * nq + qi, 0)),
            pl.BlockSpec((s, D_ATT), lambda bi, qi: (bi, 0)),
            pl.BlockSpec((s, D_ATT), lambda bi, qi: (bi, 0)),
        ],
        out_specs=pl.BlockSpec((tq, D_ATT), lambda bi, qi: (bi * nq + qi, 0)),
        out_shape=jax.ShapeDtypeStruct((b * s, D_ATT), BF16),
        compiler_params=_params(2),
        name="attn_prompt",
    )(*lams, gsub, q, kb, vb)


def _attn_sample_body(pt_ref, lq1, lk1, lq2, lk2, gsub_ref, q_ref, kn_ref, vn_ref, *rest,
                      n_pages, t, lam_init):
    del pt_ref
    kp_refs = rest[:n_pages]
    vp_refs = rest[n_pages:2 * n_pages]
    o_ref = rest[2 * n_pages]
    lam = _diff_lambda(lq1[...], lk1[...], lq2[...], lk2[...], lam_init)
    qtok = _iota((2 * t, 1), 0) % t
    for h in range(A_HEADS):
        cols = slice(h * A_DK, (h + 1) * A_DK)
        qs = _stack_maps(q_ref[:, cols])
        qf = qs.astype(F32)
        kn = kn_ref[:, cols]
        vn = vn_ref[:, cols]
        s_past = [_dot_nt(qs, kp[:, cols].astype(BF16)) for kp in kp_refs]
        s_new = []
        for i in range(t):
            si = jnp.sum(qf * kn[i:i + 1, :], axis=-1, keepdims=True)
            s_new.append(jnp.where(qtok >= i, si, -jnp.inf))
        m = s_new[0]
        for sp in s_past:
            m = jnp.maximum(m, jnp.max(sp, axis=-1, keepdims=True))
        for sn in s_new[1:]:
            m = jnp.maximum(m, sn)
        l = jnp.zeros_like(m)
        acc = jnp.zeros((2 * t, A_DV), F32)
        for sp, vp in zip(s_past, vp_refs):
            p = jnp.exp(sp - m)
            l = l + jnp.sum(p, axis=-1, keepdims=True)
            acc = acc + _dot(p.astype(BF16), vp[:, cols].astype(BF16))
        for i, sn in enumerate(s_new):
            p = jnp.exp(sn - m)
            l = l + p
            acc = acc + p * vn[i:i + 1, :]
        o = acc / l
        o = o[:t] - lam * o[t:]
        o_ref[:, cols] = _subln(o, gsub_ref[...], lam_init).astype(o_ref.dtype)


def _attn_sample(lams, gsub, q, k_new, v_new, cache_k, cache_v, page_table, lam_init):
    bd, t, _ = q.shape
    n_pages = page_table.shape[1]
    page = cache_k.shape[1]
    lam_spec = pl.BlockSpec((1, A_DH), lambda b, pt: (0, 0))
    tok_spec = pl.BlockSpec((None, t, D_ATT), lambda b, pt: (b, 0, 0))

    def page_spec(p):
        return pl.BlockSpec((None, page, D_ATT), lambda b, pt, p=p: (pt[b, p], 0, 0))

    grid_spec = pltpu.PrefetchScalarGridSpec(
        num_scalar_prefetch=1,
        grid=(bd,),
        in_specs=[lam_spec] * 4 + [pl.BlockSpec((1, A_DV), lambda b, pt: (0, 0)), tok_spec, tok_spec, tok_spec]
        + [page_spec(p) for p in range(n_pages)] * 2,
        out_specs=tok_spec,
    )
    return pl.pallas_call(
        functools.partial(_attn_sample_body, n_pages=n_pages, t=t, lam_init=lam_init),
        grid_spec=grid_spec,
        out_shape=jax.ShapeDtypeStruct((bd, t, D_ATT), BF16),
        compiler_params=_params(1),
        name="attn_sample",
    )(page_table, *lams, gsub, q, k_new, v_new, *([cache_k] * n_pages), *([cache_v] * n_pages))


def _head_blocks(n):
    return (_iota((n, n), 0) // R_HEAD) == (_iota((n, n), 1) // R_HEAD)


def _softplus(z):
    return jnp.maximum(z, 0.0) + jnp.log1p(jnp.exp(-jnp.abs(z)))


def _rwkv_tokens(cur, prev, prm, seg_ones):
    mix = {c: cur[c] + (prev[c] - cur[c]) * prm["mu_" + c] for c in ("r", "k", "v", "x")}
    r, k, v = mix["r"], mix["k"], mix["v"]
    xwa = mix["x"][:, :LORA_W + LORA_A]
    xg = mix["x"][:, LORA_W + LORA_A:]
    w_in = prm["w0"] + _dot(jnp.tanh(xwa).astype(BF16), prm["w2"])
    lw = -jnp.exp(-_softplus(-w_in) - 0.5)
    a = jax.nn.sigmoid(prm["a0"] + _dot(xwa.astype(BF16), prm["a2"]))
    g = _dot(jax.nn.sigmoid(xg).astype(BF16), prm["g2"])
    kk = k * prm["k_k"]
    nrm = jnp.sqrt(_dot_sel(kk * kk, seg_ones, passes=2))
    kk = kk / jnp.maximum(nrm, 1e-12)
    k2 = k * (1.0 + (a - 1.0) * prm["k_a"])
    bonus = _dot_sel(r * k2 * prm["r_k"], seg_ones, passes=2) * v
    return r, lw, k2, v, kk, a, g, bonus


def _rwkv_finish(y, bonus, g, prm, seg_ones):
    mean = _dot_sel(y, seg_ones, passes=2) * (1.0 / R_HEAD)
    d = y - mean
    var = _dot_sel(d * d, seg_ones, passes=2) * (1.0 / R_HEAD)
    yn = d * lax.rsqrt(var + GN_EPS) * prm["ln_w"] + prm["ln_b"]
    return (yn + bonus) * g


def _stack_heads(x):
    lane = _iota(x.shape, 1)
    zero = jnp.zeros_like(x)
    return jnp.concatenate([jnp.where(lane < R_HEAD, x, zero), jnp.where(lane >= R_HEAD, x, zero)], axis=0)


def _wkv_chunk(r, lw, k2, v, kk, a, st):
    c = r.shape[0]
    n = 2 * c
    tril = (_iota((c, c), 1) <= _iota((c, c), 0)).astype(BF16)
    big_l = _sel_dot(tril, lw)
    l_end = big_l[c - 1:c, :]
    e_in = jnp.exp(big_l)
    e_ex = jnp.exp(big_l - lw)
    e_neg = jnp.exp(-big_l)
    e_rem = jnp.exp(l_end - big_l)
    b = kk * a
    a_t = _stack_heads(-kk * e_ex)
    r_t = _stack_heads(r * e_in)
    b_t = _stack_heads(b * e_neg).astype(BF16)
    k_t = _stack_heads(k2 * e_neg).astype(BF16)
    b_h = _stack_heads(b * e_rem).astype(BF16)
    k_h = _stack_heads(k2 * e_rem).astype(BF16)
    v_s = _stack_heads(v).astype(BF16)
    a_tb = a_t.astype(BF16)
    r_tb = r_t.astype(BF16)

    row = _iota((n, n), 0)
    col = _iota((n, n), 1)
    same = (row // c) == (col // c)
    strict = same & (col < row)
    incl = same & (col <= row)
    zero = jnp.zeros((n, n), F32)
    m_ab = jnp.where(strict, _dot_nt(a_tb, b_t), zero)
    m_ak = jnp.where(strict, _dot_nt(a_tb, k_t), zero)
    m_rb = jnp.where(incl, _dot_nt(r_tb, b_t), zero).astype(BF16)
    m_rk = jnp.where(incl, _dot_nt(r_tb, k_t), zero).astype(BF16)

    x = jnp.concatenate([a_t, _dot(m_ak.astype(BF16), v_s)], axis=1)
    p = m_ab
    steps = max(1, int(math.ceil(math.log2(c))))
    for j in range(steps):
        pb = p.astype(BF16)
        x = x + _dot(pb, x.astype(BF16))
        if j + 1 < steps:
            p = _dot(pb, pb)
    xb = x.astype(BF16)

    eye = (_iota((PAIR, PAIR), 0) == _iota((PAIR, PAIR), 1))
    eye_b = eye.astype(BF16)
    b_ht = _dot_nt(eye_b, b_h).astype(BF16)
    k_ht = _dot_nt(eye_b, k_h).astype(BF16)
    mg = _dot(b_ht, xb)
    m_mat = jnp.where(eye, jnp.exp(l_end), 0.0) + mg[:, :PAIR]
    g_mat = mg[:, PAIR:] + _dot(k_ht, v_s)
    qy = _dot(m_rb, xb)
    q_mat = r_t + qy[:, :PAIR]
    y0 = qy[:, PAIR:] + _dot(m_rk, v_s)

    stb = st.astype(BF16)
    y_st = _dot(q_mat.astype(BF16), stb) + y0
    st_new = _dot(m_mat.astype(BF16), stb) + g_mat
    return y_st[:c] + y_st[c:], st_new


def _state_in(s0):
    x = s0.reshape(PAIR, R_HEAD)
    sel = (_iota((PAIR, R_HEAD), 0) % R_HEAD == _iota((PAIR, R_HEAD), 1)).astype(BF16)
    full = _sel_dot_nt(sel, x)
    return jnp.where(_head_blocks(PAIR), full, 0.0)


def _state_out(st):
    folded = st[:R_HEAD] + st[R_HEAD:]
    eye = (_iota((PAIR, PAIR), 0) == _iota((PAIR, PAIR), 1)).astype(BF16)
    return _sel_dot_nt(eye, folded).reshape(2, R_HEAD, R_HEAD)


_RWKV_PARAM_NAMES = ("mu_r", "mu_k", "mu_v", "mu_x", "w0", "w2", "a0", "a2", "g2", "k_k", "k_a", "r_k",
                     "ln_w", "ln_b")


def _rwkv_prompt_body(*refs, n_chunks):
    n_prm = len(_RWKV_PARAM_NAMES)
    pr = dict(zip(("r", "k", "v", "x"), refs[0:4]))
    sh = dict(zip(("r", "k", "v", "x"), refs[4:8]))
    prm = {nm: ref[...] for nm, ref in zip(_RWKV_PARAM_NAMES, refs[8:8 + n_prm])}
    s0_ref = refs[8 + n_prm]
    o_ref, s_out_ref = refs[9 + n_prm], refs[10 + n_prm]
    carry = dict(zip(("r", "k", "v", "x"), refs[11 + n_prm:15 + n_prm]))
    st_ref = refs[15 + n_prm]
    g = pl.program_id(2)
    tc = o_ref.shape[0]

    @pl.when(g == 0)
    def _():
        for c in carry:
            carry[c][...] = jnp.broadcast_to(sh[c][...], carry[c].shape)
        st_ref[...] = _state_in(s0_ref[...])

    cur = {c: pr[c][...] for c in pr}
    prev = {}
    for c in cur:
        first = _iota(cur[c].shape, 0) == 0
        prev[c] = jnp.where(first, carry[c][0:1, :], pltpu.roll(cur[c], 1, 0))
    for c in cur:
        carry[c][...] = jnp.broadcast_to(cur[c][tc - 1:tc, :], carry[c].shape)

    seg_ones = _head_blocks(PAIR).astype(BF16)
    r, lw, k2, v, kk, a, gate, bonus = _rwkv_tokens(cur, prev, prm, seg_ones)
    st = st_ref[...]
    ys = []
    cl = tc // n_chunks
    for i in range(n_chunks):
        rows = slice(i * cl, (i + 1) * cl)
        y, st = _wkv_chunk(r[rows], lw[rows], k2[rows], v[rows], kk[rows], a[rows], st)
        ys.append(y)
    st_ref[...] = st
    y = jnp.concatenate(ys, axis=0)
    o_ref[...] = _rwkv_finish(y, bonus, gate, prm, seg_ones).astype(o_ref.dtype)

    @pl.when(g == pl.num_programs(2) - 1)
    def _():
        s_out_ref[...] = _state_out(st)


def _rwkv_sample_body(*refs, n_seq, t_pad, t_valid):
    n_prm = len(_RWKV_PARAM_NAMES)
    pr = dict(zip(("r", "k", "v", "x"), refs[0:4]))
    pv = dict(zip(("r", "k", "v", "x"), refs[4:8]))
    prm = {nm: ref[...] for nm, ref in zip(_RWKV_PARAM_NAMES, refs[8:8 + n_prm])}
    s0_ref = refs[8 + n_prm]
    o_ref, s_out_ref = refs[9 + n_prm], refs[10 + n_prm]
    cur = {c: pr[c][...] for c in pr}
    prev = {c: pv[c][...] for c in pv}
    seg_ones = _head_blocks(PAIR).astype(BF16)
    r, lw, k2, v, kk, a, gate, bonus = _rwkv_tokens(cur, prev, prm, seg_ones)
    valid = (_iota(r.shape, 0) % t_pad) < t_valid
    zero = jnp.zeros_like(r)
    r, lw, k2, v, kk = (jnp.where(valid, z, zero) for z in (r, lw, k2, v, kk))
    ys = []
    for i in range(n_seq):
        rows = slice(i * t_pad, (i + 1) * t_pad)
        y, st = _wkv_chunk(r[rows], lw[rows], k2[rows], v[rows], kk[rows], a[rows], _state_in(s0_ref[i]))
        s_out_ref[i] = _state_out(st)
        ys.append(y)
    y = jnp.concatenate(ys, axis=0)
    o_ref[...] = _rwkv_finish(y, bonus, gate, prm, seg_ones).astype(o_ref.dtype)


def _rwkv_param_arrays(p):
    z_w = jnp.zeros((LORA_A, D_RWKV), BF16)
    z_a = jnp.zeros((LORA_W, D_RWKV), BF16)
    return dict(
        mu=p["mu_shift"].reshape(1, N_SHIFT),
        w0=p["w0"].reshape(1, D_RWKV),
        w2=jnp.concatenate([p["w2"].astype(BF16), z_w], axis=0),
        a0=p["a0"].reshape(1, D_RWKV),
        a2=jnp.concatenate([z_a, p["a2"].astype(BF16)], axis=0),
        g2=p["g2"].astype(BF16),
        k_k=p["k_k"].reshape(1, D_RWKV),
        k_a=p["k_a"].reshape(1, D_RWKV),
        r_k=p["r_k"].reshape(1, D_RWKV),
        ln_w=p["ln_x_w"].reshape(1, D_RWKV),
        ln_b=p["ln_x_b"].reshape(1, D_RWKV),
    )


def _rwkv_param_specs(pair_of):
    n_pair_blk = D_RWKV // PAIR
    x_blk = (3 * D_RWKV) // X_COLS

    def vec(off):
        return pl.BlockSpec((1, PAIR), lambda *i: (0, off + pair_of(*i)))

    def mat(rows):
        return pl.BlockSpec((rows, PAIR), lambda *i: (0, pair_of(*i)))

    return [
        vec(0), vec(n_pair_blk), vec(2 * n_pair_blk),
        pl.BlockSpec((1, X_COLS), lambda *i: (0, x_blk)),
        vec(0), mat(LORA_W + LORA_A), vec(0), mat(LORA_W + LORA_A), mat(LORA_G),
        vec(0), vec(0), vec(0), vec(0), vec(0),
    ]


def _rwkv_param_operands(pa):
    return [pa["mu"], pa["mu"], pa["mu"], pa["mu"], pa["w0"], pa["w2"], pa["a0"], pa["a2"], pa["g2"],
            pa["k_k"], pa["k_a"], pa["r_k"], pa["ln_w"], pa["ln_b"]]


def _col_specs(rows, row_of, pair_of):
    n_pair_blk = D_RWKV // PAIR
    x_blk = (3 * D_RWKV) // X_COLS
    specs = [pl.BlockSpec((rows, PAIR), lambda *i, o=o: (row_of(*i), o * n_pair_blk + pair_of(*i)))
             for o in range(3)]
    specs.append(pl.BlockSpec((rows, X_COLS), lambda *i: (row_of(*i), x_blk)))
    return specs


def _rwkv_prompt(pr, shift0, s0, pa, b, s):
    tc = min(512, s)
    nt = s // tc
    n_chunks = max(1, tc // CHUNK)
    pair_of = lambda bi, pi, gi: pi
    row_of = lambda bi, pi, gi: bi * nt + gi
    n_pair_blk = D_RWKV // PAIR
    x_blk = (3 * D_RWKV) // X_COLS
    shift_specs = [pl.BlockSpec((None, 1, PAIR), lambda bi, pi, gi, o=o: (bi, 0, o * n_pair_blk + pi))
                   for o in range(3)]
    shift_specs.append(pl.BlockSpec((None, 1, X_COLS), lambda bi, pi, gi: (bi, 0, x_blk)))
    state_spec = pl.BlockSpec((None, 2, R_HEAD, R_HEAD), lambda bi, pi, gi: (bi, pi, 0, 0))
    shift3 = shift0.reshape(b, 1, N_SHIFT)
    return pl.pallas_call(
        functools.partial(_rwkv_prompt_body, n_chunks=n_chunks),
        grid=(b, N_PAIRS, nt),
        in_specs=_col_specs(tc, row_of, pair_of) + shift_specs + _rwkv_param_specs(pair_of) + [state_spec],
        out_specs=(pl.BlockSpec((tc, PAIR), lambda bi, pi, gi: (bi * nt + gi, pi)), state_spec),
        out_shape=(jax.ShapeDtypeStruct((b * s, D_RWKV), BF16),
                   jax.ShapeDtypeStruct((b, R_HEADS, R_HEAD, R_HEAD), F32)),
        scratch_shapes=[pltpu.VMEM((SUBLANES, PAIR), F32)] * 3 + [pltpu.VMEM((SUBLANES, X_COLS), F32),
                                                                  pltpu.VMEM((PAIR, PAIR), F32)],
        compiler_params=_params(3),
        name="rwkv_prompt",
    )(pr, pr, pr, pr, shift3, shift3, shift3, shift3, *_rwkv_param_operands(pa), s0)


def _rwkv_sample(pr_pad, prev_pad, s0, pa, bd, t_pad, t_valid):
    n_seq = min(8, bd)
    rows = n_seq * t_pad
    pair_of = lambda gi, pi: pi
    row_of = lambda gi, pi: gi
    state_spec = pl.BlockSpec((n_seq, 2, R_HEAD, R_HEAD), lambda gi, pi: (gi, pi, 0, 0))
    return pl.pallas_call(
        functools.partial(_rwkv_sample_body, n_seq=n_seq, t_pad=t_pad, t_valid=t_valid),
        grid=(bd // n_seq, N_PAIRS),
        in_specs=_col_specs(rows, row_of, pair_of) * 2 + _rwkv_param_specs(pair_of) + [state_spec],
        out_specs=(pl.BlockSpec((rows, PAIR), lambda gi, pi: (gi, pi)), state_spec),
        out_shape=(jax.ShapeDtypeStruct((bd * t_pad, D_RWKV), BF16),
                   jax.ShapeDtypeStruct((bd, R_HEADS, R_HEAD, R_HEAD), F32)),
        compiler_params=_params(2),
        name="rwkv_sample",
    )(pr_pad, pr_pad, pr_pad, pr_pad, prev_pad, prev_pad, prev_pad, prev_pad, *_rwkv_param_operands(pa), s0)


def _shifted(u, carry_ref, shift, j):
    n_state = (CONV_W - 1) * shift
    if shift == 1:
        out = pltpu.roll(u, j, 0)
        row = _iota(u.shape, 0)
        for i in range(j):
            out = jnp.where(row == i, carry_ref[n_state - j + i:n_state - j + i + 1, :], out)
        return out
    keep = u.shape[0] - j * shift
    return jnp.concatenate([carry_ref[n_state - j * shift:n_state, :], u[:keep]], axis=0)


def _post_body(x_ref, oa_ref, or_ref, pe_ref, wo_ref, gmp_ref, gfp_ref, wg_ref, wv_ref, cwg_ref, cwv_ref,
               cbg_ref, cbv_ref, wd_ref, gfo_ref, wple_ref, wgate_ref, gple_ref, c0g_ref, c0v_ref,
               y_ref, cng_ref, cnv_ref, cg_ref, cv_ref, *, shift, n_fchunks):
    ti = pl.program_id(1)
    n_state = (CONV_W - 1) * shift
    tm = x_ref.shape[0]

    @pl.when(ti == 0)
    def _():
        cg_ref[0:n_state, :] = c0g_ref[...]
        cv_ref[0:n_state, :] = c0v_ref[...]

    mix = _dot(jnp.concatenate([oa_ref[...], or_ref[...]], axis=1), wo_ref[...])
    x1 = x_ref[...] + _rms(mix, gmp_ref[...])
    h = _rms(x1, gfp_ref[...]).astype(BF16)
    fc = D_FF_PAD // n_fchunks
    acc = jnp.zeros((tm, D_MODEL), F32)
    for c in range(n_fchunks):
        cols = slice(c * fc, (c + 1) * fc)
        halves = []
        for w_ref, cw_ref, cb_ref, carry_ref in ((wg_ref, cwg_ref, cbg_ref, cg_ref),
                                                 (wv_ref, cwv_ref, cbv_ref, cv_ref)):
            u = _dot(h, w_ref[:, cols])
            carry = carry_ref.at[:, cols]
            conv = cb_ref[:, cols] + cw_ref[0:1, cols] * _shifted(u, carry, shift, 2)
            conv = conv + cw_ref[1:2, cols] * _shifted(u, carry, shift, 1)
            conv = conv + cw_ref[2:3, cols] * u
            carry_ref[0:n_state, cols] = u[tm - n_state:, :]
            halves.append(conv)
        act = jax.nn.gelu(halves[0], approximate=True) * halves[1]
        acc = acc + _dot(act.astype(BF16), wd_ref[cols, :])
    x2 = x1 + _rms(acc, gfo_ref[...])
    gate = jax.nn.sigmoid(_dot(x2.astype(BF16), wgate_ref[...]))
    ple = _dot(pe_ref[...].astype(BF16), wple_ref[...]) * gate
    y_ref[...] = x2 + _rms(ple, gple_ref[...])

    @pl.when(ti == pl.num_programs(1) - 1)
    def _():
        cng_ref[...] = cg_ref[0:n_state, :]
        cnv_ref[...] = cv_ref[0:n_state, :]


def _post(x, oa, orw, pe, wts, conv0_g, conv0_v, n_seq, shift):
    n = x.shape[0]
    rows_per_seq = n // n_seq
    tm = min(256, rows_per_seq) if shift == 1 else rows_per_seq
    nt = rows_per_seq // tm
    n_state = (CONV_W - 1) * shift
    n_carry = max(SUBLANES, n_state)
    row = lambda si, ti: (si * nt + ti, 0)
    state_spec = pl.BlockSpec((None, n_state, D_FF_PAD), lambda si, ti: (si, 0, 0))
    w_arrays = [wts[k] for k in ("w_o", "g_mix_post", "g_ffn_pre", "w_gate_up", "w_val_up", "cw_g", "cw_v",
                                 "cb_g", "cb_v", "w_down", "g_ffn_post", "w_ple", "w_ple_gate", "g_ple")]
    return pl.pallas_call(
        functools.partial(_post_body, shift=shift, n_fchunks=2),
        grid=(n_seq, nt),
        in_specs=[
            pl.BlockSpec((tm, D_MODEL), row),
            pl.BlockSpec((tm, D_ATT), row),
            pl.BlockSpec((tm, D_RWKV), row),
            pl.BlockSpec((tm, D_PLE), row),
        ] + [_const_spec(w.shape) for w in w_arrays] + [state_spec, state_spec],
        out_specs=(pl.BlockSpec((tm, D_MODEL), row), state_spec, state_spec),
        out_shape=(jax.ShapeDtypeStruct((n, D_MODEL), F32),
                   jax.ShapeDtypeStruct((n_seq, n_state, D_FF_PAD), F32),
                   jax.ShapeDtypeStruct((n_seq, n_state, D_FF_PAD), F32)),
        scratch_shapes=[pltpu.VMEM((n_carry, D_FF_PAD), F32)] * 2,
        compiler_params=_params(2),
        name="post",
    )(x, oa, orw, pe, *w_arrays, conv0_g, conv0_v)


def _post_weights(p):
    pad_c = D_FF_PAD - D_FF

    def halves(a):
        widths = [(0, 0)] * (a.ndim - 1) + [(0, pad_c)]
        return jnp.pad(a[..., :D_FF], widths), jnp.pad(a[..., D_FF:], widths)

    w_g, w_v = halves(p["w_up"].astype(BF16))
    cw_g, cw_v = halves(p["conv_w"])
    cb_g, cb_v = halves(p["conv_b"].reshape(1, 2 * D_FF))
    return dict(
        w_o=p["w_o"].astype(BF16),
        g_mix_post=p["g_mix_post"].reshape(1, D_MODEL),
        g_ffn_pre=p["g_ffn_pre"].reshape(1, D_MODEL),
        w_gate_up=w_g, w_val_up=w_v, cw_g=cw_g, cw_v=cw_v, cb_g=cb_g, cb_v=cb_v,
        w_down=jnp.pad(p["w_down"].astype(BF16), ((0, pad_c), (0, 0))),
        g_ffn_post=p["g_ffn_post"].reshape(1, D_MODEL),
        w_ple=p["w_ple"].astype(BF16),
        w_ple_gate=p["w_ple_gate"].astype(BF16),
        g_ple=p["g_ple"].reshape(1, D_MODEL),
    ), halves


def kernel(x_prompt, x_sample, p_prompt, p_sample, cache_k, cache_v, page_table, state_shift, state_wkv,
           state_conv, g_mix_pre, w_in, lam_q1, lam_k1, lam_q2, lam_k2, g_subln, mu_shift, w0, w2, a0, a2,
           g2, k_k, k_a, r_k, ln_x_w, ln_x_b, w_o, g_mix_post, g_ffn_pre, w_up, conv_w, conv_b, w_down,
           g_ffn_post, w_ple, w_ple_gate, g_ple):
    depth = w_in.shape[0]
    bp, sp, _ = x_prompt.shape
    bd, td, _ = x_sample.shape
    n_pool, page = cache_k.shape[1], cache_k.shape[2]
    xp = x_prompt.reshape(bp * sp, D_MODEL)
    xs = x_sample.reshape(bd * td, D_MODEL)
    t_pad = SUBLANES
    outs = {k: [] for k in ("kp", "vp", "ks", "vs", "shp", "wkp", "cvp", "shs", "wks", "cvs")}
    for l in range(depth):
        lam_init = 0.8 - 0.6 * math.exp(-0.3 * l)
        p = dict(mu_shift=mu_shift[l], w0=w0[l], w2=w2[l], a0=a0[l], a2=a2[l], g2=g2[l], k_k=k_k[l],
                 k_a=k_a[l], r_k=r_k[l], ln_x_w=ln_x_w[l], ln_x_b=ln_x_b[l], w_o=w_o[l],
                 g_mix_post=g_mix_post[l], g_ffn_pre=g_ffn_pre[l], w_up=w_up[l], conv_w=conv_w[l],
                 conv_b=conv_b[l], w_down=w_down[l], g_ffn_post=g_ffn_post[l], w_ple=w_ple[l],
                 w_ple_gate=w_ple_gate[l], g_ple=g_ple[l])
        w_in_bf = w_in[l].astype(BF16)
        g_pre = g_mix_pre[l].reshape(1, D_MODEL)
        lams = [z[l].reshape(1, A_DH) for z in (lam_q1, lam_k1, lam_q2, lam_k2)]
        gsub = g_subln[l].reshape(1, A_DV)
        pa = _rwkv_param_arrays(p)
        wts, halves = _post_weights(p)

        q, k, v, kb, vb, pr = _inproj(xp, g_pre, w_in_bf)
        oa = _attn_prompt(lams, gsub, q, kb, vb, bp, sp, lam_init)
        orw, wkv_p = _rwkv_prompt(pr, jnp.zeros((bp, N_SHIFT), F32),
                                  jnp.zeros((bp, R_HEADS, R_HEAD, R_HEAD), F32), pa, bp, sp)
        zc = jnp.zeros((bp, CONV_W - 1, D_FF_PAD), F32)
        xp, cng, cnv = _post(xp, oa, orw, p_prompt[l].reshape(bp * sp, D_PLE), wts, zc, zc, bp, 1)
        outs["kp"].append(k.reshape(bp, sp, A_HEADS, A_DK))
        outs["vp"].append(v.reshape(bp, sp, A_HEADS, A_DV))
        outs["shp"].append(pr.reshape(bp, sp, N_SHIFT)[:, sp - 1])
        outs["wkp"].append(wkv_p)
        outs["cvp"].append(jnp.concatenate([cng[..., :D_FF], cnv[..., :D_FF]], axis=-1))

        q, k, v, kb, vb, pr = _inproj(xs, g_pre, w_in_bf)
        oa = _attn_sample(lams, gsub, q.reshape(bd, td, D_ATT), k.reshape(bd, td, D_ATT),
                          v.reshape(bd, td, D_ATT), cache_k[l].reshape(n_pool, page, D_ATT),
                          cache_v[l].reshape(n_pool, page, D_ATT), page_table, lam_init)
        pr3 = pr.reshape(bd, td, N_SHIFT)
        prev3 = jnp.concatenate([state_shift[l][:, None, :], pr3[:, :td - 1]], axis=1)
        pad_t = ((0, 0), (0, t_pad - td), (0, 0))
        orw, wkv_s = _rwkv_sample(jnp.pad(pr3, pad_t).reshape(bd * t_pad, N_SHIFT),
                                  jnp.pad(prev3, pad_t).reshape(bd * t_pad, N_SHIFT),
                                  state_wkv[l], pa, bd, t_pad, td)
        orw = orw.reshape(bd, t_pad, D_RWKV)[:, :td]
        tmaj = lambda z: jnp.swapaxes(z, 0, 1).reshape(td * bd, z.shape[-1])
        c0g, c0v = halves(jnp.swapaxes(state_conv[l], 0, 1).reshape(1, (CONV_W - 1) * bd, 2 * D_FF))
        ys, cng, cnv = _post(tmaj(xs.reshape(bd, td, D_MODEL)), tmaj(oa), tmaj(orw), tmaj(p_sample[l]),
                             wts, c0g, c0v, 1, bd)
        xs = jnp.swapaxes(ys.reshape(td, bd, D_MODEL), 0, 1).reshape(bd * td, D_MODEL)
        cvs = jnp.concatenate([cng[..., :D_FF], cnv[..., :D_FF]], axis=-1).reshape(CONV_W - 1, bd, 2 * D_FF)
        outs["ks"].append(k.reshape(bd, td, A_HEADS, A_DK))
        outs["vs"].append(v.reshape(bd, td, A_HEADS, A_DV))
        outs["shs"].append(pr3[:, td - 1])
        outs["wks"].append(wkv_s)
        outs["cvs"].append(jnp.swapaxes(cvs, 0, 1))
    st = lambda key: jnp.stack(outs[key])
    return (xp.reshape(bp, sp, D_MODEL), xs.reshape(bd, td, D_MODEL), st("kp"), st("vp"), st("ks"), st("vs"),
            st("shp"), st("wkp"), st("cvp"), st("shs"), st("wks"), st("cvs"))
```

```python
import functools
import math

import jax
import jax.numpy as jnp
from jax import lax
from jax.experimental import pallas as pl
from jax.experimental.pallas import tpu as pltpu

F32 = jnp.float32
BF16 = jnp.bfloat16

D_MODEL = 1024
A_HEADS = 4
A_DH = 64
A_DK = 2 * A_DH
A_DV = 2 * A_DH
D_ATT = A_HEADS * A_DK
R_HEAD = 64
R_HEADS = 8
D_RWKV = R_HEADS * R_HEAD
LORA_W = 64
LORA_A = 64
LORA_G = 128
N_SHIFT = 3 * D_RWKV + LORA_W + LORA_A + LORA_G
N_IN = 3 * D_ATT + N_SHIFT
D_FF = 2752
CONV_W = 3
D_PLE = 256
NORM_EPS = 1e-6
GN_EPS = 64e-5
ATT_SCALE = A_DH ** -0.5

LANES = 128
SUBLANES = 8
VMEM_LIMIT_BYTES = 56 * 1024 * 1024

D_FF_PAD = ((D_FF + LANES - 1) // LANES) * LANES
PAIR = 2 * R_HEAD
N_PAIRS = D_RWKV // PAIR
X_COLS = LORA_W + LORA_A + LORA_G
CHUNK = 64


def _params(n_axes):
    return pltpu.CompilerParams(
        dimension_semantics=("arbitrary",) * n_axes,
        vmem_limit_bytes=VMEM_LIMIT_BYTES,
    )


def _const_spec(shape):
    zeros = (0,) * len(shape)
    return pl.BlockSpec(shape, lambda *_: zeros)


def _rms(x, g):
    return x * lax.rsqrt(jnp.mean(x * x, axis=-1, keepdims=True) + NORM_EPS) * g


def _dot(a, b):
    return jnp.dot(a, b, preferred_element_type=F32)


def _dot_nt(a, b):
    return lax.dot_general(a, b, (((1,), (1,)), ((), ())), preferred_element_type=F32)


def _split3(x):
    hi = x.astype(BF16)
    r1 = x - hi.astype(F32)
    mid = r1.astype(BF16)
    lo = (r1 - mid.astype(F32)).astype(BF16)
    return hi, mid, lo


def _dot_sel(x, sel, passes=3):
    out = None
    for part in _split3(x)[:passes]:
        t = _dot(part, sel)
        out = t if out is None else out + t
    return out


def _sel_dot(sel, x, passes=3):
    out = None
    for part in _split3(x)[:passes]:
        t = _dot(sel, part)
        out = t if out is None else out + t
    return out


def _sel_dot_nt(sel, x, passes=3):
    out = None
    for part in _split3(x)[:passes]:
        t = _dot_nt(sel, part)
        out = t if out is None else out + t
    return out


def _iota(shape, dim):
    return lax.broadcasted_iota(jnp.int32, shape, dim)


def _inproj_body(x_ref, g_ref, w_ref, q_ref, k_ref, v_ref, kb_ref, vb_ref, pr_ref):
    h = _rms(x_ref[...], g_ref[...]).astype(BF16)
    q = _dot(h, w_ref[:, 0:D_ATT])
    q_ref[...] = (q * ATT_SCALE).astype(BF16)
    k = _dot(h, w_ref[:, D_ATT:2 * D_ATT])
    kb_ref[...] = k.astype(BF16)
    v = _dot(h, w_ref[:, 2 * D_ATT:3 * D_ATT])
    vb_ref[...] = v.astype(BF16)
    for hd in range(A_HEADS):
        k_ref[:, hd, :] = k[:, hd * A_DK:(hd + 1) * A_DK]
        v_ref[:, hd, :] = v[:, hd * A_DV:(hd + 1) * A_DV]
    pr_ref[...] = _dot(h, w_ref[:, 3 * D_ATT:N_IN])


def _inproj(x, g, w_in_bf):
    n = x.shape[0]
    tm = min(512, n)
    row = lambda i: (i, 0)
    outs = (
        jax.ShapeDtypeStruct((n, D_ATT), BF16),
        jax.ShapeDtypeStruct((n, A_HEADS, A_DK), F32),
        jax.ShapeDtypeStruct((n, A_HEADS, A_DV), F32),
        jax.ShapeDtypeStruct((n, D_ATT), BF16),
        jax.ShapeDtypeStruct((n, D_ATT), BF16),
        jax.ShapeDtypeStruct((n, N_SHIFT), F32),
    )
    return pl.pallas_call(
        _inproj_body,
        grid=(n // tm,),
        in_specs=[
            pl.BlockSpec((tm, D_MODEL), row),
            _const_spec((1, D_MODEL)),
            _const_spec((D_MODEL, N_IN)),
        ],
        out_specs=tuple(pl.BlockSpec((tm,) + s.shape[1:], lambda i, nd=len(s.shape): (i,) + (0,) * (nd - 1))
                        for s in outs),
        out_shape=outs,
        compiler_params=_params(1),
        name="inproj",
    )(x, g, w_in_bf)


def _diff_lambda(lq1, lk1, lq2, lk2, lam_init):
    s1 = jnp.sum(lq1 * lk1, axis=-1, keepdims=True)
    s2 = jnp.sum(lq2 * lk2, axis=-1, keepdims=True)
    return jnp.exp(s1) - jnp.exp(s2) + lam_init


def _stack_maps(q):
    lane = _iota(q.shape, 1)
    zero = jnp.zeros_like(q)
    return jnp.concatenate([jnp.where(lane < A_DH, q, zero), jnp.where(lane >= A_DH, q, zero)], axis=0)


def _subln(o, gsub, lam_init):
    return _rms(o, gsub) * (1.0 - lam_init)


def _attn_prompt_body(lq1, lk1, lq2, lk2, gsub_ref, q_ref, k_ref, v_ref, o_ref, *, tq, tk, lam_init):
    qi = pl.program_id(1)
    lam = _diff_lambda(lq1[...], lk1[...], lq2[...], lk2[...], lam_init)
    n_kv = (qi * tq + tq + tk - 1) // tk
    qpos = qi * tq + _iota((2 * tq, tk), 0) % tq
    kcol = _iota((2 * tq, tk), 1)
    for h in range(A_HEADS):
        cols = slice(h * A_DK, (h + 1) * A_DK)
        qs = _stack_maps(q_ref[:, cols])

        def step(j, carry, qs=qs, cols=cols):
            m, l, acc = carry
            off = pl.multiple_of(j * tk, tk)
            kt = k_ref[pl.ds(off, tk), cols]
            vt = v_ref[pl.ds(off, tk), cols]
            s = _dot_nt(qs, kt)
            s = jnp.where(kcol + j * tk <= qpos, s, -jnp.inf)
            m_new = jnp.maximum(m, jnp.max(s, axis=-1, keepdims=True))
            p = jnp.exp(s - m_new)
            alpha = jnp.exp(m - m_new)
            l = alpha * l + jnp.sum(p, axis=-1, keepdims=True)
            acc = alpha * acc + _dot(p.astype(BF16), vt)
            return m_new, l, acc

        init = (
            jnp.full((2 * tq, 1), -jnp.inf, F32),
            jnp.zeros((2 * tq, 1), F32),
            jnp.zeros((2 * tq, A_DV), F32),
        )
        _, l, acc = lax.fori_loop(0, n_kv, step, init)
        o = acc / l
        o = o[:tq] - lam * o[tq:]
        o_ref[:, cols] = _subln(o, gsub_ref[...], lam_init).astype(o_ref.dtype)


def _attn_prompt(lams, gsub, q, kb, vb, b, s, lam_init):
    tq = min(256, s)
    tk = tq
    nq = s // tq
    lam_spec = _const_spec((1, A_DH))
    return pl.pallas_call(
        functools.partial(_attn_prompt_body, tq=tq, tk=tk, lam_init=lam_init),
        grid=(b, nq),
        in_specs=[
            lam_spec, lam_spec, lam_spec, lam_spec,
            _const_spec((1, A_DV)),
            pl.BlockSpec((tq, D_ATT), lambda bi, qi: (bi * nq + qi, 0)),
            pl.BlockSpec((s, D_ATT), lambda bi, qi: (bi, 0)),
            pl.BlockSpec((s, D_ATT), lambda bi, qi: (bi, 0)),
        ],
        out_specs=pl.BlockSpec((tq, D_ATT), lambda bi, qi: (bi * nq + qi, 0)),
        out_shape=jax.ShapeDtypeStruct((b * s, D_ATT), BF16),
        compiler_params=_params(2),
        name="attn_prompt",
    )(*lams, gsub, q, kb, vb)


def _attn_sample_body(pt_ref, lq1, lk1, lq2, lk2, gsub_ref, q_ref, kn_ref, vn_ref, *rest,
                      n_pages, t, lam_init):
    del pt_ref
    kp_refs = rest[:n_pages]
    vp_refs = rest[n_pages:2 * n_pages]
    o_ref = rest[2 * n_pages]
    lam = _diff_lambda(lq1[...], lk1[...], lq2[...], lk2[...], lam_init)
    nq = A_HEADS * t
    qs = _stack_maps(q_ref[...])
    q_head = (_iota((2 * nq, 1), 0) % nq) // t
    q_tok = _iota((2 * nq, 1), 0) % t
    width = kp_refs[0].shape[0]
    past_ok = (_iota((2 * nq, width), 1) % A_HEADS) == q_head
    new_lane = _iota((2 * nq, nq), 1)
    new_ok = ((new_lane % A_HEADS) == q_head) & ((new_lane // A_HEADS) <= q_tok)
    s_past = [jnp.where(past_ok, _dot_nt(qs, kp[...].astype(BF16)), -jnp.inf) for kp in kp_refs]
    s_new = jnp.where(new_ok, _dot_nt(qs, kn_ref[...].astype(BF16)), -jnp.inf)
    m = jnp.max(s_new, axis=-1, keepdims=True)
    for sp in s_past:
        m = jnp.maximum(m, jnp.max(sp, axis=-1, keepdims=True))
    p_new = jnp.exp(s_new - m)
    l = jnp.sum(p_new, axis=-1, keepdims=True)
    acc = _dot(p_new.astype(BF16), vn_ref[...].astype(BF16))
    for sp, vp in zip(s_past, vp_refs):
        p = jnp.exp(sp - m)
        l = l + jnp.sum(p, axis=-1, keepdims=True)
        acc = acc + _dot(p.astype(BF16), vp[...].astype(BF16))
    o = acc / l
    o = o[:nq] - lam * o[nq:]
    o_ref[...] = _subln(o, gsub_ref[...], lam_init).astype(o_ref.dtype)


def _attn_sample(lams, gsub, q, k_new, v_new, cache_k, cache_v, layer, page_table, t, lam_init):
    bd, n_pages = page_table.shape
    nq = A_HEADS * t
    rows = cache_k.shape[2]
    lam_spec = pl.BlockSpec((1, A_DH), lambda b, pt: (0, 0))
    tok_spec = pl.BlockSpec((nq, A_DK), lambda b, pt: (b, 0))

    def page_spec(p):
        return pl.BlockSpec((None, None, rows, A_DK), lambda b, pt, p=p: (layer, pt[b, p], 0, 0))

    grid_spec = pltpu.PrefetchScalarGridSpec(
        num_scalar_prefetch=1,
        grid=(bd,),
        in_specs=[lam_spec] * 4 + [pl.BlockSpec((1, A_DV), lambda b, pt: (0, 0)), tok_spec, tok_spec, tok_spec]
        + [page_spec(p) for p in range(n_pages)] * 2,
        out_specs=tok_spec,
    )
    return pl.pallas_call(
        functools.partial(_attn_sample_body, n_pages=n_pages, t=t, lam_init=lam_init),
        grid_spec=grid_spec,
        out_shape=jax.ShapeDtypeStruct((bd * nq, A_DV), BF16),
        compiler_params=_params(1),
        name="attn_sample",
    )(page_table, *lams, gsub, q, k_new, v_new, *([cache_k] * n_pages), *([cache_v] * n_pages))


def _head_blocks(n):
    return (_iota((n, n), 0) // R_HEAD) == (_iota((n, n), 1) // R_HEAD)


def _softplus(z):
    return jnp.maximum(z, 0.0) + jnp.log1p(jnp.exp(-jnp.abs(z)))


def _rwkv_tokens(cur, prev, prm, seg_ones):
    mix = {c: cur[c] + (prev[c] - cur[c]) * prm["mu_" + c] for c in ("r", "k", "v", "x")}
    r, k, v = mix["r"], mix["k"], mix["v"]
    xwa = mix["x"][:, :LORA_W + LORA_A]
    xg = mix["x"][:, LORA_W + LORA_A:]
    w_in = prm["w0"] + _dot(jnp.tanh(xwa).astype(BF16), prm["w2"])
    lw = -jnp.exp(-_softplus(-w_in) - 0.5)
    a = jax.nn.sigmoid(prm["a0"] + _dot(xwa.astype(BF16), prm["a2"]))
    g = _dot(jax.nn.sigmoid(xg).astype(BF16), prm["g2"])
    kk = k * prm["k_k"]
    nrm = jnp.sqrt(_dot_sel(kk * kk, seg_ones, passes=2))
    kk = kk / jnp.maximum(nrm, 1e-12)
    k2 = k * (1.0 + (a - 1.0) * prm["k_a"])
    bonus = _dot_sel(r * k2 * prm["r_k"], seg_ones, passes=2) * v
    return r, lw, k2, v, kk, a, g, bonus


def _rwkv_finish(y, bonus, g, prm, seg_ones):
    mean = _dot_sel(y, seg_ones, passes=2) * (1.0 / R_HEAD)
    d = y - mean
    var = _dot_sel(d * d, seg_ones, passes=2) * (1.0 / R_HEAD)
    yn = d * lax.rsqrt(var + GN_EPS) * prm["ln_w"] + prm["ln_b"]
    return (yn + bonus) * g


def _stack_heads(x):
    lane = _iota(x.shape, 1)
    zero = jnp.zeros_like(x)
    return jnp.concatenate([jnp.where(lane < R_HEAD, x, zero), jnp.where(lane >= R_HEAD, x, zero)], axis=0)


def _wkv_prepare(chunks):
    c = chunks[0][0].shape[0]
    n = 2 * c
    tril = (_iota((c, c), 1) <= _iota((c, c), 0)).astype(BF16)
    row = _iota((n, n), 0)
    col = _iota((n, n), 1)
    same = (row // c) == (col // c)
    strict = same & (col < row)
    incl = same & (col <= row)
    zero = jnp.zeros((n, n), F32)
    eye = (_iota((PAIR, PAIR), 0) == _iota((PAIR, PAIR), 1))
    eye_b = eye.astype(BF16)

    big_ls = [_sel_dot(tril, ch[1]) for ch in chunks]
    ops = []
    for (r, lw, k2, v, kk, a), big_l in zip(chunks, big_ls):
        l_end = big_l[c - 1:c, :]
        e_in = jnp.exp(big_l)
        e_ex = jnp.exp(big_l - lw)
        e_neg = jnp.exp(-big_l)
        e_rem = jnp.exp(l_end - big_l)
        b = kk * a
        a_t = _stack_heads(-kk * e_ex)
        r_t = _stack_heads(r * e_in)
        ops.append(dict(
            l_end=l_end, a_t=a_t, r_t=r_t, a_tb=a_t.astype(BF16), r_tb=r_t.astype(BF16),
            b_t=_stack_heads(b * e_neg).astype(BF16), k_t=_stack_heads(k2 * e_neg).astype(BF16),
            b_h=_stack_heads(b * e_rem).astype(BF16), k_h=_stack_heads(k2 * e_rem).astype(BF16),
            v_s=_stack_heads(v).astype(BF16)))
    m_ab = [jnp.where(strict, _dot_nt(o["a_tb"], o["b_t"]), zero) for o in ops]
    m_ak = [jnp.where(strict, _dot_nt(o["a_tb"], o["k_t"]), zero).astype(BF16) for o in ops]
    m_rb = [jnp.where(incl, _dot_nt(o["r_tb"], o["b_t"]), zero).astype(BF16) for o in ops]
    m_rk = [jnp.where(incl, _dot_nt(o["r_tb"], o["k_t"]), zero).astype(BF16) for o in ops]
    b_ht = [_dot_nt(eye_b, o["b_h"]).astype(BF16) for o in ops]
    k_ht = [_dot_nt(eye_b, o["k_h"]).astype(BF16) for o in ops]

    xs = [jnp.concatenate([o["a_t"], _dot(mk, o["v_s"])], axis=1) for o, mk in zip(ops, m_ak)]
    ps = m_ab
    steps = max(1, int(math.ceil(math.log2(c))))
    for j in range(steps):
        pbs = [p.astype(BF16) for p in ps]
        xs = [x + _dot(pb, x.astype(BF16)) for x, pb in zip(xs, pbs)]
        if j + 1 < steps:
            ps = [_dot(pb, pb) for pb in pbs]
    xbs = [x.astype(BF16) for x in xs]

    mgs = [_dot(bt, xb) for bt, xb in zip(b_ht, xbs)]
    kvs = [_dot(kt, o["v_s"]) for kt, o in zip(k_ht, ops)]
    qys = [_dot(mr, xb) for mr, xb in zip(m_rb, xbs)]
    rkv = [_dot(mr, o["v_s"]) for mr, o in zip(m_rk, ops)]
    out = []
    for o, mg, kv, qy, rk in zip(ops, mgs, kvs, qys, rkv):
        m_mat = jnp.where(eye, jnp.exp(o["l_end"]), 0.0) + mg[:, :PAIR]
        g_mat = mg[:, PAIR:] + kv
        q_mat = o["r_t"] + qy[:, :PAIR]
        y0 = qy[:, PAIR:] + rk
        out.append((q_mat.astype(BF16), y0, m_mat.astype(BF16), g_mat))
    return out


def _wkv_apply(prep, st):
    q_mat, y0, m_mat, g_mat = prep
    c = y0.shape[0] // 2
    stb = st.astype(BF16)
    y_st = _dot(q_mat, stb) + y0
    return y_st[:c] + y_st[c:], _dot(m_mat, stb) + g_mat


def _state_in(s0):
    x = s0.reshape(PAIR, R_HEAD)
    sel = (_iota((PAIR, R_HEAD), 0) % R_HEAD == _iota((PAIR, R_HEAD), 1)).astype(BF16)
    full = _sel_dot_nt(sel, x)
    return jnp.where(_head_blocks(PAIR), full, 0.0)


def _state_out(st):
    folded = st[:R_HEAD] + st[R_HEAD:]
    eye = (_iota((PAIR, PAIR), 0) == _iota((PAIR, PAIR), 1)).astype(BF16)
    return _sel_dot_nt(eye, folded).reshape(2, R_HEAD, R_HEAD)


_RWKV_PARAM_NAMES = ("mu_r", "mu_k", "mu_v", "mu_x", "w0", "w2", "a0", "a2", "g2", "k_k", "k_a", "r_k",
                     "ln_w", "ln_b")


def _rwkv_prompt_body(*refs, n_chunks):
    n_prm = len(_RWKV_PARAM_NAMES)
    pr = dict(zip(("r", "k", "v", "x"), refs[0:4]))
    sh = dict(zip(("r", "k", "v", "x"), refs[4:8]))
    prm = {nm: ref[...] for nm, ref in zip(_RWKV_PARAM_NAMES, refs[8:8 + n_prm])}
    s0_ref = refs[8 + n_prm]
    o_ref, s_out_ref = refs[9 + n_prm], refs[10 + n_prm]
    carry = dict(zip(("r", "k", "v", "x"), refs[11 + n_prm:15 + n_prm]))
    st_ref = refs[15 + n_prm]
    g = pl.program_id(2)
    tc = o_ref.shape[0]

    @pl.when(g == 0)
    def _():
        for c in carry:
            carry[c][...] = jnp.broadcast_to(sh[c][...], carry[c].shape)
        st_ref[...] = _state_in(s0_ref[...])

    cur = {c: pr[c][...] for c in pr}
    prev = {}
    for c in cur:
        first = _iota(cur[c].shape, 0) == 0
        prev[c] = jnp.where(first, carry[c][0:1, :], pltpu.roll(cur[c], 1, 0))
    for c in cur:
        carry[c][...] = jnp.broadcast_to(cur[c][tc - 1:tc, :], carry[c].shape)

    seg_ones = _head_blocks(PAIR).astype(BF16)
    r, lw, k2, v, kk, a, gate, bonus = _rwkv_tokens(cur, prev, prm, seg_ones)
    cl = tc // n_chunks
    chunks = [tuple(z[i * cl:(i + 1) * cl] for z in (r, lw, k2, v, kk, a)) for i in range(n_chunks)]
    st = st_ref[...]
    ys = []
    for prep in _wkv_prepare(chunks):
        y, st = _wkv_apply(prep, st)
        ys.append(y)
    st_ref[...] = st
    y = jnp.concatenate(ys, axis=0)
    o_ref[...] = _rwkv_finish(y, bonus, gate, prm, seg_ones).astype(o_ref.dtype)

    @pl.when(g == pl.num_programs(2) - 1)
    def _():
        s_out_ref[...] = _state_out(st)


def _rwkv_sample_body(*refs, n_seq, t_pad, t_valid):
    n_prm = len(_RWKV_PARAM_NAMES)
    pr = dict(zip(("r", "k", "v", "x"), refs[0:4]))
    pv = dict(zip(("r", "k", "v", "x"), refs[4:8]))
    prm = {nm: ref[...] for nm, ref in zip(_RWKV_PARAM_NAMES, refs[8:8 + n_prm])}
    s0_ref = refs[8 + n_prm]
    o_ref, s_out_ref = refs[9 + n_prm], refs[10 + n_prm]
    cur = {c: pr[c][...] for c in pr}
    prev = {c: pv[c][...] for c in pv}
    seg_ones = _head_blocks(PAIR).astype(BF16)
    r, lw, k2, v, kk, a, gate, bonus = _rwkv_tokens(cur, prev, prm, seg_ones)
    valid = (_iota(r.shape, 0) % t_pad) < t_valid
    zero = jnp.zeros_like(r)
    r, lw, k2, v, kk = (jnp.where(valid, z, zero) for z in (r, lw, k2, v, kk))
    chunks = [tuple(z[i * t_pad:(i + 1) * t_pad] for z in (r, lw, k2, v, kk, a)) for i in range(n_seq)]
    states = [_state_in(s0_ref[i]) for i in range(n_seq)]
    applied = [_wkv_apply(prep, st) for prep, st in zip(_wkv_prepare(chunks), states)]
    for i, (_, st) in enumerate(applied):
        s_out_ref[i] = _state_out(st)
    y = jnp.concatenate([y for y, _ in applied], axis=0)
    o_ref[...] = _rwkv_finish(y, bonus, gate, prm, seg_ones).astype(o_ref.dtype)


def _rwkv_param_arrays(p):
    z_w = jnp.zeros((LORA_A, D_RWKV), BF16)
    z_a = jnp.zeros((LORA_W, D_RWKV), BF16)
    return dict(
        mu=p["mu_shift"].reshape(1, N_SHIFT),
        w0=p["w0"].reshape(1, D_RWKV),
        w2=jnp.concatenate([p["w2"].astype(BF16), z_w], axis=0),
        a0=p["a0"].reshape(1, D_RWKV),
        a2=jnp.concatenate([z_a, p["a2"].astype(BF16)], axis=0),
        g2=p["g2"].astype(BF16),
        k_k=p["k_k"].reshape(1, D_RWKV),
        k_a=p["k_a"].reshape(1, D_RWKV),
        r_k=p["r_k"].reshape(1, D_RWKV),
        ln_w=p["ln_x_w"].reshape(1, D_RWKV),
        ln_b=p["ln_x_b"].reshape(1, D_RWKV),
    )


def _rwkv_param_specs(pair_of):
    n_pair_blk = D_RWKV // PAIR
    x_blk = (3 * D_RWKV) // X_COLS

    def vec(off):
        return pl.BlockSpec((1, PAIR), lambda *i: (0, off + pair_of(*i)))

    def mat(rows):
        return pl.BlockSpec((rows, PAIR), lambda *i: (0, pair_of(*i)))

    return [
        vec(0), vec(n_pair_blk), vec(2 * n_pair_blk),
        pl.BlockSpec((1, X_COLS), lambda *i: (0, x_blk)),
        vec(0), mat(LORA_W + LORA_A), vec(0), mat(LORA_W + LORA_A), mat(LORA_G),
        vec(0), vec(0), vec(0), vec(0), vec(0),
    ]


def _rwkv_param_operands(pa):
    return [pa["mu"], pa["mu"], pa["mu"], pa["mu"], pa["w0"], pa["w2"], pa["a0"], pa["a2"], pa["g2"],
            pa["k_k"], pa["k_a"], pa["r_k"], pa["ln_w"], pa["ln_b"]]


def _col_specs(rows, row_of, pair_of):
    n_pair_blk = D_RWKV // PAIR
    x_blk = (3 * D_RWKV) // X_COLS
    specs = [pl.BlockSpec((rows, PAIR), lambda *i, o=o: (row_of(*i), o * n_pair_blk + pair_of(*i)))
             for o in range(3)]
    specs.append(pl.BlockSpec((rows, X_COLS), lambda *i: (row_of(*i), x_blk)))
    return specs


def _rwkv_prompt(pr, shift0, s0, pa, b, s):
    tc = min(512, s)
    nt = s // tc
    n_chunks = max(1, tc // CHUNK)
    pair_of = lambda bi, pi, gi: pi
    row_of = lambda bi, pi, gi: bi * nt + gi
    n_pair_blk = D_RWKV // PAIR
    x_blk = (3 * D_RWKV) // X_COLS
    shift_specs = [pl.BlockSpec((None, 1, PAIR), lambda bi, pi, gi, o=o: (bi, 0, o * n_pair_blk + pi))
                   for o in range(3)]
    shift_specs.append(pl.BlockSpec((None, 1, X_COLS), lambda bi, pi, gi: (bi, 0, x_blk)))
    state_spec = pl.BlockSpec((None, 2, R_HEAD, R_HEAD), lambda bi, pi, gi: (bi, pi, 0, 0))
    shift3 = shift0.reshape(b, 1, N_SHIFT)
    return pl.pallas_call(
        functools.partial(_rwkv_prompt_body, n_chunks=n_chunks),
        grid=(b, N_PAIRS, nt),
        in_specs=_col_specs(tc, row_of, pair_of) + shift_specs + _rwkv_param_specs(pair_of) + [state_spec],
        out_specs=(pl.BlockSpec((tc, PAIR), lambda bi, pi, gi: (bi * nt + gi, pi)), state_spec),
        out_shape=(jax.ShapeDtypeStruct((b * s, D_RWKV), BF16),
                   jax.ShapeDtypeStruct((b, R_HEADS, R_HEAD, R_HEAD), F32)),
        scratch_shapes=[pltpu.VMEM((SUBLANES, PAIR), F32)] * 3 + [pltpu.VMEM((SUBLANES, X_COLS), F32),
                                                                  pltpu.VMEM((PAIR, PAIR), F32)],
        compiler_params=_params(3),
        name="rwkv_prompt",
    )(pr, pr, pr, pr, shift3, shift3, shift3, shift3, *_rwkv_param_operands(pa), s0)


def _rwkv_sample(pr_pad, prev_pad, s0, pa, bd, t_pad, t_valid):
    n_seq = min(8, bd)
    rows = n_seq * t_pad
    pair_of = lambda gi, pi: pi
    row_of = lambda gi, pi: gi
    state_spec = pl.BlockSpec((n_seq, 2, R_HEAD, R_HEAD), lambda gi, pi: (gi, pi, 0, 0))
    return pl.pallas_call(
        functools.partial(_rwkv_sample_body, n_seq=n_seq, t_pad=t_pad, t_valid=t_valid),
        grid=(bd // n_seq, N_PAIRS),
        in_specs=_col_specs(rows, row_of, pair_of) * 2 + _rwkv_param_specs(pair_of) + [state_spec],
        out_specs=(pl.BlockSpec((rows, PAIR), lambda gi, pi: (gi, pi)), state_spec),
        out_shape=(jax.ShapeDtypeStruct((bd * t_pad, D_RWKV), BF16),
                   jax.ShapeDtypeStruct((bd, R_HEADS, R_HEAD, R_HEAD), F32)),
        compiler_params=_params(2),
        name="rwkv_sample",
    )(pr_pad, pr_pad, pr_pad, pr_pad, prev_pad, prev_pad, prev_pad, prev_pad, *_rwkv_param_operands(pa), s0)


def _shifted(u, carry_ref, shift, j):
    n_state = (CONV_W - 1) * shift
    if shift == 1:
        out = pltpu.roll(u, j, 0)
        row = _iota(u.shape, 0)
        for i in range(j):
            out = jnp.where(row == i, carry_ref[n_state - j + i:n_state - j + i + 1, :], out)
        return out
    keep = u.shape[0] - j * shift
    return jnp.concatenate([carry_ref[n_state - j * shift:n_state, :], u[:keep]], axis=0)


def _post_body(x_ref, oa_ref, or_ref, pe_ref, wo_ref, gmp_ref, gfp_ref, wg_ref, wv_ref, cwg_ref, cwv_ref,
               cbg_ref, cbv_ref, wd_ref, gfo_ref, wple_ref, wgate_ref, gple_ref, c0g_ref, c0v_ref,
               y_ref, cng_ref, cnv_ref, cg_ref, cv_ref, *, shift, n_fchunks):
    ti = pl.program_id(1)
    n_state = (CONV_W - 1) * shift
    tm = x_ref.shape[0]

    @pl.when(ti == 0)
    def _():
        cg_ref[0:n_state, :] = c0g_ref[...]
        cv_ref[0:n_state, :] = c0v_ref[...]

    mix = _dot(jnp.concatenate([oa_ref[...], or_ref[...]], axis=1), wo_ref[...])
    x1 = x_ref[...] + _rms(mix, gmp_ref[...])
    h = _rms(x1, gfp_ref[...]).astype(BF16)
    fc = D_FF_PAD // n_fchunks
    acc = jnp.zeros((tm, D_MODEL), F32)
    for c in range(n_fchunks):
        cols = slice(c * fc, (c + 1) * fc)
        halves = []
        for w_ref, cw_ref, cb_ref, carry_ref in ((wg_ref, cwg_ref, cbg_ref, cg_ref),
                                                 (wv_ref, cwv_ref, cbv_ref, cv_ref)):
            u = _dot(h, w_ref[:, cols])
            carry = carry_ref.at[:, cols]
            conv = cb_ref[:, cols] + cw_ref[0:1, cols] * _shifted(u, carry, shift, 2)
            conv = conv + cw_ref[1:2, cols] * _shifted(u, carry, shift, 1)
            conv = conv + cw_ref[2:3, cols] * u
            carry_ref[0:n_state, cols] = u[tm - n_state:, :]
            halves.append(conv)
        act = jax.nn.gelu(halves[0], approximate=True) * halves[1]
        acc = acc + _dot(act.astype(BF16), wd_ref[cols, :])
    x2 = x1 + _rms(acc, gfo_ref[...])
    gate = jax.nn.sigmoid(_dot(x2.astype(BF16), wgate_ref[...]))
    ple = _dot(pe_ref[...].astype(BF16), wple_ref[...]) * gate
    y_ref[...] = x2 + _rms(ple, gple_ref[...])

    @pl.when(ti == pl.num_programs(1) - 1)
    def _():
        cng_ref[...] = cg_ref[0:n_state, :]
        cnv_ref[...] = cv_ref[0:n_state, :]


def _post(x, oa, orw, pe, wts, conv0_g, conv0_v, n_seq, shift):
    n = x.shape[0]
    rows_per_seq = n // n_seq
    tm = min(256, rows_per_seq) if shift == 1 else rows_per_seq
    nt = rows_per_seq // tm
    n_state = (CONV_W - 1) * shift
    n_carry = max(SUBLANES, n_state)
    row = lambda si, ti: (si * nt + ti, 0)
    state_spec = pl.BlockSpec((None, n_state, D_FF_PAD), lambda si, ti: (si, 0, 0))
    w_arrays = [wts[k] for k in ("w_o", "g_mix_post", "g_ffn_pre", "w_gate_up", "w_val_up", "cw_g", "cw_v",
                                 "cb_g", "cb_v", "w_down", "g_ffn_post", "w_ple", "w_ple_gate", "g_ple")]
    return pl.pallas_call(
        functools.partial(_post_body, shift=shift, n_fchunks=2),
        grid=(n_seq, nt),
        in_specs=[
            pl.BlockSpec((tm, D_MODEL), row),
            pl.BlockSpec((tm, D_ATT), row),
            pl.BlockSpec((tm, D_RWKV), row),
            pl.BlockSpec((tm, D_PLE), row),
        ] + [_const_spec(w.shape) for w in w_arrays] + [state_spec, state_spec],
        out_specs=(pl.BlockSpec((tm, D_MODEL), row), state_spec, state_spec),
        out_shape=(jax.ShapeDtypeStruct((n, D_MODEL), F32),
                   jax.ShapeDtypeStruct((n_seq, n_state, D_FF_PAD), F32),
                   jax.ShapeDtypeStruct((n_seq, n_state, D_FF_PAD), F32)),
        scratch_shapes=[pltpu.VMEM((n_carry, D_FF_PAD), F32)] * 2,
        compiler_params=_params(2),
        name="post",
    )(x, oa, orw, pe, *w_arrays, conv0_g, conv0_v)


def _post_weights(p):
    pad_c = D_FF_PAD - D_FF

    def halves(a):
        widths = [(0, 0)] * (a.ndim - 1) + [(0, pad_c)]
        return jnp.pad(a[..., :D_FF], widths), jnp.pad(a[..., D_FF:], widths)

    w_g, w_v = halves(p["w_up"].astype(BF16))
    cw_g, cw_v = halves(p["conv_w"])
    cb_g, cb_v = halves(p["conv_b"].reshape(1, 2 * D_FF))
    return dict(
        w_o=p["w_o"].astype(BF16),
        g_mix_post=p["g_mix_post"].reshape(1, D_MODEL),
        g_ffn_pre=p["g_ffn_pre"].reshape(1, D_MODEL),
        w_gate_up=w_g, w_val_up=w_v, cw_g=cw_g, cw_v=cw_v, cb_g=cb_g, cb_v=cb_v,
        w_down=jnp.pad(p["w_down"].astype(BF16), ((0, pad_c), (0, 0))),
        g_ffn_post=p["g_ffn_post"].reshape(1, D_MODEL),
        w_ple=p["w_ple"].astype(BF16),
        w_ple_gate=p["w_ple_gate"].astype(BF16),
        g_ple=p["g_ple"].reshape(1, D_MODEL),
    ), halves


def kernel(x_prompt, x_sample, p_prompt, p_sample, cache_k, cache_v, page_table, state_shift, state_wkv,
           state_conv, g_mix_pre, w_in, lam_q1, lam_k1, lam_q2, lam_k2, g_subln, mu_shift, w0, w2, a0, a2,
           g2, k_k, k_a, r_k, ln_x_w, ln_x_b, w_o, g_mix_post, g_ffn_pre, w_up, conv_w, conv_b, w_down,
           g_ffn_post, w_ple, w_ple_gate, g_ple):
    depth = w_in.shape[0]
    bp, sp, _ = x_prompt.shape
    bd, td, _ = x_sample.shape
    n_pool, page = cache_k.shape[1], cache_k.shape[2]
    xp = x_prompt.reshape(bp * sp, D_MODEL)
    xs = x_sample.reshape(bd * td, D_MODEL)
    t_pad = SUBLANES
    outs = {k: [] for k in ("kp", "vp", "ks", "vs", "shp", "wkp", "cvp", "shs", "wks", "cvs")}
    for l in range(depth):
        lam_init = 0.8 - 0.6 * math.exp(-0.3 * l)
        p = dict(mu_shift=mu_shift[l], w0=w0[l], w2=w2[l], a0=a0[l], a2=a2[l], g2=g2[l], k_k=k_k[l],
                 k_a=k_a[l], r_k=r_k[l], ln_x_w=ln_x_w[l], ln_x_b=ln_x_b[l], w_o=w_o[l],
                 g_mix_post=g_mix_post[l], g_ffn_pre=g_ffn_pre[l], w_up=w_up[l], conv_w=conv_w[l],
                 conv_b=conv_b[l], w_down=w_down[l], g_ffn_post=g_ffn_post[l], w_ple=w_ple[l],
                 w_ple_gate=w_ple_gate[l], g_ple=g_ple[l])
        w_in_bf = w_in[l].astype(BF16)
        g_pre = g_mix_pre[l].reshape(1, D_MODEL)
        lams = [z[l].reshape(1, A_DH) for z in (lam_q1, lam_k1, lam_q2, lam_k2)]
        gsub = g_subln[l].reshape(1, A_DV)
        pa = _rwkv_param_arrays(p)
        wts, halves = _post_weights(p)

        q, k, v, kb, vb, pr = _inproj(xp, g_pre, w_in_bf)
        oa = _attn_prompt(lams, gsub, q, kb, vb, bp, sp, lam_init)
        orw, wkv_p = _rwkv_prompt(pr, jnp.zeros((bp, N_SHIFT), F32),
                                  jnp.zeros((bp, R_HEADS, R_HEAD, R_HEAD), F32), pa, bp, sp)
        zc = jnp.zeros((bp, CONV_W - 1, D_FF_PAD), F32)
        xp, cng, cnv = _post(xp, oa, orw, p_prompt[l].reshape(bp * sp, D_PLE), wts, zc, zc, bp, 1)
        outs["kp"].append(k.reshape(bp, sp, A_HEADS, A_DK))
        outs["vp"].append(v.reshape(bp, sp, A_HEADS, A_DV))
        outs["shp"].append(pr.reshape(bp, sp, N_SHIFT)[:, sp - 1])
        outs["wkp"].append(wkv_p)
        outs["cvp"].append(jnp.concatenate([cng[..., :D_FF], cnv[..., :D_FF]], axis=-1))

        q, k, v, kb, vb, pr = _inproj(xs, g_pre, w_in_bf)
        q_ht = jnp.swapaxes(q.reshape(bd, td, A_HEADS, A_DK), 1, 2).reshape(bd * A_HEADS * td, A_DK)
        oa = _attn_sample(lams, gsub, q_ht, k.reshape(bd * td * A_HEADS, A_DK), v.reshape(bd * td * A_HEADS, A_DV),
                          cache_k.reshape(depth, n_pool, page * A_HEADS, A_DK),
                          cache_v.reshape(depth, n_pool, page * A_HEADS, A_DV), l, page_table, td, lam_init)
        oa = jnp.swapaxes(oa.reshape(bd, A_HEADS, td, A_DV), 1, 2).reshape(bd, td, D_ATT)
        pr3 = pr.reshape(bd, td, N_SHIFT)
        prev3 = jnp.concatenate([state_shift[l][:, None, :], pr3[:, :td - 1]], axis=1)
        pad_t = ((0, 0), (0, t_pad - td), (0, 0))
        orw, wkv_s = _rwkv_sample(jnp.pad(pr3, pad_t).reshape(bd * t_pad, N_SHIFT),
                                  jnp.pad(prev3, pad_t).reshape(bd * t_pad, N_SHIFT),
                                  state_wkv[l], pa, bd, t_pad, td)
        orw = orw.reshape(bd, t_pad, D_RWKV)[:, :td]
        tmaj = lambda z: jnp.swapaxes(z, 0, 1).reshape(td * bd, z.shape[-1])
        c0g, c0v = halves(jnp.swapaxes(state_conv[l], 0, 1).reshape(1, (CONV_W - 1) * bd, 2 * D_FF))
        ys, cng, cnv = _post(tmaj(xs.reshape(bd, td, D_MODEL)), tmaj(oa), tmaj(orw), tmaj(p_sample[l]),
                             wts, c0g, c0v, 1, bd)
        xs = jnp.swapaxes(ys.reshape(td, bd, D_MODEL), 0, 1).reshape(bd * td, D_MODEL)
        cvs = jnp.concatenate([cng[..., :D_FF], cnv[..., :D_FF]], axis=-1).reshape(CONV_W - 1, bd, 2 * D_FF)
        outs["ks"].append(k.reshape(bd, td, A_HEADS, A_DK))
        outs["vs"].append(v.reshape(bd, td, A_HEADS, A_DV))
        outs["shs"].append(pr3[:, td - 1])
        outs["wks"].append(wkv_s)
        outs["cvs"].append(jnp.swapaxes(cvs, 0, 1))
    st = lambda key: jnp.stack(outs[key])
    return (xp.reshape(bp, sp, D_MODEL), xs.reshape(bd, td, D_MODEL), st("kp"), st("vp"), st("ks"), st("vs"),
            st("shp"), st("wkp"), st("cvp"), st("shs"), st("wks"), st("cvs"))
```

```python
import functools
import math

import jax
import jax.numpy as jnp
from jax import lax
from jax.experimental import pallas as pl
from jax.experimental.pallas import tpu as pltpu

F32 = jnp.float32
BF16 = jnp.bfloat16

D_MODEL = 1024
A_HEADS = 4
A_DH = 64
A_DK = 2 * A_DH
A_DV = 2 * A_DH
D_ATT = A_HEADS * A_DK
R_HEAD = 64
R_HEADS = 8
D_RWKV = R_HEADS * R_HEAD
LORA_W = 64
LORA_A = 64
LORA_G = 128
N_SHIFT = 3 * D_RWKV + LORA_W + LORA_A + LORA_G
N_IN = 3 * D_ATT + N_SHIFT
D_FF = 2752
CONV_W = 3
D_PLE = 256
NORM_EPS = 1e-6
GN_EPS = 64e-5
ATT_SCALE = A_DH ** -0.5
LOG2E = math.log2(math.e)

LANES = 128
SUBLANES = 8
VMEM_LIMIT_BYTES = 56 * 1024 * 1024

D_FF_PAD = ((D_FF + LANES - 1) // LANES) * LANES
PAIR = 2 * R_HEAD
N_PAIRS = D_RWKV // PAIR
X_COLS = LORA_W + LORA_A + LORA_G
CHUNK = 64
ATTN_HEADS_PER_LOOP = 4


def _params(n_axes):
    return pltpu.CompilerParams(
        dimension_semantics=("arbitrary",) * n_axes,
        vmem_limit_bytes=VMEM_LIMIT_BYTES,
    )


def _const_spec(shape):
    zeros = (0,) * len(shape)
    return pl.BlockSpec(shape, lambda *_: zeros)


def _rms(x, g):
    return x * lax.rsqrt(jnp.mean(x * x, axis=-1, keepdims=True) + NORM_EPS) * g


def _dot(a, b):
    return jnp.dot(a, b, preferred_element_type=F32)


def _dot_nt(a, b):
    return lax.dot_general(a, b, (((1,), (1,)), ((), ())), preferred_element_type=F32)


def _split3(x):
    hi = x.astype(BF16)
    r1 = x - hi.astype(F32)
    mid = r1.astype(BF16)
    lo = (r1 - mid.astype(F32)).astype(BF16)
    return hi, mid, lo


def _dot_sel(x, sel, passes=3):
    out = None
    for part in _split3(x)[:passes]:
        t = _dot(part, sel)
        out = t if out is None else out + t
    return out


def _sel_dot(sel, x, passes=3):
    out = None
    for part in _split3(x)[:passes]:
        t = _dot(sel, part)
        out = t if out is None else out + t
    return out


def _sel_dot_nt(sel, x, passes=3):
    out = None
    for part in _split3(x)[:passes]:
        t = _dot_nt(sel, part)
        out = t if out is None else out + t
    return out


def _iota(shape, dim):
    return lax.broadcasted_iota(jnp.int32, shape, dim)


def _inproj_body(x_ref, g_ref, w_ref, q_ref, k_ref, v_ref, kb_ref, vb_ref, pr_ref):
    h = _rms(x_ref[...], g_ref[...]).astype(BF16)
    q = _dot(h, w_ref[:, 0:D_ATT])
    q_ref[...] = (q * (ATT_SCALE * LOG2E)).astype(BF16)
    k = _dot(h, w_ref[:, D_ATT:2 * D_ATT])
    kb_ref[...] = k.astype(BF16)
    v = _dot(h, w_ref[:, 2 * D_ATT:3 * D_ATT])
    vb_ref[...] = v.astype(BF16)
    for hd in range(A_HEADS):
        k_ref[:, hd, :] = k[:, hd * A_DK:(hd + 1) * A_DK]
        v_ref[:, hd, :] = v[:, hd * A_DV:(hd + 1) * A_DV]
    pr_ref[...] = _dot(h, w_ref[:, 3 * D_ATT:N_IN])


def _inproj(x, g, w_in_bf):
    n = x.shape[0]
    tm = min(512, n)
    row = lambda i: (i, 0)
    outs = (
        jax.ShapeDtypeStruct((n, D_ATT), BF16),
        jax.ShapeDtypeStruct((n, A_HEADS, A_DK), F32),
        jax.ShapeDtypeStruct((n, A_HEADS, A_DV), F32),
        jax.ShapeDtypeStruct((n, D_ATT), BF16),
        jax.ShapeDtypeStruct((n, D_ATT), BF16),
        jax.ShapeDtypeStruct((n, N_SHIFT), F32),
    )
    return pl.pallas_call(
        _inproj_body,
        grid=(n // tm,),
        in_specs=[
            pl.BlockSpec((tm, D_MODEL), row),
            _const_spec((1, D_MODEL)),
            _const_spec((D_MODEL, N_IN)),
        ],
        out_specs=tuple(pl.BlockSpec((tm,) + s.shape[1:], lambda i, nd=len(s.shape): (i,) + (0,) * (nd - 1))
                        for s in outs),
        out_shape=outs,
        compiler_params=_params(1),
        name="inproj",
    )(x, g, w_in_bf)


def _diff_lambda(lq1, lk1, lq2, lk2, lam_init):
    s1 = jnp.sum(lq1 * lk1, axis=-1, keepdims=True)
    s2 = jnp.sum(lq2 * lk2, axis=-1, keepdims=True)
    return jnp.exp(s1) - jnp.exp(s2) + lam_init


def _stack_maps(q):
    lane = _iota(q.shape, 1)
    zero = jnp.zeros_like(q)
    return jnp.concatenate([jnp.where(lane < A_DH, q, zero), jnp.where(lane >= A_DH, q, zero)], axis=0)


def _subln(o, gsub, lam_init):
    return _rms(o, gsub) * (1.0 - lam_init)


def _attn_prompt_body(lq1, lk1, lq2, lk2, gsub_ref, q_ref, k_ref, v_ref, o_ref, *, tq, lam_init):
    qi = pl.program_id(1)
    lam = _diff_lambda(lq1[...], lk1[...], lq2[...], lk2[...], lam_init)
    heads = [slice(h * A_DK, (h + 1) * A_DK) for h in range(A_HEADS)]
    qs = [_stack_maps(q_ref[:, cols]) for cols in heads]
    causal = _iota((2 * tq, tq), 1) <= _iota((2 * tq, tq), 0) % tq

    def tile(j, carry, group, diagonal):
        off = pl.multiple_of(j * tq, tq)
        out = []
        for h, (m, l, acc) in zip(group, carry):
            s = _dot_nt(qs[h], k_ref[pl.ds(off, tq), heads[h]])
            if diagonal:
                s = jnp.where(causal, s, -jnp.inf)
            m_new = jnp.maximum(m, jnp.max(s, axis=-1, keepdims=True))
            p = jnp.exp2(s - m_new)
            alpha = jnp.exp2(m - m_new)
            l = alpha * l + jnp.sum(p, axis=-1, keepdims=True)
            acc = alpha * acc + _dot(p.astype(BF16), v_ref[pl.ds(off, tq), heads[h]])
            out.append((m_new, l, acc))
        return tuple(out)

    for first in range(0, A_HEADS, ATTN_HEADS_PER_LOOP):
        group = tuple(range(first, first + ATTN_HEADS_PER_LOOP))
        init = tuple((jnp.full((2 * tq, 1), -jnp.inf, F32), jnp.zeros((2 * tq, 1), F32),
                      jnp.zeros((2 * tq, A_DV), F32)) for _ in group)
        carry = lax.fori_loop(0, qi, lambda j, c, group=group: tile(j, c, group, False), init)
        carry = tile(qi, carry, group, True)
        for h, (_, l, acc) in zip(group, carry):
            o = acc / l
            o = o[:tq] - lam * o[tq:]
            o_ref[:, heads[h]] = _subln(o, gsub_ref[...], lam_init).astype(o_ref.dtype)


def _attn_prompt(lams, gsub, q, kb, vb, b, s, lam_init):
    tq = min(256, s)
    nq = s // tq
    lam_spec = _const_spec((1, A_DH))
    return pl.pallas_call(
        functools.partial(_attn_prompt_body, tq=tq, lam_init=lam_init),
        grid=(b, nq),
        in_specs=[
            lam_spec, lam_spec, lam_spec, lam_spec,
            _const_spec((1, A_DV)),
            pl.BlockSpec((tq, D_ATT), lambda bi, qi: (bi * nq + qi, 0)),
            pl.BlockSpec((s, D_ATT), lambda bi, qi: (bi, 0)),
            pl.BlockSpec((s, D_ATT), lambda bi, qi: (bi, 0)),
        ],
        out_specs=pl.BlockSpec((tq, D_ATT), lambda bi, qi: (bi * nq + qi, 0)),
        out_shape=jax.ShapeDtypeStruct((b * s, D_ATT), BF16),
        compiler_params=_params(2),
        name="attn_prompt",
    )(*lams, gsub, q, kb, vb)


def _attn_sample_body(pt_ref, lq1, lk1, lq2, lk2, gsub_ref, q_ref, kn_ref, vn_ref, *rest,
                      n_pages, t, lam_init):
    del pt_ref
    kp_refs = rest[:n_pages]
    vp_refs = rest[n_pages:2 * n_pages]
    o_ref = rest[2 * n_pages]
    lam = _diff_lambda(lq1[...], lk1[...], lq2[...], lk2[...], lam_init)
    nq = A_HEADS * t
    qs = _stack_maps(q_ref[...])
    q_head = (_iota((2 * nq, 1), 0) % nq) // t
    q_tok = _iota((2 * nq, 1), 0) % t
    width = kp_refs[0].shape[0]
    past_ok = (_iota((2 * nq, width), 1) % A_HEADS) == q_head
    new_lane = _iota((2 * nq, nq), 1)
    new_ok = ((new_lane % A_HEADS) == q_head) & ((new_lane // A_HEADS) <= q_tok)
    s_past = [jnp.where(past_ok, _dot_nt(qs, kp[...].astype(BF16)), -jnp.inf) for kp in kp_refs]
    s_new = jnp.where(new_ok, _dot_nt(qs, kn_ref[...].astype(BF16)), -jnp.inf)
    m = jnp.max(s_new, axis=-1, keepdims=True)
    for sp in s_past:
        m = jnp.maximum(m, jnp.max(sp, axis=-1, keepdims=True))
    p_new = jnp.exp2(s_new - m)
    l = jnp.sum(p_new, axis=-1, keepdims=True)
    acc = _dot(p_new.astype(BF16), vn_ref[...].astype(BF16))
    for sp, vp in zip(s_past, vp_refs):
        p = jnp.exp2(sp - m)
        l = l + jnp.sum(p, axis=-1, keepdims=True)
        acc = acc + _dot(p.astype(BF16), vp[...].astype(BF16))
    o = acc / l
    o = o[:nq] - lam * o[nq:]
    o_ref[...] = _subln(o, gsub_ref[...], lam_init).astype(o_ref.dtype)


def _attn_sample(lams, gsub, q, k_new, v_new, cache_k, cache_v, layer, page_table, t, lam_init):
    bd, n_pages = page_table.shape
    nq = A_HEADS * t
    rows = cache_k.shape[2]
    lam_spec = pl.BlockSpec((1, A_DH), lambda b, pt: (0, 0))
    tok_spec = pl.BlockSpec((nq, A_DK), lambda b, pt: (b, 0))

    def page_spec(p):
        return pl.BlockSpec((None, None, rows, A_DK), lambda b, pt, p=p: (layer, pt[b, p], 0, 0))

    grid_spec = pltpu.PrefetchScalarGridSpec(
        num_scalar_prefetch=1,
        grid=(bd,),
        in_specs=[lam_spec] * 4 + [pl.BlockSpec((1, A_DV), lambda b, pt: (0, 0)), tok_spec, tok_spec, tok_spec]
        + [page_spec(p) for p in range(n_pages)] * 2,
        out_specs=tok_spec,
    )
    return pl.pallas_call(
        functools.partial(_attn_sample_body, n_pages=n_pages, t=t, lam_init=lam_init),
        grid_spec=grid_spec,
        out_shape=jax.ShapeDtypeStruct((bd * nq, A_DV), BF16),
        compiler_params=_params(1),
        name="attn_sample",
    )(page_table, *lams, gsub, q, k_new, v_new, *([cache_k] * n_pages), *([cache_v] * n_pages))


def _head_blocks(n):
    return (_iota((n, n), 0) // R_HEAD) == (_iota((n, n), 1) // R_HEAD)


def _softplus(z):
    return jnp.maximum(z, 0.0) + jnp.log1p(jnp.exp(-jnp.abs(z)))


def _rwkv_tokens(cur, prev, prm, seg_ones):
    mix = {c: cur[c] + (prev[c] - cur[c]) * prm["mu_" + c] for c in ("r", "k", "v", "x")}
    r, k, v = mix["r"], mix["k"], mix["v"]
    xwa = mix["x"][:, :LORA_W + LORA_A]
    xg = mix["x"][:, LORA_W + LORA_A:]
    w_in = prm["w0"] + _dot(jnp.tanh(xwa).astype(BF16), prm["w2"])
    lw = -jnp.exp(-_softplus(-w_in) - 0.5)
    a = jax.nn.sigmoid(prm["a0"] + _dot(xwa.astype(BF16), prm["a2"]))
    g = _dot(jax.nn.sigmoid(xg).astype(BF16), prm["g2"])
    kk = k * prm["k_k"]
    nrm = jnp.sqrt(_dot_sel(kk * kk, seg_ones, passes=1))
    kk = kk / jnp.maximum(nrm, 1e-12)
    k2 = k * (1.0 + (a - 1.0) * prm["k_a"])
    bonus = _dot_sel(r * k2 * prm["r_k"], seg_ones, passes=1) * v
    return r, lw, k2, v, kk, a, g, bonus


def _rwkv_finish(y, bonus, g, prm, seg_ones):
    mean = _dot_sel(y, seg_ones, passes=1) * (1.0 / R_HEAD)
    d = y - mean
    var = _dot_sel(d * d, seg_ones, passes=1) * (1.0 / R_HEAD)
    yn = d * lax.rsqrt(var + GN_EPS) * prm["ln_w"] + prm["ln_b"]
    return (yn + bonus) * g


def _stack_heads(x):
    lane = _iota(x.shape, 1)
    zero = jnp.zeros_like(x)
    return jnp.concatenate([jnp.where(lane < R_HEAD, x, zero), jnp.where(lane >= R_HEAD, x, zero)], axis=0)


def _wkv_prepare_stages(chunks):
    c = chunks[0][0].shape[0]
    n = 2 * c
    tril = (_iota((c, c), 1) <= _iota((c, c), 0)).astype(BF16)
    row = _iota((n, n), 0)
    col = _iota((n, n), 1)
    same = (row // c) == (col // c)
    strict = same & (col < row)
    incl = same & (col <= row)
    zero = jnp.zeros((n, n), F32)
    eye = (_iota((PAIR, PAIR), 0) == _iota((PAIR, PAIR), 1))
    eye_b = eye.astype(BF16)

    steps = max(1, int(math.ceil(math.log2(c))))
    eye_n = jnp.where(row == col, 1.0, 0.0)
    env = {}

    def cumsum():
        env["big_l"] = [_sel_dot(tril, ch[1], passes=2) for ch in chunks]

    def operands():
        ops = []
        for (r, lw, k2, v, kk, a), big_l in zip(chunks, env["big_l"]):
            l_end = big_l[c - 1:c, :]
            e_in = jnp.exp(big_l)
            e_ex = jnp.exp(big_l - lw)
            e_neg = jnp.exp(-big_l)
            e_rem = jnp.exp(l_end - big_l)
            b = kk * a
            a_t = _stack_heads(-kk * e_ex)
            r_t = _stack_heads(r * e_in)
            ops.append(dict(
                l_end=l_end, a_t=a_t, r_t=r_t, a_tb=a_t.astype(BF16), r_tb=r_t.astype(BF16),
                b_t=_stack_heads(b * e_neg).astype(BF16), k_t=_stack_heads(k2 * e_neg).astype(BF16),
                b_h=_stack_heads(b * e_rem).astype(BF16), k_h=_stack_heads(k2 * e_rem).astype(BF16),
                v_s=_stack_heads(v).astype(BF16)))
        env["ops"] = ops

    def pair(name, lhs, rhs, mask):
        def run():
            env[name] = [jnp.where(mask, _dot_nt(o[lhs], o[rhs]), zero) for o in env["ops"]]
        return run

    def transposes():
        env["b_ht"] = [_dot_nt(eye_b, o["b_h"]).astype(BF16) for o in env["ops"]]
        env["k_ht"] = [_dot_nt(eye_b, o["k_h"]).astype(BF16) for o in env["ops"]]

    def rhs():
        env["x"] = [jnp.concatenate([o["a_t"], _dot(mk.astype(BF16), o["v_s"])], axis=1).astype(BF16)
                    for o, mk in zip(env["ops"], env["m_ak"])]
        env["pb"] = [p.astype(BF16) for p in env["m_ab"]]
        env["t"] = [eye_n + p for p in env["m_ab"]]

    def square():
        env["pb"] = [_dot(pb, pb).astype(BF16) for pb in env["pb"]]

    def extend():
        env["t"] = [t + _dot(t.astype(BF16), pb) for t, pb in zip(env["t"], env["pb"])]

    def solve():
        env["w"] = [_dot(t.astype(BF16), x).astype(BF16) for t, x in zip(env["t"], env["x"])]

    def state_side():
        env["mg"] = [_dot(bt, w) for bt, w in zip(env["b_ht"], env["w"])]
        env["kv"] = [_dot(kt, o["v_s"]) for kt, o in zip(env["k_ht"], env["ops"])]

    def output_side():
        env["qy"] = [_dot(mr.astype(BF16), w) for mr, w in zip(env["m_rb"], env["w"])]
        env["rkv"] = [_dot(mr.astype(BF16), o["v_s"]) for mr, o in zip(env["m_rk"], env["ops"])]

    def finish():
        out = []
        for o, mg, kv, qy, rk in zip(env["ops"], env["mg"], env["kv"], env["qy"], env["rkv"]):
            m_mat = jnp.where(eye, jnp.exp(o["l_end"]), 0.0) + mg[:, :PAIR]
            g_mat = mg[:, PAIR:] + kv
            q_mat = o["r_t"] + qy[:, :PAIR]
            y0 = qy[:, PAIR:] + rk
            out.append((q_mat.astype(BF16), y0, m_mat.astype(BF16), g_mat))
        env["out"] = out

    stages = [cumsum, operands, pair("m_ab", "a_tb", "b_t", strict), pair("m_ak", "a_tb", "k_t", strict),
              pair("m_rb", "r_tb", "b_t", incl), pair("m_rk", "r_tb", "k_t", incl), transposes, rhs]
    for _ in range(steps - 1):
        stages += [square, extend]
    stages += [solve, state_side, output_side, finish]
    return stages, env


def _wkv_prepare(chunks):
    stages, env = _wkv_prepare_stages(chunks)
    for stage in stages:
        stage()
    return env["out"]


def _wkv_apply(prep, st):
    q_mat, y0, m_mat, g_mat = prep
    c = y0.shape[0] // 2
    stb = st.astype(BF16)
    y_st = _dot(q_mat, stb) + y0
    return y_st[:c] + y_st[c:], _dot(m_mat, stb) + g_mat


def _state_in(s0):
    x = s0.reshape(PAIR, R_HEAD)
    sel = (_iota((PAIR, R_HEAD), 0) % R_HEAD == _iota((PAIR, R_HEAD), 1)).astype(BF16)
    full = _sel_dot_nt(sel, x)
    return jnp.where(_head_blocks(PAIR), full, 0.0)


def _state_out(st):
    folded = st[:R_HEAD] + st[R_HEAD:]
    eye = (_iota((PAIR, PAIR), 0) == _iota((PAIR, PAIR), 1)).astype(BF16)
    return _sel_dot_nt(eye, folded).reshape(2, R_HEAD, R_HEAD)


_RWKV_PARAM_NAMES = ("mu_r", "mu_k", "mu_v", "mu_x", "w0", "w2", "a0", "a2", "g2", "k_k", "k_a", "r_k",
                     "ln_w", "ln_b")


def _rwkv_prompt_body(*refs, n_chunks):
    n_prm = len(_RWKV_PARAM_NAMES)
    pr = dict(zip(("r", "k", "v", "x"), refs[0:4]))
    sh = dict(zip(("r", "k", "v", "x"), refs[4:8]))
    prm = {nm: ref[...] for nm, ref in zip(_RWKV_PARAM_NAMES, refs[8:8 + n_prm])}
    s0_ref = refs[8 + n_prm]
    o_ref, s_out_ref = refs[9 + n_prm], refs[10 + n_prm]
    scratch = refs[11 + n_prm:]
    carry = dict(zip(("r", "k", "v", "x"), scratch[0:4]))
    st_ref, q_sc, y0_sc, m_sc, g_sc, bonus_sc, gate_sc = scratch[4:11]
    g = pl.program_id(2)
    tc = bonus_sc.shape[0]
    seg_ones = _head_blocks(PAIR).astype(BF16)

    @pl.when(g == 0)
    def _():
        for c in carry:
            carry[c][...] = jnp.broadcast_to(sh[c][...], carry[c].shape)
        st_ref[...] = _state_in(s0_ref[...])
        for ref in (q_sc, y0_sc, m_sc, g_sc, bonus_sc, gate_sc):
            ref[...] = jnp.zeros(ref.shape, ref.dtype)

    def run_chain(preps, st, between=None):
        ys = []
        for i, prep in enumerate(preps):
            y, st = _wkv_apply(prep, st)
            ys.append(y)
            if between is not None:
                between(i)
        return jnp.concatenate(ys, axis=0), st

    def emit(y, bonus, gate, tile):
        rows = pl.ds(pl.multiple_of(tile * tc, tc), tc)
        o_ref[rows, :] = _rwkv_finish(y, bonus, gate, prm, seg_ones).astype(o_ref.dtype)

    old = [(q_sc[i], y0_sc[i], m_sc[i], g_sc[i]) for i in range(n_chunks)]
    old_bonus, old_gate = bonus_sc[...], gate_sc[...]
    st_in = st_ref[...]

    cur = {c: pr[c][...] for c in pr}
    prev = {}
    for c in cur:
        first = _iota(cur[c].shape, 0) == 0
        prev[c] = jnp.where(first, carry[c][0:1, :], pltpu.roll(cur[c], 1, 0))
    for c in cur:
        carry[c][...] = jnp.broadcast_to(cur[c][tc - 1:tc, :], carry[c].shape)

    r, lw, k2, v, kk, a, gate, bonus = _rwkv_tokens(cur, prev, prm, seg_ones)
    cl = tc // n_chunks
    chunks = [tuple(z[i * cl:(i + 1) * cl] for z in (r, lw, k2, v, kk, a)) for i in range(n_chunks)]
    stages, env = _wkv_prepare_stages(chunks)
    per = max(1, len(stages) // (n_chunks + 1))
    todo = list(stages)

    def between(_):
        for stage in todo[:per]:
            stage()
        del todo[:per]

    between(0)
    y_old, st = run_chain(old, st_in, between)
    for stage in todo:
        stage()
    emit(y_old, old_bonus, old_gate, jnp.maximum(g - 1, 0))
    st_ref[...] = jnp.where(g == 0, st_in, st)
    for i, (q_mat, y0, m_mat, g_mat) in enumerate(env["out"]):
        q_sc[i], y0_sc[i], m_sc[i], g_sc[i] = q_mat, y0, m_mat, g_mat
    bonus_sc[...] = bonus
    gate_sc[...] = gate

    @pl.when(g == pl.num_programs(2) - 1)
    def _():
        new = [(q_sc[i], y0_sc[i], m_sc[i], g_sc[i]) for i in range(n_chunks)]
        y_new, st_end = run_chain(new, st_ref[...])
        emit(y_new, bonus_sc[...], gate_sc[...], g)
        s_out_ref[...] = _state_out(st_end)


def _rwkv_sample_body(*refs, n_seq, t_pad, t_valid):
    n_prm = len(_RWKV_PARAM_NAMES)
    pr = dict(zip(("r", "k", "v", "x"), refs[0:4]))
    pv = dict(zip(("r", "k", "v", "x"), refs[4:8]))
    prm = {nm: ref[...] for nm, ref in zip(_RWKV_PARAM_NAMES, refs[8:8 + n_prm])}
    s0_ref = refs[8 + n_prm]
    o_ref, s_out_ref = refs[9 + n_prm], refs[10 + n_prm]
    cur = {c: pr[c][...] for c in pr}
    prev = {c: pv[c][...] for c in pv}
    seg_ones = _head_blocks(PAIR).astype(BF16)
    r, lw, k2, v, kk, a, gate, bonus = _rwkv_tokens(cur, prev, prm, seg_ones)
    valid = (_iota(r.shape, 0) % t_pad) < t_valid
    zero = jnp.zeros_like(r)
    r, lw, k2, v, kk = (jnp.where(valid, z, zero) for z in (r, lw, k2, v, kk))
    chunks = [tuple(z[i * t_pad:(i + 1) * t_pad] for z in (r, lw, k2, v, kk, a)) for i in range(n_seq)]
    states = [_state_in(s0_ref[i]) for i in range(n_seq)]
    applied = [_wkv_apply(prep, st) for prep, st in zip(_wkv_prepare(chunks), states)]
    for i, (_, st) in enumerate(applied):
        s_out_ref[i] = _state_out(st)
    y = jnp.concatenate([y for y, _ in applied], axis=0)
    o_ref[...] = _rwkv_finish(y, bonus, gate, prm, seg_ones).astype(o_ref.dtype)


def _rwkv_param_arrays(p):
    z_w = jnp.zeros((LORA_A, D_RWKV), BF16)
    z_a = jnp.zeros((LORA_W, D_RWKV), BF16)
    return dict(
        mu=p["mu_shift"].reshape(1, N_SHIFT),
        w0=p["w0"].reshape(1, D_RWKV),
        w2=jnp.concatenate([p["w2"].astype(BF16), z_w], axis=0),
        a0=p["a0"].reshape(1, D_RWKV),
        a2=jnp.concatenate([z_a, p["a2"].astype(BF16)], axis=0),
        g2=p["g2"].astype(BF16),
        k_k=p["k_k"].reshape(1, D_RWKV),
        k_a=p["k_a"].reshape(1, D_RWKV),
        r_k=p["r_k"].reshape(1, D_RWKV),
        ln_w=p["ln_x_w"].reshape(1, D_RWKV),
        ln_b=p["ln_x_b"].reshape(1, D_RWKV),
    )


def _rwkv_param_specs(pair_of):
    n_pair_blk = D_RWKV // PAIR
    x_blk = (3 * D_RWKV) // X_COLS

    def vec(off):
        return pl.BlockSpec((1, PAIR), lambda *i: (0, off + pair_of(*i)))

    def mat(rows):
        return pl.BlockSpec((rows, PAIR), lambda *i: (0, pair_of(*i)))

    return [
        vec(0), vec(n_pair_blk), vec(2 * n_pair_blk),
        pl.BlockSpec((1, X_COLS), lambda *i: (0, x_blk)),
        vec(0), mat(LORA_W + LORA_A), vec(0), mat(LORA_W + LORA_A), mat(LORA_G),
        vec(0), vec(0), vec(0), vec(0), vec(0),
    ]


def _rwkv_param_operands(pa):
    return [pa["mu"], pa["mu"], pa["mu"], pa["mu"], pa["w0"], pa["w2"], pa["a0"], pa["a2"], pa["g2"],
            pa["k_k"], pa["k_a"], pa["r_k"], pa["ln_w"], pa["ln_b"]]


def _col_specs(rows, row_of, pair_of):
    n_pair_blk = D_RWKV // PAIR
    x_blk = (3 * D_RWKV) // X_COLS
    specs = [pl.BlockSpec((rows, PAIR), lambda *i, o=o: (row_of(*i), o * n_pair_blk + pair_of(*i)))
             for o in range(3)]
    specs.append(pl.BlockSpec((rows, X_COLS), lambda *i: (row_of(*i), x_blk)))
    return specs


def _rwkv_prompt(pr, shift0, s0, pa, b, s):
    tc = min(512, s)
    nt = s // tc
    n_chunks = max(1, tc // CHUNK)
    cl = tc // n_chunks
    pair_of = lambda bi, pi, gi: pi
    row_of = lambda bi, pi, gi: bi * nt + gi
    n_pair_blk = D_RWKV // PAIR
    x_blk = (3 * D_RWKV) // X_COLS
    shift_specs = [pl.BlockSpec((None, 1, PAIR), lambda bi, pi, gi, o=o: (bi, 0, o * n_pair_blk + pi))
                   for o in range(3)]
    shift_specs.append(pl.BlockSpec((None, 1, X_COLS), lambda bi, pi, gi: (bi, 0, x_blk)))
    state_spec = pl.BlockSpec((None, 2, R_HEAD, R_HEAD), lambda bi, pi, gi: (bi, pi, 0, 0))
    shift3 = shift0.reshape(b, 1, N_SHIFT)
    return pl.pallas_call(
        functools.partial(_rwkv_prompt_body, n_chunks=n_chunks),
        grid=(b, N_PAIRS, nt),
        in_specs=_col_specs(tc, row_of, pair_of) + shift_specs + _rwkv_param_specs(pair_of) + [state_spec],
        out_specs=(pl.BlockSpec((s, PAIR), lambda bi, pi, gi: (bi, pi)), state_spec),
        out_shape=(jax.ShapeDtypeStruct((b * s, D_RWKV), BF16),
                   jax.ShapeDtypeStruct((b, R_HEADS, R_HEAD, R_HEAD), F32)),
        scratch_shapes=[pltpu.VMEM((SUBLANES, PAIR), F32)] * 3 + [
            pltpu.VMEM((SUBLANES, X_COLS), F32),
            pltpu.VMEM((PAIR, PAIR), F32),
            pltpu.VMEM((n_chunks, 2 * cl, PAIR), BF16),
            pltpu.VMEM((n_chunks, 2 * cl, PAIR), F32),
            pltpu.VMEM((n_chunks, PAIR, PAIR), BF16),
            pltpu.VMEM((n_chunks, PAIR, PAIR), F32),
            pltpu.VMEM((tc, PAIR), F32),
            pltpu.VMEM((tc, PAIR), F32),
        ],
        compiler_params=_params(3),
        name="rwkv_prompt",
    )(pr, pr, pr, pr, shift3, shift3, shift3, shift3, *_rwkv_param_operands(pa), s0)


def _rwkv_sample(pr_pad, prev_pad, s0, pa, bd, t_pad, t_valid):
    n_seq = min(8, bd)
    rows = n_seq * t_pad
    pair_of = lambda gi, pi: pi
    row_of = lambda gi, pi: gi
    state_spec = pl.BlockSpec((n_seq, 2, R_HEAD, R_HEAD), lambda gi, pi: (gi, pi, 0, 0))
    return pl.pallas_call(
        functools.partial(_rwkv_sample_body, n_seq=n_seq, t_pad=t_pad, t_valid=t_valid),
        grid=(bd // n_seq, N_PAIRS),
        in_specs=_col_specs(rows, row_of, pair_of) * 2 + _rwkv_param_specs(pair_of) + [state_spec],
        out_specs=(pl.BlockSpec((rows, PAIR), lambda gi, pi: (gi, pi)), state_spec),
        out_shape=(jax.ShapeDtypeStruct((bd * t_pad, D_RWKV), BF16),
                   jax.ShapeDtypeStruct((bd, R_HEADS, R_HEAD, R_HEAD), F32)),
        compiler_params=_params(2),
        name="rwkv_sample",
    )(pr_pad, pr_pad, pr_pad, pr_pad, prev_pad, prev_pad, prev_pad, prev_pad, *_rwkv_param_operands(pa), s0)


def _shifted(u, carry_ref, shift, j):
    n_state = (CONV_W - 1) * shift
    if shift == 1:
        out = pltpu.roll(u, j, 0)
        row = _iota(u.shape, 0)
        for i in range(j):
            out = jnp.where(row == i, carry_ref[n_state - j + i:n_state - j + i + 1, :], out)
        return out
    keep = u.shape[0] - j * shift
    return jnp.concatenate([carry_ref[n_state - j * shift:n_state, :], u[:keep]], axis=0)


def _post_body(x_ref, oa_ref, or_ref, pe_ref, wo_ref, gmp_ref, gfp_ref, wg_ref, wv_ref, cwg_ref, cwv_ref,
               cbg_ref, cbv_ref, wd_ref, gfo_ref, wple_ref, wgate_ref, gple_ref, c0g_ref, c0v_ref,
               y_ref, cng_ref, cnv_ref, cg_ref, cv_ref, *, shift, n_fchunks):
    ti = pl.program_id(1)
    n_state = (CONV_W - 1) * shift
    tm = x_ref.shape[0]

    @pl.when(ti == 0)
    def _():
        cg_ref[0:n_state, :] = c0g_ref[...]
        cv_ref[0:n_state, :] = c0v_ref[...]

    mix = _dot(jnp.concatenate([oa_ref[...], or_ref[...]], axis=1), wo_ref[...])
    x1 = x_ref[...] + _rms(mix, gmp_ref[...])
    h = _rms(x1, gfp_ref[...]).astype(BF16)
    fc = D_FF_PAD // n_fchunks
    acc = jnp.zeros((tm, D_MODEL), F32)
    for c in range(n_fchunks):
        cols = slice(c * fc, (c + 1) * fc)
        halves = []
        for w_ref, cw_ref, cb_ref, carry_ref in ((wg_ref, cwg_ref, cbg_ref, cg_ref),
                                                 (wv_ref, cwv_ref, cbv_ref, cv_ref)):
            u = _dot(h, w_ref[:, cols])
            carry = carry_ref.at[:, cols]
            conv = cb_ref[:, cols] + cw_ref[0:1, cols] * _shifted(u, carry, shift, 2)
            conv = conv + cw_ref[1:2, cols] * _shifted(u, carry, shift, 1)
            conv = conv + cw_ref[2:3, cols] * u
            carry_ref[0:n_state, cols] = u[tm - n_state:, :]
            halves.append(conv)
        act = jax.nn.gelu(halves[0], approximate=True) * halves[1]
        acc = acc + _dot(act.astype(BF16), wd_ref[cols, :])
    x2 = x1 + _rms(acc, gfo_ref[...])
    gate = jax.nn.sigmoid(_dot(x2.astype(BF16), wgate_ref[...]))
    ple = _dot(pe_ref[...].astype(BF16), wple_ref[...]) * gate
    y_ref[...] = x2 + _rms(ple, gple_ref[...])

    @pl.when(ti == pl.num_programs(1) - 1)
    def _():
        cng_ref[...] = cg_ref[0:n_state, :]
        cnv_ref[...] = cv_ref[0:n_state, :]


def _post(x, oa, orw, pe, wts, conv0_g, conv0_v, n_seq, shift):
    n = x.shape[0]
    rows_per_seq = n // n_seq
    tm = min(256, rows_per_seq) if shift == 1 else rows_per_seq
    nt = rows_per_seq // tm
    n_state = (CONV_W - 1) * shift
    n_carry = max(SUBLANES, n_state)
    row = lambda si, ti: (si * nt + ti, 0)
    state_spec = pl.BlockSpec((None, n_state, D_FF_PAD), lambda si, ti: (si, 0, 0))
    w_arrays = [wts[k] for k in ("w_o", "g_mix_post", "g_ffn_pre", "w_gate_up", "w_val_up", "cw_g", "cw_v",
                                 "cb_g", "cb_v", "w_down", "g_ffn_post", "w_ple", "w_ple_gate", "g_ple")]
    return pl.pallas_call(
        functools.partial(_post_body, shift=shift, n_fchunks=2),
        grid=(n_seq, nt),
        in_specs=[
            pl.BlockSpec((tm, D_MODEL), row),
            pl.BlockSpec((tm, D_ATT), row),
            pl.BlockSpec((tm, D_RWKV), row),
            pl.BlockSpec((tm, D_PLE), row),
        ] + [_const_spec(w.shape) for w in w_arrays] + [state_spec, state_spec],
        out_specs=(pl.BlockSpec((tm, D_MODEL), row), state_spec, state_spec),
        out_shape=(jax.ShapeDtypeStruct((n, D_MODEL), F32),
                   jax.ShapeDtypeStruct((n_seq, n_state, D_FF_PAD), F32),
                   jax.ShapeDtypeStruct((n_seq, n_state, D_FF_PAD), F32)),
        scratch_shapes=[pltpu.VMEM((n_carry, D_FF_PAD), F32)] * 2,
        compiler_params=_params(2),
        name="post",
    )(x, oa, orw, pe, *w_arrays, conv0_g, conv0_v)


def _post_weights(p):
    pad_c = D_FF_PAD - D_FF

    def halves(a):
        widths = [(0, 0)] * (a.ndim - 1) + [(0, pad_c)]
        return jnp.pad(a[..., :D_FF], widths), jnp.pad(a[..., D_FF:], widths)

    w_g, w_v = halves(p["w_up"].astype(BF16))
    cw_g, cw_v = halves(p["conv_w"])
    cb_g, cb_v = halves(p["conv_b"].reshape(1, 2 * D_FF))
    return dict(
        w_o=p["w_o"].astype(BF16),
        g_mix_post=p["g_mix_post"].reshape(1, D_MODEL),
        g_ffn_pre=p["g_ffn_pre"].reshape(1, D_MODEL),
        w_gate_up=w_g, w_val_up=w_v, cw_g=cw_g, cw_v=cw_v, cb_g=cb_g, cb_v=cb_v,
        w_down=jnp.pad(p["w_down"].astype(BF16), ((0, pad_c), (0, 0))),
        g_ffn_post=p["g_ffn_post"].reshape(1, D_MODEL),
        w_ple=p["w_ple"].astype(BF16),
        w_ple_gate=p["w_ple_gate"].astype(BF16),
        g_ple=p["g_ple"].reshape(1, D_MODEL),
    ), halves


def kernel(x_prompt, x_sample, p_prompt, p_sample, cache_k, cache_v, page_table, state_shift, state_wkv,
           state_conv, g_mix_pre, w_in, lam_q1, lam_k1, lam_q2, lam_k2, g_subln, mu_shift, w0, w2, a0, a2,
           g2, k_k, k_a, r_k, ln_x_w, ln_x_b, w_o, g_mix_post, g_ffn_pre, w_up, conv_w, conv_b, w_down,
           g_ffn_post, w_ple, w_ple_gate, g_ple):
    depth = w_in.shape[0]
    bp, sp, _ = x_prompt.shape
    bd, td, _ = x_sample.shape
    n_pool, page = cache_k.shape[1], cache_k.shape[2]
    xp = x_prompt.reshape(bp * sp, D_MODEL)
    xs = x_sample.reshape(bd * td, D_MODEL)
    t_pad = SUBLANES
    outs = {k: [] for k in ("kp", "vp", "ks", "vs", "shp", "wkp", "cvp", "shs", "wks", "cvs")}
    for l in range(depth):
        lam_init = 0.8 - 0.6 * math.exp(-0.3 * l)
        p = dict(mu_shift=mu_shift[l], w0=w0[l], w2=w2[l], a0=a0[l], a2=a2[l], g2=g2[l], k_k=k_k[l],
                 k_a=k_a[l], r_k=r_k[l], ln_x_w=ln_x_w[l], ln_x_b=ln_x_b[l], w_o=w_o[l],
                 g_mix_post=g_mix_post[l], g_ffn_pre=g_ffn_pre[l], w_up=w_up[l], conv_w=conv_w[l],
                 conv_b=conv_b[l], w_down=w_down[l], g_ffn_post=g_ffn_post[l], w_ple=w_ple[l],
                 w_ple_gate=w_ple_gate[l], g_ple=g_ple[l])
        w_in_bf = w_in[l].astype(BF16)
        g_pre = g_mix_pre[l].reshape(1, D_MODEL)
        lams = [z[l].reshape(1, A_DH) for z in (lam_q1, lam_k1, lam_q2, lam_k2)]
        gsub = g_subln[l].reshape(1, A_DV)
        pa = _rwkv_param_arrays(p)
        wts, halves = _post_weights(p)

        q, k, v, kb, vb, pr = _inproj(xp, g_pre, w_in_bf)
        oa = _attn_prompt(lams, gsub, q, kb, vb, bp, sp, lam_init)
        orw, wkv_p = _rwkv_prompt(pr, jnp.zeros((bp, N_SHIFT), F32),
                                  jnp.zeros((bp, R_HEADS, R_HEAD, R_HEAD), F32), pa, bp, sp)
        zc = jnp.zeros((bp, CONV_W - 1, D_FF_PAD), F32)
        xp, cng, cnv = _post(xp, oa, orw, p_prompt[l].reshape(bp * sp, D_PLE), wts, zc, zc, bp, 1)
        outs["kp"].append(k.reshape(bp, sp, A_HEADS, A_DK))
        outs["vp"].append(v.reshape(bp, sp, A_HEADS, A_DV))
        outs["shp"].append(pr.reshape(bp, sp, N_SHIFT)[:, sp - 1])
        outs["wkp"].append(wkv_p)
        outs["cvp"].append(jnp.concatenate([cng[..., :D_FF], cnv[..., :D_FF]], axis=-1))

        q, k, v, kb, vb, pr = _inproj(xs, g_pre, w_in_bf)
        q_ht = jnp.swapaxes(q.reshape(bd, td, A_HEADS, A_DK), 1, 2).reshape(bd * A_HEADS * td, A_DK)
        oa = _attn_sample(lams, gsub, q_ht, k.reshape(bd * td * A_HEADS, A_DK), v.reshape(bd * td * A_HEADS, A_DV),
                          cache_k.reshape(depth, n_pool, page * A_HEADS, A_DK),
                          cache_v.reshape(depth, n_pool, page * A_HEADS, A_DV), l, page_table, td, lam_init)
        oa = jnp.swapaxes(oa.reshape(bd, A_HEADS, td, A_DV), 1, 2).reshape(bd, td, D_ATT)
        pr3 = pr.reshape(bd, td, N_SHIFT)
        prev3 = jnp.concatenate([state_shift[l][:, None, :], pr3[:, :td - 1]], axis=1)
        pad_t = ((0, 0), (0, t_pad - td), (0, 0))
        orw, wkv_s = _rwkv_sample(jnp.pad(pr3, pad_t).reshape(bd * t_pad, N_SHIFT),
                                  jnp.pad(prev3, pad_t).reshape(bd * t_pad, N_SHIFT),
                                  state_wkv[l], pa, bd, t_pad, td)
        orw = orw.reshape(bd, t_pad, D_RWKV)[:, :td]
        tmaj = lambda z: jnp.swapaxes(z, 0, 1).reshape(td * bd, z.shape[-1])
        c0g, c0v = halves(jnp.swapaxes(state_conv[l], 0, 1).reshape(1, (CONV_W - 1) * bd, 2 * D_FF))
        ys, cng, cnv = _post(tmaj(xs.reshape(bd, td, D_MODEL)), tmaj(oa), tmaj(orw), tmaj(p_sample[l]),
                             wts, c0g, c0v, 1, bd)
        xs = jnp.swapaxes(ys.reshape(td, bd, D_MODEL), 0, 1).reshape(bd * td, D_MODEL)
        cvs = jnp.concatenate([cng[..., :D_FF], cnv[..., :D_FF]], axis=-1).reshape(CONV_W - 1, bd, 2 * D_FF)
        outs["ks"].append(k.reshape(bd, td, A_HEADS, A_DK))
        outs["vs"].append(v.reshape(bd, td, A_HEADS, A_DV))
        outs["shs"].append(pr3[:, td - 1])
        outs["wks"].append(wkv_s)
        outs["cvs"].append(jnp.swapaxes(cvs, 0, 1))
    st = lambda key: jnp.stack(outs[key])
    return (xp.reshape(bp, sp, D_MODEL), xs.reshape(bd, td, D_MODEL), st("kp"), st("vp"), st("ks"), st("vs"),
            st("shp"), st("wkp"), st("cvp"), st("shs"), st("wks"), st("cvs"))
```

```python
import functools
import math

import jax
import jax.numpy as jnp
from jax import lax
from jax.experimental import pallas as pl
from jax.experimental.pallas import tpu as pltpu

F32 = jnp.float32
BF16 = jnp.bfloat16

D_MODEL = 1024
A_HEADS = 4
A_DH = 64
A_DK = 2 * A_DH
A_DV = 2 * A_DH
D_ATT = A_HEADS * A_DK
R_HEAD = 64
R_HEADS = 8
D_RWKV = R_HEADS * R_HEAD
LORA_W = 64
LORA_A = 64
LORA_G = 128
N_SHIFT = 3 * D_RWKV + LORA_W + LORA_A + LORA_G
N_IN = 3 * D_ATT + N_SHIFT
D_FF = 2752
CONV_W = 3
D_PLE = 256
NORM_EPS = 1e-6
GN_EPS = 64e-5
ATT_SCALE = A_DH ** -0.5
LOG2E = math.log2(math.e)

LANES = 128
SUBLANES = 8
VMEM_LIMIT_BYTES = 56 * 1024 * 1024

D_FF_PAD = ((D_FF + LANES - 1) // LANES) * LANES
PAIR = 2 * R_HEAD
RWKV_COLS = 2 * PAIR
N_COL_BLOCKS = D_RWKV // RWKV_COLS
X_COLS = LORA_W + LORA_A + LORA_G
CHUNK = 64
ATTN_HEADS_PER_LOOP = 4


def _params(n_axes):
    return pltpu.CompilerParams(
        dimension_semantics=("arbitrary",) * n_axes,
        vmem_limit_bytes=VMEM_LIMIT_BYTES,
    )


def _const_spec(shape):
    zeros = (0,) * len(shape)
    return pl.BlockSpec(shape, lambda *_: zeros, pipeline_mode=pl.Buffered(1))


def _rms(x, g):
    return x * lax.rsqrt(jnp.mean(x * x, axis=-1, keepdims=True) + NORM_EPS) * g


def _dot(a, b):
    return jnp.dot(a, b, preferred_element_type=F32)


def _dot_nt(a, b):
    return lax.dot_general(a, b, (((1,), (1,)), ((), ())), preferred_element_type=F32)


def _split3(x):
    hi = x.astype(BF16)
    r1 = x - hi.astype(F32)
    mid = r1.astype(BF16)
    lo = (r1 - mid.astype(F32)).astype(BF16)
    return hi, mid, lo


def _dot_sel(x, sel, passes=3):
    out = None
    for part in _split3(x)[:passes]:
        t = _dot(part, sel)
        out = t if out is None else out + t
    return out


def _sel_dot(sel, x, passes=3):
    out = None
    for part in _split3(x)[:passes]:
        t = _dot(sel, part)
        out = t if out is None else out + t
    return out


def _sel_dot_nt(sel, x, passes=3):
    out = None
    for part in _split3(x)[:passes]:
        t = _dot_nt(sel, part)
        out = t if out is None else out + t
    return out


def _iota(shape, dim):
    return lax.broadcasted_iota(jnp.int32, shape, dim)


def _inproj_body(x_ref, g_ref, w_ref, q_ref, k_ref, v_ref, kb_ref, vb_ref, pr_ref):
    h = _rms(x_ref[...], g_ref[...]).astype(BF16)
    q = _dot(h, w_ref[:, 0:D_ATT])
    q_ref[...] = (q * (ATT_SCALE * LOG2E)).astype(BF16)
    k = _dot(h, w_ref[:, D_ATT:2 * D_ATT])
    kb_ref[...] = k.astype(BF16)
    v = _dot(h, w_ref[:, 2 * D_ATT:3 * D_ATT])
    vb_ref[...] = v.astype(BF16)
    for hd in range(A_HEADS):
        k_ref[:, hd, :] = k[:, hd * A_DK:(hd + 1) * A_DK]
        v_ref[:, hd, :] = v[:, hd * A_DV:(hd + 1) * A_DV]
    pr_ref[...] = _dot(h, w_ref[:, 3 * D_ATT:N_IN])


def _inproj(x, g, w_in_bf):
    n = x.shape[0]
    tm = min(512, n)
    row = lambda i: (i, 0)
    outs = (
        jax.ShapeDtypeStruct((n, D_ATT), BF16),
        jax.ShapeDtypeStruct((n, A_HEADS, A_DK), F32),
        jax.ShapeDtypeStruct((n, A_HEADS, A_DV), F32),
        jax.ShapeDtypeStruct((n, D_ATT), BF16),
        jax.ShapeDtypeStruct((n, D_ATT), BF16),
        jax.ShapeDtypeStruct((n, N_SHIFT), F32),
    )
    return pl.pallas_call(
        _inproj_body,
        grid=(n // tm,),
        in_specs=[
            pl.BlockSpec((tm, D_MODEL), row),
            _const_spec((1, D_MODEL)),
            _const_spec((D_MODEL, N_IN)),
        ],
        out_specs=tuple(pl.BlockSpec((tm,) + s.shape[1:], lambda i, nd=len(s.shape): (i,) + (0,) * (nd - 1))
                        for s in outs),
        out_shape=outs,
        compiler_params=_params(1),
        name="inproj",
    )(x, g, w_in_bf)


def _diff_lambda(lq1, lk1, lq2, lk2, lam_init):
    s1 = jnp.sum(lq1 * lk1, axis=-1, keepdims=True)
    s2 = jnp.sum(lq2 * lk2, axis=-1, keepdims=True)
    return jnp.exp(s1) - jnp.exp(s2) + lam_init


def _stack_maps(q):
    lane = _iota(q.shape, 1)
    zero = jnp.zeros_like(q)
    return jnp.concatenate([jnp.where(lane < A_DH, q, zero), jnp.where(lane >= A_DH, q, zero)], axis=0)


def _subln(o, gsub, lam_init):
    return _rms(o, gsub) * (1.0 - lam_init)


def _attn_prompt_body(lq1, lk1, lq2, lk2, gsub_ref, q_ref, k_ref, v_ref, o_ref, *, tq, lam_init):
    qi = pl.program_id(1)
    lam = _diff_lambda(lq1[...], lk1[...], lq2[...], lk2[...], lam_init)
    heads = [slice(h * A_DK, (h + 1) * A_DK) for h in range(A_HEADS)]
    qs = [_stack_maps(q_ref[:, cols]) for cols in heads]
    causal = _iota((2 * tq, tq), 1) <= _iota((2 * tq, tq), 0) % tq

    def tile(j, carry, group, diagonal):
        off = pl.multiple_of(j * tq, tq)
        out = []
        for h, (m, l, acc) in zip(group, carry):
            s = _dot_nt(qs[h], k_ref[pl.ds(off, tq), heads[h]])
            if diagonal:
                s = jnp.where(causal, s, -jnp.inf)
            m_new = jnp.maximum(m, jnp.max(s, axis=-1, keepdims=True))
            p = jnp.exp2(s - m_new)
            alpha = jnp.exp2(m - m_new)
            l = alpha * l + jnp.sum(p, axis=-1, keepdims=True)
            acc = alpha * acc + _dot(p.astype(BF16), v_ref[pl.ds(off, tq), heads[h]])
            out.append((m_new, l, acc))
        return tuple(out)

    for first in range(0, A_HEADS, ATTN_HEADS_PER_LOOP):
        group = tuple(range(first, first + ATTN_HEADS_PER_LOOP))
        init = tuple((jnp.full((2 * tq, 1), -jnp.inf, F32), jnp.zeros((2 * tq, 1), F32),
                      jnp.zeros((2 * tq, A_DV), F32)) for _ in group)
        carry = lax.fori_loop(0, qi, lambda j, c, group=group: tile(j, c, group, False), init)
        carry = tile(qi, carry, group, True)
        for h, (_, l, acc) in zip(group, carry):
            o = acc / l
            o = o[:tq] - lam * o[tq:]
            o_ref[:, heads[h]] = _subln(o, gsub_ref[...], lam_init).astype(o_ref.dtype)


def _attn_prompt(lams, gsub, q, kb, vb, b, s, lam_init):
    tq = min(256, s)
    nq = s // tq
    lam_spec = _const_spec((1, A_DH))
    return pl.pallas_call(
        functools.partial(_attn_prompt_body, tq=tq, lam_init=lam_init),
        grid=(b, nq),
        in_specs=[
            lam_spec, lam_spec, lam_spec, lam_spec,
            _const_spec((1, A_DV)),
            pl.BlockSpec((tq, D_ATT), lambda bi, qi: (bi * nq + qi, 0)),
            pl.BlockSpec((s, D_ATT), lambda bi, qi: (bi, 0)),
            pl.BlockSpec((s, D_ATT), lambda bi, qi: (bi, 0)),
        ],
        out_specs=pl.BlockSpec((tq, D_ATT), lambda bi, qi: (bi * nq + qi, 0)),
        out_shape=jax.ShapeDtypeStruct((b * s, D_ATT), BF16),
        compiler_params=_params(2),
        name="attn_prompt",
    )(*lams, gsub, q, kb, vb)


def _attn_sample_body(pt_ref, lq1, lk1, lq2, lk2, gsub_ref, q_ref, kn_ref, vn_ref, *rest,
                      n_pages, t, lam_init):
    del pt_ref
    kp_refs = rest[:n_pages]
    vp_refs = rest[n_pages:2 * n_pages]
    o_ref = rest[2 * n_pages]
    lam = _diff_lambda(lq1[...], lk1[...], lq2[...], lk2[...], lam_init)
    nq = A_HEADS * t
    qs = _stack_maps(q_ref[...])
    q_head = (_iota((2 * nq, 1), 0) % nq) // t
    q_tok = _iota((2 * nq, 1), 0) % t
    width = kp_refs[0].shape[0]
    past_ok = (_iota((2 * nq, width), 1) % A_HEADS) == q_head
    new_lane = _iota((2 * nq, nq), 1)
    new_ok = ((new_lane % A_HEADS) == q_head) & ((new_lane // A_HEADS) <= q_tok)
    s_past = [jnp.where(past_ok, _dot_nt(qs, kp[...].astype(BF16)), -jnp.inf) for kp in kp_refs]
    s_new = jnp.where(new_ok, _dot_nt(qs, kn_ref[...].astype(BF16)), -jnp.inf)
    m = jnp.max(s_new, axis=-1, keepdims=True)
    for sp in s_past:
        m = jnp.maximum(m, jnp.max(sp, axis=-1, keepdims=True))
    p_new = jnp.exp2(s_new - m)
    l = jnp.sum(p_new, axis=-1, keepdims=True)
    acc = _dot(p_new.astype(BF16), vn_ref[...].astype(BF16))
    for sp, vp in zip(s_past, vp_refs):
        p = jnp.exp2(sp - m)
        l = l + jnp.sum(p, axis=-1, keepdims=True)
        acc = acc + _dot(p.astype(BF16), vp[...].astype(BF16))
    o = acc / l
    o = o[:nq] - lam * o[nq:]
    o_ref[...] = _subln(o, gsub_ref[...], lam_init).astype(o_ref.dtype)


def _attn_sample(lams, gsub, q, k_new, v_new, cache_k, cache_v, layer, page_table, t, lam_init):
    bd, n_pages = page_table.shape
    nq = A_HEADS * t
    rows = cache_k.shape[2]
    lam_spec = pl.BlockSpec((1, A_DH), lambda b, pt: (0, 0))
    tok_spec = pl.BlockSpec((nq, A_DK), lambda b, pt: (b, 0))

    def page_spec(p):
        return pl.BlockSpec((None, None, rows, A_DK), lambda b, pt, p=p: (layer, pt[b, p], 0, 0))

    grid_spec = pltpu.PrefetchScalarGridSpec(
        num_scalar_prefetch=1,
        grid=(bd,),
        in_specs=[lam_spec] * 4 + [pl.BlockSpec((1, A_DV), lambda b, pt: (0, 0)), tok_spec, tok_spec, tok_spec]
        + [page_spec(p) for p in range(n_pages)] * 2,
        out_specs=tok_spec,
    )
    return pl.pallas_call(
        functools.partial(_attn_sample_body, n_pages=n_pages, t=t, lam_init=lam_init),
        grid_spec=grid_spec,
        out_shape=jax.ShapeDtypeStruct((bd * nq, A_DV), BF16),
        compiler_params=_params(1),
        name="attn_sample",
    )(page_table, *lams, gsub, q, k_new, v_new, *([cache_k] * n_pages), *([cache_v] * n_pages))


def _head_blocks(n):
    return (_iota((n, n), 0) // R_HEAD) == (_iota((n, n), 1) // R_HEAD)


def _softplus(z):
    return jnp.maximum(z, 0.0) + jnp.log1p(jnp.exp(-jnp.abs(z)))


def _rwkv_tokens(cur, prev, prm, seg_ones):
    mix = {c: cur[c] + (prev[c] - cur[c]) * prm["mu_" + c] for c in ("r", "k", "v", "x")}
    r, k, v = mix["r"], mix["k"], mix["v"]
    xwa = mix["x"][:, :LORA_W + LORA_A]
    xg = mix["x"][:, LORA_W + LORA_A:]
    w_in = prm["w0"] + _dot(jnp.tanh(xwa).astype(BF16), prm["w2"])
    lw = -jnp.exp(-_softplus(-w_in) - 0.5)
    a = jax.nn.sigmoid(prm["a0"] + _dot(xwa.astype(BF16), prm["a2"]))
    g = _dot(jax.nn.sigmoid(xg).astype(BF16), prm["g2"])
    kk = k * prm["k_k"]
    nrm = jnp.sqrt(_dot_sel(kk * kk, seg_ones, passes=1))
    kk = kk / jnp.maximum(nrm, 1e-12)
    k2 = k * (1.0 + (a - 1.0) * prm["k_a"])
    bonus = _dot_sel(r * k2 * prm["r_k"], seg_ones, passes=1) * v
    return r, lw, k2, v, kk, a, g, bonus


def _rwkv_finish(y, bonus, g, prm, seg_ones):
    mean = _dot_sel(y, seg_ones, passes=1) * (1.0 / R_HEAD)
    d = y - mean
    var = _dot_sel(d * d, seg_ones, passes=1) * (1.0 / R_HEAD)
    yn = d * lax.rsqrt(var + GN_EPS) * prm["ln_w"] + prm["ln_b"]
    return (yn + bonus) * g


def _stack_heads(x):
    lane = _iota(x.shape, 1)
    zero = jnp.zeros_like(x)
    return jnp.concatenate([jnp.where(lane < R_HEAD, x, zero), jnp.where(lane >= R_HEAD, x, zero)], axis=0)


def _wkv_prepare_stages(chunks):
    c = chunks[0][0].shape[0]
    n = 2 * c
    steps = max(1, int(math.ceil(math.log2(c))))
    tril = (_iota((c, c), 1) <= _iota((c, c), 0)).astype(BF16)
    row = _iota((n, n), 0)
    col = _iota((n, n), 1)
    same = (row // c) == (col // c)
    strict = same & (col < row)
    incl = same & (col <= row)
    zero = jnp.zeros((n, n), F32)
    eye_n = jnp.where(row == col, 1.0, 0.0)
    eye = (_iota((PAIR, PAIR), 0) == _iota((PAIR, PAIR), 1))
    eye_b = eye.astype(BF16)
    env = {}

    def cumsum():
        env["big_l"] = [_sel_dot(tril, ch[1], passes=2) for ch in chunks]

    def operands():
        ops = []
        for (r, lw, k2, v, kk, a), big_l in zip(chunks, env["big_l"]):
            l_end = big_l[c - 1:c, :]
            e_in = jnp.exp(big_l)
            e_ex = jnp.exp(big_l - lw)
            e_neg = jnp.exp(-big_l)
            e_rem = jnp.exp(l_end - big_l)
            b = kk * a
            a_t = _stack_heads(-kk * e_ex)
            r_t = _stack_heads(r * e_in)
            ops.append(dict(
                l_end=l_end, a_t=a_t, r_t=r_t, a_tb=a_t.astype(BF16), r_tb=r_t.astype(BF16),
                b_t=_stack_heads(b * e_neg).astype(BF16), k_t=_stack_heads(k2 * e_neg).astype(BF16),
                b_h=_stack_heads(b * e_rem).astype(BF16), k_h=_stack_heads(k2 * e_rem).astype(BF16),
                v_s=_stack_heads(v).astype(BF16)))
        env["ops"] = ops

    def pair(name, lhs, rhs, mask):
        def run():
            env[name] = [jnp.where(mask, _dot_nt(o[lhs], o[rhs]), zero) for o in env["ops"]]
        return run

    def transposes():
        env["b_ht"] = [_dot_nt(eye_b, o["b_h"]).astype(BF16) for o in env["ops"]]
        env["k_ht"] = [_dot_nt(eye_b, o["k_h"]).astype(BF16) for o in env["ops"]]

    def rhs():
        env["x"] = [jnp.concatenate([o["a_t"], _dot(mk.astype(BF16), o["v_s"])], axis=1).astype(BF16)
                    for o, mk in zip(env["ops"], env["m_ak"])]
        env["pb"] = [p.astype(BF16) for p in env["m_ab"]]
        env["t"] = [eye_n + p for p in env["m_ab"]]

    def square():
        env["pb"] = [_dot(pb, pb).astype(BF16) for pb in env["pb"]]

    def extend():
        env["t"] = [t + _dot(t.astype(BF16), pb) for t, pb in zip(env["t"], env["pb"])]

    def solve():
        env["w"] = [_dot(t.astype(BF16), x).astype(BF16) for t, x in zip(env["t"], env["x"])]

    def state_side():
        env["mg"] = [_dot(bt, w) for bt, w in zip(env["b_ht"], env["w"])]
        env["kv"] = [_dot(kt, o["v_s"]) for kt, o in zip(env["k_ht"], env["ops"])]

    def output_side():
        env["qy"] = [_dot(mr.astype(BF16), w) for mr, w in zip(env["m_rb"], env["w"])]
        env["rkv"] = [_dot(mr.astype(BF16), o["v_s"]) for mr, o in zip(env["m_rk"], env["ops"])]

    def finish():
        out = []
        for o, mg, kv, qy, rk in zip(env["ops"], env["mg"], env["kv"], env["qy"], env["rkv"]):
            m_mat = jnp.where(eye, jnp.exp(o["l_end"]), 0.0) + mg[:, :PAIR]
            g_mat = mg[:, PAIR:] + kv
            q_mat = o["r_t"] + qy[:, :PAIR]
            y0 = qy[:, PAIR:] + rk
            out.append((q_mat.astype(BF16), y0, m_mat.astype(BF16), g_mat))
        env["out"] = out

    stages = [cumsum, operands, pair("m_ab", "a_tb", "b_t", strict), pair("m_ak", "a_tb", "k_t", strict),
              pair("m_rb", "r_tb", "b_t", incl), pair("m_rk", "r_tb", "k_t", incl), transposes, rhs]
    for _ in range(steps - 1):
        stages += [square, extend]
    stages += [solve, state_side, output_side, finish]
    return stages, env


def _wkv_prepare(chunks):
    stages, env = _wkv_prepare_stages(chunks)
    for stage in stages:
        stage()
    return env["out"]


def _wkv_apply(prep, st):
    q_mat, y0, m_mat, g_mat = prep
    c = y0.shape[0] // 2
    stb = st.astype(BF16)
    y_st = _dot(q_mat, stb) + y0
    return y_st[:c] + y_st[c:], _dot(m_mat, stb) + g_mat


def _state_in(s0):
    x = s0.reshape(PAIR, R_HEAD)
    sel = (_iota((PAIR, R_HEAD), 0) % R_HEAD == _iota((PAIR, R_HEAD), 1)).astype(BF16)
    full = _sel_dot_nt(sel, x)
    return jnp.where(_head_blocks(PAIR), full, 0.0)


def _state_out(st):
    folded = st[:R_HEAD] + st[R_HEAD:]
    eye = (_iota((PAIR, PAIR), 0) == _iota((PAIR, PAIR), 1)).astype(BF16)
    return _sel_dot_nt(eye, folded).reshape(2, R_HEAD, R_HEAD)


_RWKV_PARAM_NAMES = ("mu_r", "mu_k", "mu_v", "mu_x", "w0", "w2", "a0", "a2", "g2", "k_k", "k_a", "r_k",
                     "ln_w", "ln_b")


def _split_pairs(z, rows):
    return [z[rows, p * PAIR:(p + 1) * PAIR] for p in range(z.shape[1] // PAIR)]


def _run_chains(preps, states, between=None):
    states = list(states)
    ys = [[] for _ in preps]
    for i in range(len(preps[0])):
        for p, chain in enumerate(preps):
            y, states[p] = _wkv_apply(chain[i], states[p])
            ys[p].append(y)
        if between is not None:
            between()
    return jnp.concatenate([jnp.concatenate(col, axis=0) for col in ys], axis=1), states


def _rwkv_prompt_body(*refs, n_chunks):
    n_prm = len(_RWKV_PARAM_NAMES)
    pr = dict(zip(("r", "k", "v", "x"), refs[0:4]))
    sh = dict(zip(("r", "k", "v", "x"), refs[4:8]))
    prm = {nm: ref[...] for nm, ref in zip(_RWKV_PARAM_NAMES, refs[8:8 + n_prm])}
    s0_ref = refs[8 + n_prm]
    o_ref, s_out_ref = refs[9 + n_prm], refs[10 + n_prm]
    scratch = refs[11 + n_prm:]
    carry = dict(zip(("r", "k", "v", "x"), scratch[0:4]))
    st_ref, q_sc, y0_sc, m_sc, g_sc, bonus_sc, gate_sc = scratch[4:11]
    g = pl.program_id(2)
    tc, width = bonus_sc.shape
    n_pairs = width // PAIR
    cl = tc // n_chunks
    seg_ones = _head_blocks(width).astype(BF16)

    @pl.when(g == 0)
    def _():
        for c in carry:
            carry[c][...] = jnp.broadcast_to(sh[c][...], carry[c].shape)
        for p in range(n_pairs):
            st_ref[p] = _state_in(s0_ref[2 * p:2 * p + 2])
        for ref in (q_sc, y0_sc, m_sc, g_sc, bonus_sc, gate_sc):
            ref[...] = jnp.zeros(ref.shape, ref.dtype)

    def stored():
        return [[(q_sc[p * n_chunks + i], y0_sc[p * n_chunks + i], m_sc[p * n_chunks + i],
                  g_sc[p * n_chunks + i]) for i in range(n_chunks)] for p in range(n_pairs)]

    def emit(y, bonus, gate, tile):
        rows = pl.ds(pl.multiple_of(tile * tc, tc), tc)
        o_ref[rows, :] = _rwkv_finish(y, bonus, gate, prm, seg_ones).astype(o_ref.dtype)

    old = stored()
    old_bonus, old_gate = bonus_sc[...], gate_sc[...]
    st_in = [st_ref[p] for p in range(n_pairs)]

    cur = {c: pr[c][...] for c in pr}
    prev = {}
    for c in cur:
        first = _iota(cur[c].shape, 0) == 0
        prev[c] = jnp.where(first, carry[c][0:1, :], pltpu.roll(cur[c], 1, 0))
    for c in cur:
        carry[c][...] = jnp.broadcast_to(cur[c][tc - 1:tc, :], carry[c].shape)
    r, lw, k2, v, kk, a, gate, bonus = _rwkv_tokens(cur, prev, prm, seg_ones)
    per_chunk = [list(zip(*(_split_pairs(z, slice(i * cl, (i + 1) * cl)) for z in (r, lw, k2, v, kk, a))))
                 for i in range(n_chunks)]
    chunks = [per_chunk[i][p] for p in range(n_pairs) for i in range(n_chunks)]
    stages, env = _wkv_prepare_stages(chunks)
    per = max(1, len(stages) // (n_chunks + 1))
    todo = list(stages)

    def between():
        for stage in todo[:per]:
            stage()
        del todo[:per]

    between()
    y_old, st = _run_chains(old, st_in, between)
    for stage in todo:
        stage()
    emit(y_old, old_bonus, old_gate, jnp.maximum(g - 1, 0))
    for p in range(n_pairs):
        st_ref[p] = jnp.where(g == 0, st_in[p], st[p])
    for i, (q_mat, y0, m_mat, g_mat) in enumerate(env["out"]):
        q_sc[i], y0_sc[i], m_sc[i], g_sc[i] = q_mat, y0, m_mat, g_mat
    bonus_sc[...] = bonus
    gate_sc[...] = gate

    @pl.when(g == pl.num_programs(2) - 1)
    def _():
        y_new, st_end = _run_chains(stored(), [st_ref[p] for p in range(n_pairs)])
        emit(y_new, bonus_sc[...], gate_sc[...], g)
        for p in range(n_pairs):
            s_out_ref[2 * p:2 * p + 2] = _state_out(st_end[p])


def _rwkv_sample_body(*refs, n_seq, t_pad, t_valid):
    n_prm = len(_RWKV_PARAM_NAMES)
    pr = dict(zip(("r", "k", "v", "x"), refs[0:4]))
    pv = dict(zip(("r", "k", "v", "x"), refs[4:8]))
    prm = {nm: ref[...] for nm, ref in zip(_RWKV_PARAM_NAMES, refs[8:8 + n_prm])}
    s0_ref = refs[8 + n_prm]
    o_ref, s_out_ref = refs[9 + n_prm], refs[10 + n_prm]
    cur = {c: pr[c][...] for c in pr}
    prev = {c: pv[c][...] for c in pv}
    width = o_ref.shape[1]
    n_pairs = width // PAIR
    seg_ones = _head_blocks(width).astype(BF16)
    r, lw, k2, v, kk, a, gate, bonus = _rwkv_tokens(cur, prev, prm, seg_ones)
    valid = (_iota(r.shape, 0) % t_pad) < t_valid
    zero = jnp.zeros_like(r)
    r, lw, k2, v, kk = (jnp.where(valid, z, zero) for z in (r, lw, k2, v, kk))
    per_seq = [list(zip(*(_split_pairs(z, slice(i * t_pad, (i + 1) * t_pad)) for z in (r, lw, k2, v, kk, a))))
               for i in range(n_seq)]
    chunks = [per_seq[i][p] for i in range(n_seq) for p in range(n_pairs)]
    states = [_state_in(s0_ref[i, 2 * p:2 * p + 2]) for i in range(n_seq) for p in range(n_pairs)]
    applied = [_wkv_apply(prep, st) for prep, st in zip(_wkv_prepare(chunks), states)]
    rows = []
    for i in range(n_seq):
        for p in range(n_pairs):
            s_out_ref[i, 2 * p:2 * p + 2] = _state_out(applied[i * n_pairs + p][1])
        rows.append(jnp.concatenate([applied[i * n_pairs + p][0] for p in range(n_pairs)], axis=1))
    y = jnp.concatenate(rows, axis=0)
    o_ref[...] = _rwkv_finish(y, bonus, gate, prm, seg_ones).astype(o_ref.dtype)


def _rwkv_param_arrays(p):
    z_w = jnp.zeros((LORA_A, D_RWKV), BF16)
    z_a = jnp.zeros((LORA_W, D_RWKV), BF16)
    return dict(
        mu=p["mu_shift"].reshape(1, N_SHIFT),
        w0=p["w0"].reshape(1, D_RWKV),
        w2=jnp.concatenate([p["w2"].astype(BF16), z_w], axis=0),
        a0=p["a0"].reshape(1, D_RWKV),
        a2=jnp.concatenate([z_a, p["a2"].astype(BF16)], axis=0),
        g2=p["g2"].astype(BF16),
        k_k=p["k_k"].reshape(1, D_RWKV),
        k_a=p["k_a"].reshape(1, D_RWKV),
        r_k=p["r_k"].reshape(1, D_RWKV),
        ln_w=p["ln_x_w"].reshape(1, D_RWKV),
        ln_b=p["ln_x_b"].reshape(1, D_RWKV),
    )


def _rwkv_param_specs(block_of):
    x_blk = (3 * D_RWKV) // X_COLS

    def vec(off):
        return pl.BlockSpec((1, RWKV_COLS), lambda *i: (0, off + block_of(*i)))

    def mat(rows):
        return pl.BlockSpec((rows, RWKV_COLS), lambda *i: (0, block_of(*i)))

    return [
        vec(0), vec(N_COL_BLOCKS), vec(2 * N_COL_BLOCKS),
        pl.BlockSpec((1, X_COLS), lambda *i: (0, x_blk)),
        vec(0), mat(LORA_W + LORA_A), vec(0), mat(LORA_W + LORA_A), mat(LORA_G),
        vec(0), vec(0), vec(0), vec(0), vec(0),
    ]


def _rwkv_param_operands(pa):
    return [pa["mu"], pa["mu"], pa["mu"], pa["mu"], pa["w0"], pa["w2"], pa["a0"], pa["a2"], pa["g2"],
            pa["k_k"], pa["k_a"], pa["r_k"], pa["ln_w"], pa["ln_b"]]


def _col_specs(rows, row_of, block_of):
    x_blk = (3 * D_RWKV) // X_COLS
    specs = [pl.BlockSpec((rows, RWKV_COLS), lambda *i, o=o: (row_of(*i), o * N_COL_BLOCKS + block_of(*i)))
             for o in range(3)]
    specs.append(pl.BlockSpec((rows, X_COLS), lambda *i: (row_of(*i), x_blk)))
    return specs


def _rwkv_prompt(pr, shift0, s0, pa, b, s):
    tc = min(512, s)
    nt = s // tc
    n_chunks = max(1, tc // CHUNK)
    cl = tc // n_chunks
    pairs = RWKV_COLS // PAIR
    block_of = lambda bi, ci, gi: ci
    row_of = lambda bi, ci, gi: bi * nt + gi
    x_blk = (3 * D_RWKV) // X_COLS
    shift_specs = [pl.BlockSpec((None, 1, RWKV_COLS), lambda bi, ci, gi, o=o: (bi, 0, o * N_COL_BLOCKS + ci))
                   for o in range(3)]
    shift_specs.append(pl.BlockSpec((None, 1, X_COLS), lambda bi, ci, gi: (bi, 0, x_blk)))
    state_spec = pl.BlockSpec((None, 2 * pairs, R_HEAD, R_HEAD), lambda bi, ci, gi: (bi, ci, 0, 0))
    shift3 = shift0.reshape(b, 1, N_SHIFT)
    return pl.pallas_call(
        functools.partial(_rwkv_prompt_body, n_chunks=n_chunks),
        grid=(b, N_COL_BLOCKS, nt),
        in_specs=_col_specs(tc, row_of, block_of) + shift_specs + _rwkv_param_specs(block_of) + [state_spec],
        out_specs=(pl.BlockSpec((s, RWKV_COLS), lambda bi, ci, gi: (bi, ci)), state_spec),
        out_shape=(jax.ShapeDtypeStruct((b * s, D_RWKV), BF16),
                   jax.ShapeDtypeStruct((b, R_HEADS, R_HEAD, R_HEAD), F32)),
        scratch_shapes=[pltpu.VMEM((SUBLANES, RWKV_COLS), F32)] * 3 + [
            pltpu.VMEM((SUBLANES, X_COLS), F32),
            pltpu.VMEM((pairs, PAIR, PAIR), F32),
            pltpu.VMEM((pairs * n_chunks, 2 * cl, PAIR), BF16),
            pltpu.VMEM((pairs * n_chunks, 2 * cl, PAIR), F32),
            pltpu.VMEM((pairs * n_chunks, PAIR, PAIR), BF16),
            pltpu.VMEM((pairs * n_chunks, PAIR, PAIR), F32),
            pltpu.VMEM((tc, RWKV_COLS), F32),
            pltpu.VMEM((tc, RWKV_COLS), F32),
        ],
        compiler_params=_params(3),
        name="rwkv_prompt",
    )(pr, pr, pr, pr, shift3, shift3, shift3, shift3, *_rwkv_param_operands(pa), s0)


def _rwkv_sample(pr_pad, prev_pad, s0, pa, bd, t_pad, t_valid):
    n_seq = min(8, bd)
    rows = n_seq * t_pad
    pairs = RWKV_COLS // PAIR
    block_of = lambda gi, ci: ci
    row_of = lambda gi, ci: gi
    state_spec = pl.BlockSpec((n_seq, 2 * pairs, R_HEAD, R_HEAD), lambda gi, ci: (gi, ci, 0, 0))
    return pl.pallas_call(
        functools.partial(_rwkv_sample_body, n_seq=n_seq, t_pad=t_pad, t_valid=t_valid),
        grid=(bd // n_seq, N_COL_BLOCKS),
        in_specs=_col_specs(rows, row_of, block_of) * 2 + _rwkv_param_specs(block_of) + [state_spec],
        out_specs=(pl.BlockSpec((rows, RWKV_COLS), lambda gi, ci: (gi, ci)), state_spec),
        out_shape=(jax.ShapeDtypeStruct((bd * t_pad, D_RWKV), BF16),
                   jax.ShapeDtypeStruct((bd, R_HEADS, R_HEAD, R_HEAD), F32)),
        compiler_params=_params(2),
        name="rwkv_sample",
    )(pr_pad, pr_pad, pr_pad, pr_pad, prev_pad, prev_pad, prev_pad, prev_pad, *_rwkv_param_operands(pa), s0)


def _shifted(u, carry_ref, shift, j):
    n_state = (CONV_W - 1) * shift
    if shift == 1:
        out = pltpu.roll(u, j, 0)
        row = _iota(u.shape, 0)
        for i in range(j):
            out = jnp.where(row == i, carry_ref[n_state - j + i:n_state - j + i + 1, :], out)
        return out
    keep = u.shape[0] - j * shift
    return jnp.concatenate([carry_ref[n_state - j * shift:n_state, :], u[:keep]], axis=0)


def _post_body(x_ref, oa_ref, or_ref, pe_ref, wo_ref, gmp_ref, gfp_ref, wg_ref, wv_ref, cwg_ref, cwv_ref,
               cbg_ref, cbv_ref, wd_ref, gfo_ref, wple_ref, wgate_ref, gple_ref, c0g_ref, c0v_ref,
               y_ref, cng_ref, cnv_ref, cg_ref, cv_ref, *, shift, n_fchunks):
    ti = pl.program_id(1)
    n_state = (CONV_W - 1) * shift
    tm = x_ref.shape[0]

    @pl.when(ti == 0)
    def _():
        cg_ref[0:n_state, :] = c0g_ref[...]
        cv_ref[0:n_state, :] = c0v_ref[...]

    mix = _dot(jnp.concatenate([oa_ref[...], or_ref[...]], axis=1), wo_ref[...])
    x1 = x_ref[...] + _rms(mix, gmp_ref[...])
    h = _rms(x1, gfp_ref[...]).astype(BF16)
    fc = D_FF_PAD // n_fchunks
    acc = jnp.zeros((tm, D_MODEL), F32)
    for c in range(n_fchunks):
        cols = slice(c * fc, (c + 1) * fc)
        halves = []
        for w_ref, cw_ref, cb_ref, carry_ref in ((wg_ref, cwg_ref, cbg_ref, cg_ref),
                                                 (wv_ref, cwv_ref, cbv_ref, cv_ref)):
            u = _dot(h, w_ref[:, cols])
            carry = carry_ref.at[:, cols]
            conv = cb_ref[:, cols] + cw_ref[0:1, cols] * _shifted(u, carry, shift, 2)
            conv = conv + cw_ref[1:2, cols] * _shifted(u, carry, shift, 1)
            conv = conv + cw_ref[2:3, cols] * u
            carry_ref[0:n_state, cols] = u[tm - n_state:, :]
            halves.append(conv)
        act = jax.nn.gelu(halves[0], approximate=True) * halves[1]
        acc = acc + _dot(act.astype(BF16), wd_ref[cols, :])
    x2 = x1 + _rms(acc, gfo_ref[...])
    gate = jax.nn.sigmoid(_dot(x2.astype(BF16), wgate_ref[...]))
    ple = _dot(pe_ref[...].astype(BF16), wple_ref[...]) * gate
    y_ref[...] = x2 + _rms(ple, gple_ref[...])

    @pl.when(ti == pl.num_programs(1) - 1)
    def _():
        cng_ref[...] = cg_ref[0:n_state, :]
        cnv_ref[...] = cv_ref[0:n_state, :]


def _post(x, oa, orw, pe, wts, conv0_g, conv0_v, n_seq, shift):
    n = x.shape[0]
    rows_per_seq = n // n_seq
    tm = min(512, rows_per_seq) if shift == 1 else rows_per_seq
    nt = rows_per_seq // tm
    n_state = (CONV_W - 1) * shift
    n_carry = max(SUBLANES, n_state)
    row = lambda si, ti: (si * nt + ti, 0)
    state_spec = pl.BlockSpec((None, n_state, D_FF_PAD), lambda si, ti: (si, 0, 0))
    w_arrays = [wts[k] for k in ("w_o", "g_mix_post", "g_ffn_pre", "w_gate_up", "w_val_up", "cw_g", "cw_v",
                                 "cb_g", "cb_v", "w_down", "g_ffn_post", "w_ple", "w_ple_gate", "g_ple")]
    return pl.pallas_call(
        functools.partial(_post_body, shift=shift, n_fchunks=2),
        grid=(n_seq, nt),
        in_specs=[
            pl.BlockSpec((tm, D_MODEL), row),
            pl.BlockSpec((tm, D_ATT), row),
            pl.BlockSpec((tm, D_RWKV), row),
            pl.BlockSpec((tm, D_PLE), row),
        ] + [_const_spec(w.shape) for w in w_arrays] + [state_spec, state_spec],
        out_specs=(pl.BlockSpec((tm, D_MODEL), row), state_spec, state_spec),
        out_shape=(jax.ShapeDtypeStruct((n, D_MODEL), F32),
                   jax.ShapeDtypeStruct((n_seq, n_state, D_FF_PAD), F32),
                   jax.ShapeDtypeStruct((n_seq, n_state, D_FF_PAD), F32)),
        scratch_shapes=[pltpu.VMEM((n_carry, D_FF_PAD), F32)] * 2,
        compiler_params=_params(2),
        name="post",
    )(x, oa, orw, pe, *w_arrays, conv0_g, conv0_v)


def _post_weights(p):
    pad_c = D_FF_PAD - D_FF

    def halves(a):
        widths = [(0, 0)] * (a.ndim - 1) + [(0, pad_c)]
        return jnp.pad(a[..., :D_FF], widths), jnp.pad(a[..., D_FF:], widths)

    w_g, w_v = halves(p["w_up"].astype(BF16))
    cw_g, cw_v = halves(p["conv_w"])
    cb_g, cb_v = halves(p["conv_b"].reshape(1, 2 * D_FF))
    return dict(
        w_o=p["w_o"].astype(BF16),
        g_mix_post=p["g_mix_post"].reshape(1, D_MODEL),
        g_ffn_pre=p["g_ffn_pre"].reshape(1, D_MODEL),
        w_gate_up=w_g, w_val_up=w_v, cw_g=cw_g, cw_v=cw_v, cb_g=cb_g, cb_v=cb_v,
        w_down=jnp.pad(p["w_down"].astype(BF16), ((0, pad_c), (0, 0))),
        g_ffn_post=p["g_ffn_post"].reshape(1, D_MODEL),
        w_ple=p["w_ple"].astype(BF16),
        w_ple_gate=p["w_ple_gate"].astype(BF16),
        g_ple=p["g_ple"].reshape(1, D_MODEL),
    ), halves


def kernel(x_prompt, x_sample, p_prompt, p_sample, cache_k, cache_v, page_table, state_shift, state_wkv,
           state_conv, g_mix_pre, w_in, lam_q1, lam_k1, lam_q2, lam_k2, g_subln, mu_shift, w0, w2, a0, a2,
           g2, k_k, k_a, r_k, ln_x_w, ln_x_b, w_o, g_mix_post, g_ffn_pre, w_up, conv_w, conv_b, w_down,
           g_ffn_post, w_ple, w_ple_gate, g_ple):
    depth = w_in.shape[0]
    bp, sp, _ = x_prompt.shape
    bd, td, _ = x_sample.shape
    n_pool, page = cache_k.shape[1], cache_k.shape[2]
    xp = x_prompt.reshape(bp * sp, D_MODEL)
    xs = x_sample.reshape(bd * td, D_MODEL)
    t_pad = SUBLANES
    outs = {k: [] for k in ("kp", "vp", "ks", "vs", "shp", "wkp", "cvp", "shs", "wks", "cvs")}
    for l in range(depth):
        lam_init = 0.8 - 0.6 * math.exp(-0.3 * l)
        p = dict(mu_shift=mu_shift[l], w0=w0[l], w2=w2[l], a0=a0[l], a2=a2[l], g2=g2[l], k_k=k_k[l],
                 k_a=k_a[l], r_k=r_k[l], ln_x_w=ln_x_w[l], ln_x_b=ln_x_b[l], w_o=w_o[l],
                 g_mix_post=g_mix_post[l], g_ffn_pre=g_ffn_pre[l], w_up=w_up[l], conv_w=conv_w[l],
                 conv_b=conv_b[l], w_down=w_down[l], g_ffn_post=g_ffn_post[l], w_ple=w_ple[l],
                 w_ple_gate=w_ple_gate[l], g_ple=g_ple[l])
        w_in_bf = w_in[l].astype(BF16)
        g_pre = g_mix_pre[l].reshape(1, D_MODEL)
        lams = [z[l].reshape(1, A_DH) for z in (lam_q1, lam_k1, lam_q2, lam_k2)]
        gsub = g_subln[l].reshape(1, A_DV)
        pa = _rwkv_param_arrays(p)
        wts, halves = _post_weights(p)

        q, k, v, kb, vb, pr = _inproj(xp, g_pre, w_in_bf)
        oa = _attn_prompt(lams, gsub, q, kb, vb, bp, sp, lam_init)
        orw, wkv_p = _rwkv_prompt(pr, jnp.zeros((bp, N_SHIFT), F32),
                                  jnp.zeros((bp, R_HEADS, R_HEAD, R_HEAD), F32), pa, bp, sp)
        zc = jnp.zeros((bp, CONV_W - 1, D_FF_PAD), F32)
        xp, cng, cnv = _post(xp, oa, orw, p_prompt[l].reshape(bp * sp, D_PLE), wts, zc, zc, bp, 1)
        outs["kp"].append(k.reshape(bp, sp, A_HEADS, A_DK))
        outs["vp"].append(v.reshape(bp, sp, A_HEADS, A_DV))
        outs["shp"].append(pr.reshape(bp, sp, N_SHIFT)[:, sp - 1])
        outs["wkp"].append(wkv_p)
        outs["cvp"].append(jnp.concatenate([cng[..., :D_FF], cnv[..., :D_FF]], axis=-1))

        q, k, v, kb, vb, pr = _inproj(xs, g_pre, w_in_bf)
        q_ht = jnp.swapaxes(q.reshape(bd, td, A_HEADS, A_DK), 1, 2).reshape(bd * A_HEADS * td, A_DK)
        oa = _attn_sample(lams, gsub, q_ht, k.reshape(bd * td * A_HEADS, A_DK), v.reshape(bd * td * A_HEADS, A_DV),
                          cache_k.reshape(depth, n_pool, page * A_HEADS, A_DK),
                          cache_v.reshape(depth, n_pool, page * A_HEADS, A_DV), l, page_table, td, lam_init)
        oa = jnp.swapaxes(oa.reshape(bd, A_HEADS, td, A_DV), 1, 2).reshape(bd, td, D_ATT)
        pr3 = pr.reshape(bd, td, N_SHIFT)
        prev3 = jnp.concatenate([state_shift[l][:, None, :], pr3[:, :td - 1]], axis=1)
        pad_t = ((0, 0), (0, t_pad - td), (0, 0))
        orw, wkv_s = _rwkv_sample(jnp.pad(pr3, pad_t).reshape(bd * t_pad, N_SHIFT),
                                  jnp.pad(prev3, pad_t).reshape(bd * t_pad, N_SHIFT),
                                  state_wkv[l], pa, bd, t_pad, td)
        orw = orw.reshape(bd, t_pad, D_RWKV)[:, :td]
        tmaj = lambda z: jnp.swapaxes(z, 0, 1).reshape(td * bd, z.shape[-1])
        c0g, c0v = halves(jnp.swapaxes(state_conv[l], 0, 1).reshape(1, (CONV_W - 1) * bd, 2 * D_FF))
        ys, cng, cnv = _post(tmaj(xs.reshape(bd, td, D_MODEL)), tmaj(oa), tmaj(orw), tmaj(p_sample[l]),
                             wts, c0g, c0v, 1, bd)
        xs = jnp.swapaxes(ys.reshape(td, bd, D_MODEL), 0, 1).reshape(bd * td, D_MODEL)
        cvs = jnp.concatenate([cng[..., :D_FF], cnv[..., :D_FF]], axis=-1).reshape(CONV_W - 1, bd, 2 * D_FF)
        outs["ks"].append(k.reshape(bd, td, A_HEADS, A_DK))
        outs["vs"].append(v.reshape(bd, td, A_HEADS, A_DV))
        outs["shs"].append(pr3[:, td - 1])
        outs["wks"].append(wkv_s)
        outs["cvs"].append(jnp.swapaxes(cvs, 0, 1))
    st = lambda key: jnp.stack(outs[key])
    return (xp.reshape(bp, sp, D_MODEL), xs.reshape(bd, td, D_MODEL), st("kp"), st("vp"), st("ks"), st("vs"),
            st("shp"), st("wkp"), st("cvp"), st("shs"), st("wks"), st("cvs"))
```

```python
import functools
import math

import jax
import jax.numpy as jnp
from jax import lax
from jax.experimental import pallas as pl
from jax.experimental.pallas import tpu as pltpu

F32 = jnp.float32
BF16 = jnp.bfloat16

D_MODEL = 1024
A_HEADS = 4
A_DH = 64
A_DK = 2 * A_DH
A_DV = 2 * A_DH
D_ATT = A_HEADS * A_DK
R_HEAD = 64
R_HEADS = 8
D_RWKV = R_HEADS * R_HEAD
LORA_W = 64
LORA_A = 64
LORA_G = 128
N_SHIFT = 3 * D_RWKV + LORA_W + LORA_A + LORA_G
N_IN = 3 * D_ATT + N_SHIFT
D_FF = 2752
CONV_W = 3
D_PLE = 256
NORM_EPS = 1e-6
GN_EPS = 64e-5
ATT_SCALE = A_DH ** -0.5
LOG2E = math.log2(math.e)

LANES = 128
SUBLANES = 8
VMEM_LIMIT_BYTES = 56 * 1024 * 1024

D_FF_PAD = ((D_FF + LANES - 1) // LANES) * LANES
PAIR = 2 * R_HEAD
RWKV_COLS = 4 * PAIR
N_COL_BLOCKS = D_RWKV // RWKV_COLS
X_COLS = LORA_W + LORA_A + LORA_G
CHUNK = 64
WKV_HEAD_STAGES = 2
ATTN_HEADS_PER_LOOP = 4


def _params(n_axes):
    return pltpu.CompilerParams(
        dimension_semantics=("arbitrary",) * n_axes,
        vmem_limit_bytes=VMEM_LIMIT_BYTES,
    )


def _const_spec(shape):
    zeros = (0,) * len(shape)
    return pl.BlockSpec(shape, lambda *_: zeros, pipeline_mode=pl.Buffered(1))


def _rms(x, g):
    return x * lax.rsqrt(jnp.mean(x * x, axis=-1, keepdims=True) + NORM_EPS) * g


def _dot(a, b):
    return jnp.dot(a, b, preferred_element_type=F32)


def _dot_nt(a, b):
    return lax.dot_general(a, b, (((1,), (1,)), ((), ())), preferred_element_type=F32)


def _split3(x):
    hi = x.astype(BF16)
    r1 = x - hi.astype(F32)
    mid = r1.astype(BF16)
    lo = (r1 - mid.astype(F32)).astype(BF16)
    return hi, mid, lo


def _dot_sel(x, sel, passes=3):
    out = None
    for part in _split3(x)[:passes]:
        t = _dot(part, sel)
        out = t if out is None else out + t
    return out


def _sel_dot(sel, x, passes=3):
    out = None
    for part in _split3(x)[:passes]:
        t = _dot(sel, part)
        out = t if out is None else out + t
    return out


def _sel_dot_nt(sel, x, passes=3):
    out = None
    for part in _split3(x)[:passes]:
        t = _dot_nt(sel, part)
        out = t if out is None else out + t
    return out


def _iota(shape, dim):
    return lax.broadcasted_iota(jnp.int32, shape, dim)


def _inproj_body(x_ref, g_ref, w_ref, q_ref, k_ref, v_ref, kb_ref, vb_ref, pr_ref):
    h = _rms(x_ref[...], g_ref[...]).astype(BF16)
    q = _dot(h, w_ref[:, 0:D_ATT])
    q_ref[...] = (q * (ATT_SCALE * LOG2E)).astype(BF16)
    k = _dot(h, w_ref[:, D_ATT:2 * D_ATT])
    kb_ref[...] = k.astype(BF16)
    v = _dot(h, w_ref[:, 2 * D_ATT:3 * D_ATT])
    vb_ref[...] = v.astype(BF16)
    for hd in range(A_HEADS):
        k_ref[:, hd, :] = k[:, hd * A_DK:(hd + 1) * A_DK]
        v_ref[:, hd, :] = v[:, hd * A_DV:(hd + 1) * A_DV]
    pr_ref[...] = _dot(h, w_ref[:, 3 * D_ATT:N_IN])


def _inproj(x, g, w_in_bf):
    n = x.shape[0]
    tm = min(512, n)
    row = lambda i: (i, 0)
    outs = (
        jax.ShapeDtypeStruct((n, D_ATT), BF16),
        jax.ShapeDtypeStruct((n, A_HEADS, A_DK), F32),
        jax.ShapeDtypeStruct((n, A_HEADS, A_DV), F32),
        jax.ShapeDtypeStruct((n, D_ATT), BF16),
        jax.ShapeDtypeStruct((n, D_ATT), BF16),
        jax.ShapeDtypeStruct((n, N_SHIFT), F32),
    )
    return pl.pallas_call(
        _inproj_body,
        grid=(n // tm,),
        in_specs=[
            pl.BlockSpec((tm, D_MODEL), row),
            _const_spec((1, D_MODEL)),
            _const_spec((D_MODEL, N_IN)),
        ],
        out_specs=tuple(pl.BlockSpec((tm,) + s.shape[1:], lambda i, nd=len(s.shape): (i,) + (0,) * (nd - 1))
                        for s in outs),
        out_shape=outs,
        compiler_params=_params(1),
        name="inproj",
    )(x, g, w_in_bf)


def _diff_lambda(lq1, lk1, lq2, lk2, lam_init):
    s1 = jnp.sum(lq1 * lk1, axis=-1, keepdims=True)
    s2 = jnp.sum(lq2 * lk2, axis=-1, keepdims=True)
    return jnp.exp(s1) - jnp.exp(s2) + lam_init


def _stack_maps(q):
    lane = _iota(q.shape, 1)
    zero = jnp.zeros_like(q)
    return jnp.concatenate([jnp.where(lane < A_DH, q, zero), jnp.where(lane >= A_DH, q, zero)], axis=0)


def _subln(o, gsub, lam_init):
    return _rms(o, gsub) * (1.0 - lam_init)


def _attn_prompt_body(lq1, lk1, lq2, lk2, gsub_ref, q_ref, k_ref, v_ref, o_ref, *, tq, lam_init):
    qi = pl.program_id(1)
    lam = _diff_lambda(lq1[...], lk1[...], lq2[...], lk2[...], lam_init)
    heads = [slice(h * A_DK, (h + 1) * A_DK) for h in range(A_HEADS)]
    qs = [_stack_maps(q_ref[:, cols]) for cols in heads]
    causal = _iota((2 * tq, tq), 1) <= _iota((2 * tq, tq), 0) % tq

    def tile(j, carry, group, diagonal):
        off = pl.multiple_of(j * tq, tq)
        out = []
        for h, (m, l, acc) in zip(group, carry):
            s = _dot_nt(qs[h], k_ref[pl.ds(off, tq), heads[h]])
            if diagonal:
                s = jnp.where(causal, s, -jnp.inf)
            m_new = jnp.maximum(m, jnp.max(s, axis=-1, keepdims=True))
            p = jnp.exp2(s - m_new)
            alpha = jnp.exp2(m - m_new)
            l = alpha * l + jnp.sum(p, axis=-1, keepdims=True)
            acc = alpha * acc + _dot(p.astype(BF16), v_ref[pl.ds(off, tq), heads[h]])
            out.append((m_new, l, acc))
        return tuple(out)

    for first in range(0, A_HEADS, ATTN_HEADS_PER_LOOP):
        group = tuple(range(first, first + ATTN_HEADS_PER_LOOP))
        init = tuple((jnp.full((2 * tq, 1), -jnp.inf, F32), jnp.zeros((2 * tq, 1), F32),
                      jnp.zeros((2 * tq, A_DV), F32)) for _ in group)
        carry = lax.fori_loop(0, qi, lambda j, c, group=group: tile(j, c, group, False), init)
        carry = tile(qi, carry, group, True)
        for h, (_, l, acc) in zip(group, carry):
            o = acc / l
            o = o[:tq] - lam * o[tq:]
            o_ref[:, heads[h]] = _subln(o, gsub_ref[...], lam_init).astype(o_ref.dtype)


def _attn_prompt(lams, gsub, q, kb, vb, b, s, lam_init):
    tq = min(512, s)
    nq = s // tq
    lam_spec = _const_spec((1, A_DH))
    return pl.pallas_call(
        functools.partial(_attn_prompt_body, tq=tq, lam_init=lam_init),
        grid=(b, nq),
        in_specs=[
            lam_spec, lam_spec, lam_spec, lam_spec,
            _const_spec((1, A_DV)),
            pl.BlockSpec((tq, D_ATT), lambda bi, qi: (bi * nq + qi, 0)),
            pl.BlockSpec((s, D_ATT), lambda bi, qi: (bi, 0)),
            pl.BlockSpec((s, D_ATT), lambda bi, qi: (bi, 0)),
        ],
        out_specs=pl.BlockSpec((tq, D_ATT), lambda bi, qi: (bi * nq + qi, 0)),
        out_shape=jax.ShapeDtypeStruct((b * s, D_ATT), BF16),
        compiler_params=_params(2),
        name="attn_prompt",
    )(*lams, gsub, q, kb, vb)


def _attn_sample_body(pt_ref, lq1, lk1, lq2, lk2, gsub_ref, q_ref, kn_ref, vn_ref, *rest,
                      n_pages, t, lam_init):
    del pt_ref
    kp_refs = rest[:n_pages]
    vp_refs = rest[n_pages:2 * n_pages]
    o_ref = rest[2 * n_pages]
    lam = _diff_lambda(lq1[...], lk1[...], lq2[...], lk2[...], lam_init)
    nq = A_HEADS * t
    qs = _stack_maps(q_ref[...])
    q_head = (_iota((2 * nq, 1), 0) % nq) // t
    q_tok = _iota((2 * nq, 1), 0) % t
    width = kp_refs[0].shape[0]
    past_ok = (_iota((2 * nq, width), 1) % A_HEADS) == q_head
    new_lane = _iota((2 * nq, nq), 1)
    new_ok = ((new_lane % A_HEADS) == q_head) & ((new_lane // A_HEADS) <= q_tok)
    s_past = [jnp.where(past_ok, _dot_nt(qs, kp[...].astype(BF16)), -jnp.inf) for kp in kp_refs]
    s_new = jnp.where(new_ok, _dot_nt(qs, kn_ref[...].astype(BF16)), -jnp.inf)
    m = jnp.max(s_new, axis=-1, keepdims=True)
    for sp in s_past:
        m = jnp.maximum(m, jnp.max(sp, axis=-1, keepdims=True))
    p_new = jnp.exp2(s_new - m)
    l = jnp.sum(p_new, axis=-1, keepdims=True)
    acc = _dot(p_new.astype(BF16), vn_ref[...].astype(BF16))
    for sp, vp in zip(s_past, vp_refs):
        p = jnp.exp2(sp - m)
        l = l + jnp.sum(p, axis=-1, keepdims=True)
        acc = acc + _dot(p.astype(BF16), vp[...].astype(BF16))
    o = acc / l
    o = o[:nq] - lam * o[nq:]
    o_ref[...] = _subln(o, gsub_ref[...], lam_init).astype(o_ref.dtype)


def _attn_sample(lams, gsub, q, k_new, v_new, cache_k, cache_v, layer, page_table, t, lam_init):
    bd, n_pages = page_table.shape
    nq = A_HEADS * t
    rows = cache_k.shape[2]
    lam_spec = pl.BlockSpec((1, A_DH), lambda b, pt: (0, 0))
    tok_spec = pl.BlockSpec((nq, A_DK), lambda b, pt: (b, 0))

    def page_spec(p):
        return pl.BlockSpec((None, None, rows, A_DK), lambda b, pt, p=p: (layer, pt[b, p], 0, 0))

    grid_spec = pltpu.PrefetchScalarGridSpec(
        num_scalar_prefetch=1,
        grid=(bd,),
        in_specs=[lam_spec] * 4 + [pl.BlockSpec((1, A_DV), lambda b, pt: (0, 0)), tok_spec, tok_spec, tok_spec]
        + [page_spec(p) for p in range(n_pages)] * 2,
        out_specs=tok_spec,
    )
    return pl.pallas_call(
        functools.partial(_attn_sample_body, n_pages=n_pages, t=t, lam_init=lam_init),
        grid_spec=grid_spec,
        out_shape=jax.ShapeDtypeStruct((bd * nq, A_DV), BF16),
        compiler_params=_params(1),
        name="attn_sample",
    )(page_table, *lams, gsub, q, k_new, v_new, *([cache_k] * n_pages), *([cache_v] * n_pages))


def _head_blocks(n):
    return (_iota((n, n), 0) // R_HEAD) == (_iota((n, n), 1) // R_HEAD)


def _softplus(z):
    return jnp.maximum(z, 0.0) + jnp.log1p(jnp.exp(-jnp.abs(z)))


def _rwkv_tokens(cur, prev, prm, seg_ones):
    mix = {c: cur[c] + (prev[c] - cur[c]) * prm["mu_" + c] for c in ("r", "k", "v", "x")}
    r, k, v = mix["r"], mix["k"], mix["v"]
    xwa = mix["x"][:, :LORA_W + LORA_A]
    xg = mix["x"][:, LORA_W + LORA_A:]
    w_in = prm["w0"] + _dot(jnp.tanh(xwa).astype(BF16), prm["w2"])
    lw = -jnp.exp(-_softplus(-w_in) - 0.5)
    a = jax.nn.sigmoid(prm["a0"] + _dot(xwa.astype(BF16), prm["a2"]))
    g = _dot(jax.nn.sigmoid(xg).astype(BF16), prm["g2"])
    kk = k * prm["k_k"]
    nrm = jnp.sqrt(_dot_sel(kk * kk, seg_ones, passes=1))
    kk = kk / jnp.maximum(nrm, 1e-12)
    k2 = k * (1.0 + (a - 1.0) * prm["k_a"])
    bonus = _dot_sel(r * k2 * prm["r_k"], seg_ones, passes=1) * v
    return r, lw, k2, v, kk, a, g, bonus


def _rwkv_finish(y, bonus, g, prm, seg_ones):
    mean = _dot_sel(y, seg_ones, passes=1) * (1.0 / R_HEAD)
    d = y - mean
    var = _dot_sel(d * d, seg_ones, passes=1) * (1.0 / R_HEAD)
    yn = d * lax.rsqrt(var + GN_EPS) * prm["ln_w"] + prm["ln_b"]
    return (yn + bonus) * g


def _stack_heads(x):
    lane = _iota(x.shape, 1)
    zero = jnp.zeros_like(x)
    return jnp.concatenate([jnp.where(lane < R_HEAD, x, zero), jnp.where(lane >= R_HEAD, x, zero)], axis=0)


def _wkv_front_stages(chunks):
    c = chunks[0][0].shape[0]
    n = 2 * c
    tril = (_iota((c, c), 1) <= _iota((c, c), 0)).astype(BF16)
    row = _iota((n, n), 0)
    col = _iota((n, n), 1)
    same = (row // c) == (col // c)
    strict = same & (col < row)
    incl = same & (col <= row)
    zero = jnp.zeros((n, n), F32)
    eye_b = (_iota((PAIR, PAIR), 0) == _iota((PAIR, PAIR), 1)).astype(BF16)
    env = {}

    def cumsum():
        env["big_l"] = [_sel_dot(tril, ch[1], passes=2) for ch in chunks]

    def operands():
        ops = []
        for (r, lw, k2, v, kk, a), big_l in zip(chunks, env["big_l"]):
            l_end = big_l[c - 1:c, :]
            e_in = jnp.exp(big_l)
            e_ex = jnp.exp(big_l - lw)
            e_neg = jnp.exp(-big_l)
            e_rem = jnp.exp(l_end - big_l)
            b = kk * a
            a_t = _stack_heads(-kk * e_ex)
            r_t = _stack_heads(r * e_in)
            ops.append(dict(
                l_end=l_end, a_t=a_t, r_t=r_t, a_tb=a_t.astype(BF16), r_tb=r_t.astype(BF16),
                b_t=_stack_heads(b * e_neg).astype(BF16), k_t=_stack_heads(k2 * e_neg).astype(BF16),
                b_h=_stack_heads(b * e_rem).astype(BF16), k_h=_stack_heads(k2 * e_rem).astype(BF16),
                v_s=_stack_heads(v).astype(BF16)))
        env["ops"] = ops

    def pair(name, lhs, rhs, mask):
        def run():
            env[name] = [jnp.where(mask, _dot_nt(o[lhs], o[rhs]), zero) for o in env["ops"]]
        return run

    def transposes():
        env["b_ht"] = [_dot_nt(eye_b, o["b_h"]).astype(BF16) for o in env["ops"]]
        env["k_ht"] = [_dot_nt(eye_b, o["k_h"]).astype(BF16) for o in env["ops"]]

    stages = [cumsum, operands, pair("m_ab", "a_tb", "b_t", strict), pair("m_ak", "a_tb", "k_t", strict),
              pair("m_rb", "r_tb", "b_t", incl), pair("m_rk", "r_tb", "k_t", incl), transposes]
    return stages, env


_WKV_FRONT_KEYS = ("ops", "m_ab", "m_ak", "m_rb", "m_rk", "b_ht", "k_ht")


def _wkv_back_stages(fronts, c):
    n = 2 * c
    steps = max(1, int(math.ceil(math.log2(c))))
    eye_n = jnp.where(_iota((n, n), 0) == _iota((n, n), 1), 1.0, 0.0)
    eye = (_iota((PAIR, PAIR), 0) == _iota((PAIR, PAIR), 1))
    env = {}

    def rhs():
        env.update({key: [x for f in fronts for x in f[key]] for key in _WKV_FRONT_KEYS})
        env["x"] = [jnp.concatenate([o["a_t"], _dot(mk.astype(BF16), o["v_s"])], axis=1).astype(BF16)
                    for o, mk in zip(env["ops"], env["m_ak"])]
        env["pb"] = [p.astype(BF16) for p in env["m_ab"]]
        env["t"] = [eye_n + p for p in env["m_ab"]]

    def square():
        env["pb"] = [_dot(pb, pb).astype(BF16) for pb in env["pb"]]

    def extend():
        env["t"] = [t + _dot(t.astype(BF16), pb) for t, pb in zip(env["t"], env["pb"])]

    def solve():
        env["w"] = [_dot(t.astype(BF16), x).astype(BF16) for t, x in zip(env["t"], env["x"])]

    def state_side():
        env["mg"] = [_dot(bt, w) for bt, w in zip(env["b_ht"], env["w"])]
        env["kv"] = [_dot(kt, o["v_s"]) for kt, o in zip(env["k_ht"], env["ops"])]

    def output_side():
        env["qy"] = [_dot(mr.astype(BF16), w) for mr, w in zip(env["m_rb"], env["w"])]
        env["rkv"] = [_dot(mr.astype(BF16), o["v_s"]) for mr, o in zip(env["m_rk"], env["ops"])]

    def finish():
        out = []
        for o, mg, kv, qy, rk in zip(env["ops"], env["mg"], env["kv"], env["qy"], env["rkv"]):
            m_mat = jnp.where(eye, jnp.exp(o["l_end"]), 0.0) + mg[:, :PAIR]
            g_mat = mg[:, PAIR:] + kv
            q_mat = o["r_t"] + qy[:, :PAIR]
            y0 = qy[:, PAIR:] + rk
            out.append((q_mat.astype(BF16), y0, m_mat.astype(BF16), g_mat))
        env["out"] = out

    stages = [rhs]
    for _ in range(steps - 1):
        stages += [square, extend]
    stages += [solve, state_side, output_side, finish]
    return stages, env


def _wkv_prepare(chunks):
    front, env = _wkv_front_stages(chunks)
    for stage in front:
        stage()
    back, env = _wkv_back_stages([env], chunks[0][0].shape[0])
    for stage in back:
        stage()
    return env["out"]


def _wkv_apply(prep, st):
    q_mat, y0, m_mat, g_mat = prep
    c = y0.shape[0] // 2
    stb = st.astype(BF16)
    y_st = _dot(q_mat, stb) + y0
    return y_st[:c] + y_st[c:], _dot(m_mat, stb) + g_mat


def _state_in(s0):
    x = s0.reshape(PAIR, R_HEAD)
    sel = (_iota((PAIR, R_HEAD), 0) % R_HEAD == _iota((PAIR, R_HEAD), 1)).astype(BF16)
    full = _sel_dot_nt(sel, x)
    return jnp.where(_head_blocks(PAIR), full, 0.0)


def _state_out(st):
    folded = st[:R_HEAD] + st[R_HEAD:]
    eye = (_iota((PAIR, PAIR), 0) == _iota((PAIR, PAIR), 1)).astype(BF16)
    return _sel_dot_nt(eye, folded).reshape(2, R_HEAD, R_HEAD)


_RWKV_PARAM_NAMES = ("mu_r", "mu_k", "mu_v", "mu_x", "w0", "w2", "a0", "a2", "g2", "k_k", "k_a", "r_k",
                     "ln_w", "ln_b")


def _split_pairs(z, rows):
    return [z[rows, p * PAIR:(p + 1) * PAIR] for p in range(z.shape[1] // PAIR)]


def _run_chains(preps, states, between=None):
    states = list(states)
    ys = [[] for _ in preps]
    for i in range(len(preps[0])):
        for p, chain in enumerate(preps):
            y, states[p] = _wkv_apply(chain[i], states[p])
            ys[p].append(y)
        if between is not None:
            between()
    return jnp.concatenate([jnp.concatenate(col, axis=0) for col in ys], axis=1), states


def _rwkv_prompt_body(*refs, n_chunks):
    n_prm = len(_RWKV_PARAM_NAMES)
    pr = dict(zip(("r", "k", "v", "x"), refs[0:4]))
    sh = dict(zip(("r", "k", "v", "x"), refs[4:8]))
    prm = {nm: ref[...] for nm, ref in zip(_RWKV_PARAM_NAMES, refs[8:8 + n_prm])}
    s0_ref = refs[8 + n_prm]
    o_ref, s_out_ref = refs[9 + n_prm], refs[10 + n_prm]
    scratch = refs[11 + n_prm:]
    carry = dict(zip(("r", "k", "v", "x"), scratch[0:4]))
    st_ref, q_sc, y0_sc, m_sc, g_sc, bonus_sc, gate_sc = scratch[4:11]
    g = pl.program_id(2)
    tc, width = bonus_sc.shape
    n_pairs = width // PAIR
    cl = tc // n_chunks
    seg_ones = _head_blocks(width).astype(BF16)

    @pl.when(g == 0)
    def _():
        for c in carry:
            carry[c][...] = jnp.broadcast_to(sh[c][...], carry[c].shape)
        for p in range(n_pairs):
            st_ref[p] = _state_in(s0_ref[2 * p:2 * p + 2])
        for ref in (q_sc, y0_sc, m_sc, g_sc, bonus_sc, gate_sc):
            ref[...] = jnp.zeros(ref.shape, ref.dtype)

    def stored():
        return [[(q_sc[p * n_chunks + i], y0_sc[p * n_chunks + i], m_sc[p * n_chunks + i],
                  g_sc[p * n_chunks + i]) for i in range(n_chunks)] for p in range(n_pairs)]

    def emit(y, bonus, gate, tile):
        rows = pl.ds(pl.multiple_of(tile * tc, tc), tc)
        o_ref[rows, :] = _rwkv_finish(y, bonus, gate, prm, seg_ones).astype(o_ref.dtype)

    old = stored()
    old_bonus, old_gate = bonus_sc[...], gate_sc[...]
    st_in = [st_ref[p] for p in range(n_pairs)]

    cur = {c: pr[c][...] for c in pr}
    prev = {}
    for c in cur:
        first = _iota(cur[c].shape, 0) == 0
        prev[c] = jnp.where(first, carry[c][0:1, :], pltpu.roll(cur[c], 1, 0))
    for c in cur:
        carry[c][...] = jnp.broadcast_to(cur[c][tc - 1:tc, :], carry[c].shape)
    r, lw, k2, v, kk, a, gate, bonus = _rwkv_tokens(cur, prev, prm, seg_ones)
    per_chunk = [list(zip(*(_split_pairs(z, slice(i * cl, (i + 1) * cl)) for z in (r, lw, k2, v, kk, a))))
                 for i in range(n_chunks)]
    fronts = [_wkv_front_stages([per_chunk[i][p] for i in range(n_chunks)]) for p in range(n_pairs)]
    todo = []
    for p, (stages, _) in enumerate(fronts):
        todo += stages[:WKV_HEAD_STAGES]
        if p > 0:
            todo += fronts[p - 1][0][WKV_HEAD_STAGES:]
    todo += fronts[-1][0][WKV_HEAD_STAGES:]
    back_stages, back_env = _wkv_back_stages([env for _, env in fronts], cl)
    todo += back_stages
    per = max(1, len(todo) // (n_chunks + 1))

    def between():
        for stage in todo[:per]:
            stage()
        del todo[:per]

    between()
    y_old, st = _run_chains(old, st_in, between)
    for stage in todo:
        stage()
    emit(y_old, old_bonus, old_gate, jnp.maximum(g - 1, 0))
    for p in range(n_pairs):
        st_ref[p] = jnp.where(g == 0, st_in[p], st[p])
    prepared = back_env["out"]
    for i, (q_mat, y0, m_mat, g_mat) in enumerate(prepared):
        q_sc[i], y0_sc[i], m_sc[i], g_sc[i] = q_mat, y0, m_mat, g_mat
    bonus_sc[...] = bonus
    gate_sc[...] = gate

    @pl.when(g == pl.num_programs(2) - 1)
    def _():
        y_new, st_end = _run_chains(stored(), [st_ref[p] for p in range(n_pairs)])
        emit(y_new, bonus_sc[...], gate_sc[...], g)
        for p in range(n_pairs):
            s_out_ref[2 * p:2 * p + 2] = _state_out(st_end[p])


def _rwkv_sample_body(*refs, n_seq, t_pad, t_valid):
    n_prm = len(_RWKV_PARAM_NAMES)
    pr = dict(zip(("r", "k", "v", "x"), refs[0:4]))
    pv = dict(zip(("r", "k", "v", "x"), refs[4:8]))
    prm = {nm: ref[...] for nm, ref in zip(_RWKV_PARAM_NAMES, refs[8:8 + n_prm])}
    s0_ref = refs[8 + n_prm]
    o_ref, s_out_ref = refs[9 + n_prm], refs[10 + n_prm]
    cur = {c: pr[c][...] for c in pr}
    prev = {c: pv[c][...] for c in pv}
    width = o_ref.shape[1]
    n_pairs = width // PAIR
    seg_ones = _head_blocks(width).astype(BF16)
    r, lw, k2, v, kk, a, gate, bonus = _rwkv_tokens(cur, prev, prm, seg_ones)
    valid = (_iota(r.shape, 0) % t_pad) < t_valid
    zero = jnp.zeros_like(r)
    r, lw, k2, v, kk = (jnp.where(valid, z, zero) for z in (r, lw, k2, v, kk))
    per_seq = [list(zip(*(_split_pairs(z, slice(i * t_pad, (i + 1) * t_pad)) for z in (r, lw, k2, v, kk, a))))
               for i in range(n_seq)]
    chunks = [per_seq[i][p] for i in range(n_seq) for p in range(n_pairs)]
    states = [_state_in(s0_ref[i, 2 * p:2 * p + 2]) for i in range(n_seq) for p in range(n_pairs)]
    applied = [_wkv_apply(prep, st) for prep, st in zip(_wkv_prepare(chunks), states)]
    rows = []
    for i in range(n_seq):
        for p in range(n_pairs):
            s_out_ref[i, 2 * p:2 * p + 2] = _state_out(applied[i * n_pairs + p][1])
        rows.append(jnp.concatenate([applied[i * n_pairs + p][0] for p in range(n_pairs)], axis=1))
    y = jnp.concatenate(rows, axis=0)
    o_ref[...] = _rwkv_finish(y, bonus, gate, prm, seg_ones).astype(o_ref.dtype)


def _rwkv_param_arrays(p):
    z_w = jnp.zeros((LORA_A, D_RWKV), BF16)
    z_a = jnp.zeros((LORA_W, D_RWKV), BF16)
    return dict(
        mu=p["mu_shift"].reshape(1, N_SHIFT),
        w0=p["w0"].reshape(1, D_RWKV),
        w2=jnp.concatenate([p["w2"].astype(BF16), z_w], axis=0),
        a0=p["a0"].reshape(1, D_RWKV),
        a2=jnp.concatenate([z_a, p["a2"].astype(BF16)], axis=0),
        g2=p["g2"].astype(BF16),
        k_k=p["k_k"].reshape(1, D_RWKV),
        k_a=p["k_a"].reshape(1, D_RWKV),
        r_k=p["r_k"].reshape(1, D_RWKV),
        ln_w=p["ln_x_w"].reshape(1, D_RWKV),
        ln_b=p["ln_x_b"].reshape(1, D_RWKV),
    )


def _rwkv_param_specs(block_of):
    x_blk = (3 * D_RWKV) // X_COLS

    def vec(off):
        return pl.BlockSpec((1, RWKV_COLS), lambda *i: (0, off + block_of(*i)))

    def mat(rows):
        return pl.BlockSpec((rows, RWKV_COLS), lambda *i: (0, block_of(*i)))

    return [
        vec(0), vec(N_COL_BLOCKS), vec(2 * N_COL_BLOCKS),
        pl.BlockSpec((1, X_COLS), lambda *i: (0, x_blk)),
        vec(0), mat(LORA_W + LORA_A), vec(0), mat(LORA_W + LORA_A), mat(LORA_G),
        vec(0), vec(0), vec(0), vec(0), vec(0),
    ]


def _rwkv_param_operands(pa):
    return [pa["mu"], pa["mu"], pa["mu"], pa["mu"], pa["w0"], pa["w2"], pa["a0"], pa["a2"], pa["g2"],
            pa["k_k"], pa["k_a"], pa["r_k"], pa["ln_w"], pa["ln_b"]]


def _col_specs(rows, row_of, block_of):
    x_blk = (3 * D_RWKV) // X_COLS
    specs = [pl.BlockSpec((rows, RWKV_COLS), lambda *i, o=o: (row_of(*i), o * N_COL_BLOCKS + block_of(*i)))
             for o in range(3)]
    specs.append(pl.BlockSpec((rows, X_COLS), lambda *i: (row_of(*i), x_blk)))
    return specs


def _rwkv_prompt(pr, shift0, s0, pa, b, s):
    tc = min(512, s)
    nt = s // tc
    n_chunks = max(1, tc // CHUNK)
    cl = tc // n_chunks
    pairs = RWKV_COLS // PAIR
    block_of = lambda bi, ci, gi: ci
    row_of = lambda bi, ci, gi: bi * nt + gi
    x_blk = (3 * D_RWKV) // X_COLS
    shift_specs = [pl.BlockSpec((None, 1, RWKV_COLS), lambda bi, ci, gi, o=o: (bi, 0, o * N_COL_BLOCKS + ci))
                   for o in range(3)]
    shift_specs.append(pl.BlockSpec((None, 1, X_COLS), lambda bi, ci, gi: (bi, 0, x_blk)))
    state_spec = pl.BlockSpec((None, 2 * pairs, R_HEAD, R_HEAD), lambda bi, ci, gi: (bi, ci, 0, 0))
    shift3 = shift0.reshape(b, 1, N_SHIFT)
    return pl.pallas_call(
        functools.partial(_rwkv_prompt_body, n_chunks=n_chunks),
        grid=(b, N_COL_BLOCKS, nt),
        in_specs=_col_specs(tc, row_of, block_of) + shift_specs + _rwkv_param_specs(block_of) + [state_spec],
        out_specs=(pl.BlockSpec((s, RWKV_COLS), lambda bi, ci, gi: (bi, ci)), state_spec),
        out_shape=(jax.ShapeDtypeStruct((b * s, D_RWKV), BF16),
                   jax.ShapeDtypeStruct((b, R_HEADS, R_HEAD, R_HEAD), F32)),
        scratch_shapes=[pltpu.VMEM((SUBLANES, RWKV_COLS), F32)] * 3 + [
            pltpu.VMEM((SUBLANES, X_COLS), F32),
            pltpu.VMEM((pairs, PAIR, PAIR), F32),
            pltpu.VMEM((pairs * n_chunks, 2 * cl, PAIR), BF16),
            pltpu.VMEM((pairs * n_chunks, 2 * cl, PAIR), F32),
            pltpu.VMEM((pairs * n_chunks, PAIR, PAIR), BF16),
            pltpu.VMEM((pairs * n_chunks, PAIR, PAIR), F32),
            pltpu.VMEM((tc, RWKV_COLS), F32),
            pltpu.VMEM((tc, RWKV_COLS), F32),
        ],
        compiler_params=_params(3),
        name="rwkv_prompt",
    )(pr, pr, pr, pr, shift3, shift3, shift3, shift3, *_rwkv_param_operands(pa), s0)


def _rwkv_sample(pr_pad, prev_pad, s0, pa, bd, t_pad, t_valid):
    n_seq = min(8, bd)
    rows = n_seq * t_pad
    pairs = RWKV_COLS // PAIR
    block_of = lambda gi, ci: ci
    row_of = lambda gi, ci: gi
    state_spec = pl.BlockSpec((n_seq, 2 * pairs, R_HEAD, R_HEAD), lambda gi, ci: (gi, ci, 0, 0))
    return pl.pallas_call(
        functools.partial(_rwkv_sample_body, n_seq=n_seq, t_pad=t_pad, t_valid=t_valid),
        grid=(bd // n_seq, N_COL_BLOCKS),
        in_specs=_col_specs(rows, row_of, block_of) * 2 + _rwkv_param_specs(block_of) + [state_spec],
        out_specs=(pl.BlockSpec((rows, RWKV_COLS), lambda gi, ci: (gi, ci)), state_spec),
        out_shape=(jax.ShapeDtypeStruct((bd * t_pad, D_RWKV), BF16),
                   jax.ShapeDtypeStruct((bd, R_HEADS, R_HEAD, R_HEAD), F32)),
        compiler_params=_params(2),
        name="rwkv_sample",
    )(pr_pad, pr_pad, pr_pad, pr_pad, prev_pad, prev_pad, prev_pad, prev_pad, *_rwkv_param_operands(pa), s0)


def _shifted(u, carry_ref, shift, j):
    n_state = (CONV_W - 1) * shift
    if shift == 1:
        out = pltpu.roll(u, j, 0)
        row = _iota(u.shape, 0)
        for i in range(j):
            out = jnp.where(row == i, carry_ref[n_state - j + i:n_state - j + i + 1, :], out)
        return out
    keep = u.shape[0] - j * shift
    return jnp.concatenate([carry_ref[n_state - j * shift:n_state, :], u[:keep]], axis=0)


def _post_body(x_ref, oa_ref, or_ref, pe_ref, wo_ref, gmp_ref, gfp_ref, wg_ref, wv_ref, cwg_ref, cwv_ref,
               cbg_ref, cbv_ref, wd_ref, gfo_ref, wple_ref, wgate_ref, gple_ref, c0g_ref, c0v_ref,
               y_ref, cng_ref, cnv_ref, cg_ref, cv_ref, *, shift, n_fchunks):
    ti = pl.program_id(1)
    n_state = (CONV_W - 1) * shift
    tm = x_ref.shape[0]

    @pl.when(ti == 0)
    def _():
        cg_ref[0:n_state, :] = c0g_ref[...]
        cv_ref[0:n_state, :] = c0v_ref[...]

    mix = _dot(jnp.concatenate([oa_ref[...], or_ref[...]], axis=1), wo_ref[...])
    x1 = x_ref[...] + _rms(mix, gmp_ref[...])
    h = _rms(x1, gfp_ref[...]).astype(BF16)
    fc = D_FF_PAD // n_fchunks
    acc = jnp.zeros((tm, D_MODEL), F32)
    for c in range(n_fchunks):
        cols = slice(c * fc, (c + 1) * fc)
        halves = []
        for w_ref, cw_ref, cb_ref, carry_ref in ((wg_ref, cwg_ref, cbg_ref, cg_ref),
                                                 (wv_ref, cwv_ref, cbv_ref, cv_ref)):
            u = _dot(h, w_ref[:, cols])
            carry = carry_ref.at[:, cols]
            conv = cb_ref[:, cols] + cw_ref[0:1, cols] * _shifted(u, carry, shift, 2)
            conv = conv + cw_ref[1:2, cols] * _shifted(u, carry, shift, 1)
            conv = conv + cw_ref[2:3, cols] * u
            carry_ref[0:n_state, cols] = u[tm - n_state:, :]
            halves.append(conv)
        act = jax.nn.gelu(halves[0], approximate=True) * halves[1]
        acc = acc + _dot(act.astype(BF16), wd_ref[cols, :])
    x2 = x1 + _rms(acc, gfo_ref[...])
    gate = jax.nn.sigmoid(_dot(x2.astype(BF16), wgate_ref[...]))
    ple = _dot(pe_ref[...].astype(BF16), wple_ref[...]) * gate
    y_ref[...] = x2 + _rms(ple, gple_ref[...])

    @pl.when(ti == pl.num_programs(1) - 1)
    def _():
        cng_ref[...] = cg_ref[0:n_state, :]
        cnv_ref[...] = cv_ref[0:n_state, :]


def _post(x, oa, orw, pe, wts, conv0_g, conv0_v, n_seq, shift):
    n = x.shape[0]
    rows_per_seq = n // n_seq
    tm = min(512, rows_per_seq) if shift == 1 else rows_per_seq
    nt = rows_per_seq // tm
    n_state = (CONV_W - 1) * shift
    n_carry = max(SUBLANES, n_state)
    row = lambda si, ti: (si * nt + ti, 0)
    state_spec = pl.BlockSpec((None, n_state, D_FF_PAD), lambda si, ti: (si, 0, 0))
    w_arrays = [wts[k] for k in ("w_o", "g_mix_post", "g_ffn_pre", "w_gate_up", "w_val_up", "cw_g", "cw_v",
                                 "cb_g", "cb_v", "w_down", "g_ffn_post", "w_ple", "w_ple_gate", "g_ple")]
    return pl.pallas_call(
        functools.partial(_post_body, shift=shift, n_fchunks=1),
        grid=(n_seq, nt),
        in_specs=[
            pl.BlockSpec((tm, D_MODEL), row),
            pl.BlockSpec((tm, D_ATT), row),
            pl.BlockSpec((tm, D_RWKV), row),
            pl.BlockSpec((tm, D_PLE), row),
        ] + [_const_spec(w.shape) for w in w_arrays] + [state_spec, state_spec],
        out_specs=(pl.BlockSpec((tm, D_MODEL), row), state_spec, state_spec),
        out_shape=(jax.ShapeDtypeStruct((n, D_MODEL), F32),
                   jax.ShapeDtypeStruct((n_seq, n_state, D_FF_PAD), F32),
                   jax.ShapeDtypeStruct((n_seq, n_state, D_FF_PAD), F32)),
        scratch_shapes=[pltpu.VMEM((n_carry, D_FF_PAD), F32)] * 2,
        compiler_params=_params(2),
        name="post",
    )(x, oa, orw, pe, *w_arrays, conv0_g, conv0_v)


def _post_weights(p):
    pad_c = D_FF_PAD - D_FF

    def halves(a):
        widths = [(0, 0)] * (a.ndim - 1) + [(0, pad_c)]
        return jnp.pad(a[..., :D_FF], widths), jnp.pad(a[..., D_FF:], widths)

    w_g, w_v = halves(p["w_up"].astype(BF16))
    cw_g, cw_v = halves(p["conv_w"])
    cb_g, cb_v = halves(p["conv_b"].reshape(1, 2 * D_FF))
    return dict(
        w_o=p["w_o"].astype(BF16),
        g_mix_post=p["g_mix_post"].reshape(1, D_MODEL),
        g_ffn_pre=p["g_ffn_pre"].reshape(1, D_MODEL),
        w_gate_up=w_g, w_val_up=w_v, cw_g=cw_g, cw_v=cw_v, cb_g=cb_g, cb_v=cb_v,
        w_down=jnp.pad(p["w_down"].astype(BF16), ((0, pad_c), (0, 0))),
        g_ffn_post=p["g_ffn_post"].reshape(1, D_MODEL),
        w_ple=p["w_ple"].astype(BF16),
        w_ple_gate=p["w_ple_gate"].astype(BF16),
        g_ple=p["g_ple"].reshape(1, D_MODEL),
    ), halves


def kernel(x_prompt, x_sample, p_prompt, p_sample, cache_k, cache_v, page_table, state_shift, state_wkv,
           state_conv, g_mix_pre, w_in, lam_q1, lam_k1, lam_q2, lam_k2, g_subln, mu_shift, w0, w2, a0, a2,
           g2, k_k, k_a, r_k, ln_x_w, ln_x_b, w_o, g_mix_post, g_ffn_pre, w_up, conv_w, conv_b, w_down,
           g_ffn_post, w_ple, w_ple_gate, g_ple):
    depth = w_in.shape[0]
    bp, sp, _ = x_prompt.shape
    bd, td, _ = x_sample.shape
    n_pool, page = cache_k.shape[1], cache_k.shape[2]
    xp = x_prompt.reshape(bp * sp, D_MODEL)
    xs = x_sample.reshape(bd * td, D_MODEL)
    t_pad = SUBLANES
    outs = {k: [] for k in ("kp", "vp", "ks", "vs", "shp", "wkp", "cvp", "shs", "wks", "cvs")}
    for l in range(depth):
        lam_init = 0.8 - 0.6 * math.exp(-0.3 * l)
        p = dict(mu_shift=mu_shift[l], w0=w0[l], w2=w2[l], a0=a0[l], a2=a2[l], g2=g2[l], k_k=k_k[l],
                 k_a=k_a[l], r_k=r_k[l], ln_x_w=ln_x_w[l], ln_x_b=ln_x_b[l], w_o=w_o[l],
                 g_mix_post=g_mix_post[l], g_ffn_pre=g_ffn_pre[l], w_up=w_up[l], conv_w=conv_w[l],
                 conv_b=conv_b[l], w_down=w_down[l], g_ffn_post=g_ffn_post[l], w_ple=w_ple[l],
                 w_ple_gate=w_ple_gate[l], g_ple=g_ple[l])
        w_in_bf = w_in[l].astype(BF16)
        g_pre = g_mix_pre[l].reshape(1, D_MODEL)
        lams = [z[l].reshape(1, A_DH) for z in (lam_q1, lam_k1, lam_q2, lam_k2)]
        gsub = g_subln[l].reshape(1, A_DV)
        pa = _rwkv_param_arrays(p)
        wts, halves = _post_weights(p)

        q, k, v, kb, vb, pr = _inproj(xp, g_pre, w_in_bf)
        oa = _attn_prompt(lams, gsub, q, kb, vb, bp, sp, lam_init)
        orw, wkv_p = _rwkv_prompt(pr, jnp.zeros((bp, N_SHIFT), F32),
                                  jnp.zeros((bp, R_HEADS, R_HEAD, R_HEAD), F32), pa, bp, sp)
        zc = jnp.zeros((bp, CONV_W - 1, D_FF_PAD), F32)
        xp, cng, cnv = _post(xp, oa, orw, p_prompt[l].reshape(bp * sp, D_PLE), wts, zc, zc, bp, 1)
        outs["kp"].append(k.reshape(bp, sp, A_HEADS, A_DK))
        outs["vp"].append(v.reshape(bp, sp, A_HEADS, A_DV))
        outs["shp"].append(pr.reshape(bp, sp, N_SHIFT)[:, sp - 1])
        outs["wkp"].append(wkv_p)
        outs["cvp"].append(jnp.concatenate([cng[..., :D_FF], cnv[..., :D_FF]], axis=-1))

        q, k, v, kb, vb, pr = _inproj(xs, g_pre, w_in_bf)
        q_ht = jnp.swapaxes(q.reshape(bd, td, A_HEADS, A_DK), 1, 2).reshape(bd * A_HEADS * td, A_DK)
        oa = _attn_sample(lams, gsub, q_ht, k.reshape(bd * td * A_HEADS, A_DK), v.reshape(bd * td * A_HEADS, A_DV),
                          cache_k.reshape(depth, n_pool, page * A_HEADS, A_DK),
                          cache_v.reshape(depth, n_pool, page * A_HEADS, A_DV), l, page_table, td, lam_init)
        oa = jnp.swapaxes(oa.reshape(bd, A_HEADS, td, A_DV), 1, 2).reshape(bd, td, D_ATT)
        pr3 = pr.reshape(bd, td, N_SHIFT)
        prev3 = jnp.concatenate([state_shift[l][:, None, :], pr3[:, :td - 1]], axis=1)
        pad_t = ((0, 0), (0, t_pad - td), (0, 0))
        orw, wkv_s = _rwkv_sample(jnp.pad(pr3, pad_t).reshape(bd * t_pad, N_SHIFT),
                                  jnp.pad(prev3, pad_t).reshape(bd * t_pad, N_SHIFT),
                                  state_wkv[l], pa, bd, t_pad, td)
        orw = orw.reshape(bd, t_pad, D_RWKV)[:, :td]
        tmaj = lambda z: jnp.swapaxes(z, 0, 1).reshape(td * bd, z.shape[-1])
        c0g, c0v = halves(jnp.swapaxes(state_conv[l], 0, 1).reshape(1, (CONV_W - 1) * bd, 2 * D_FF))
        ys, cng, cnv = _post(tmaj(xs.reshape(bd, td, D_MODEL)), tmaj(oa), tmaj(orw), tmaj(p_sample[l]),
                             wts, c0g, c0v, 1, bd)
        xs = jnp.swapaxes(ys.reshape(td, bd, D_MODEL), 0, 1).reshape(bd * td, D_MODEL)
        cvs = jnp.concatenate([cng[..., :D_FF], cnv[..., :D_FF]], axis=-1).reshape(CONV_W - 1, bd, 2 * D_FF)
        outs["ks"].append(k.reshape(bd, td, A_HEADS, A_DK))
        outs["vs"].append(v.reshape(bd, td, A_HEADS, A_DV))
        outs["shs"].append(pr3[:, td - 1])
        outs["wks"].append(wkv_s)
        outs["cvs"].append(jnp.swapaxes(cvs, 0, 1))
    st = lambda key: jnp.stack(outs[key])
    return (xp.reshape(bp, sp, D_MODEL), xs.reshape(bd, td, D_MODEL), st("kp"), st("vp"), st("ks"), st("vs"),
            st("shp"), st("wkp"), st("cvp"), st("shs"), st("wks"), st("cvs"))
```

```python
import functools
import math

import jax
import jax.numpy as jnp
from jax import lax
from jax.experimental import pallas as pl
from jax.experimental.pallas import tpu as pltpu

F32 = jnp.float32
BF16 = jnp.bfloat16

D_MODEL = 1024
A_HEADS = 4
A_DH = 64
A_DK = 2 * A_DH
A_DV = 2 * A_DH
D_ATT = A_HEADS * A_DK
R_HEAD = 64
R_HEADS = 8
D_RWKV = R_HEADS * R_HEAD
LORA_W = 64
LORA_A = 64
LORA_G = 128
N_SHIFT = 3 * D_RWKV + LORA_W + LORA_A + LORA_G
N_IN = 3 * D_ATT + N_SHIFT
D_FF = 2752
CONV_W = 3
D_PLE = 256
NORM_EPS = 1e-6
GN_EPS = 64e-5
ATT_SCALE = A_DH ** -0.5
LOG2E = math.log2(math.e)

LANES = 128
SUBLANES = 8
VMEM_LIMIT_BYTES = 56 * 1024 * 1024

D_FF_PAD = ((D_FF + LANES - 1) // LANES) * LANES
PAIR = 2 * R_HEAD
RWKV_COLS = 4 * PAIR
N_COL_BLOCKS = D_RWKV // RWKV_COLS
X_COLS = LORA_W + LORA_A + LORA_G
CHUNK = 64
WKV_HEAD_STAGES = 2
ATTN_HEADS_PER_LOOP = 4


def _params(n_axes):
    return pltpu.CompilerParams(
        dimension_semantics=("arbitrary",) * n_axes,
        vmem_limit_bytes=VMEM_LIMIT_BYTES,
    )


def _const_spec(shape):
    zeros = (0,) * len(shape)
    return pl.BlockSpec(shape, lambda *_: zeros, pipeline_mode=pl.Buffered(1))


def _rms(x, g):
    return x * lax.rsqrt(jnp.mean(x * x, axis=-1, keepdims=True) + NORM_EPS) * g


def _dot(a, b):
    return jnp.dot(a, b, preferred_element_type=F32)


def _dot_nt(a, b):
    return lax.dot_general(a, b, (((1,), (1,)), ((), ())), preferred_element_type=F32)


def _split3(x):
    hi = x.astype(BF16)
    r1 = x - hi.astype(F32)
    mid = r1.astype(BF16)
    lo = (r1 - mid.astype(F32)).astype(BF16)
    return hi, mid, lo


def _dot_sel(x, sel, passes=3):
    out = None
    for part in _split3(x)[:passes]:
        t = _dot(part, sel)
        out = t if out is None else out + t
    return out


def _sel_dot(sel, x, passes=3):
    out = None
    for part in _split3(x)[:passes]:
        t = _dot(sel, part)
        out = t if out is None else out + t
    return out


def _sel_dot_nt(sel, x, passes=3):
    out = None
    for part in _split3(x)[:passes]:
        t = _dot_nt(sel, part)
        out = t if out is None else out + t
    return out


def _iota(shape, dim):
    return lax.broadcasted_iota(jnp.int32, shape, dim)


def _inproj_body(x_ref, g_ref, w_ref, q_ref, k_ref, v_ref, kb_ref, vb_ref, pr_ref):
    h = _rms(x_ref[...], g_ref[...]).astype(BF16)
    q = _dot(h, w_ref[:, 0:D_ATT])
    q_ref[...] = (q * (ATT_SCALE * LOG2E)).astype(BF16)
    k = _dot(h, w_ref[:, D_ATT:2 * D_ATT])
    kb_ref[...] = k.astype(BF16)
    v = _dot(h, w_ref[:, 2 * D_ATT:3 * D_ATT])
    vb_ref[...] = v.astype(BF16)
    for hd in range(A_HEADS):
        k_ref[:, hd, :] = k[:, hd * A_DK:(hd + 1) * A_DK]
        v_ref[:, hd, :] = v[:, hd * A_DV:(hd + 1) * A_DV]
    pr_ref[...] = _dot(h, w_ref[:, 3 * D_ATT:N_IN])


def _inproj(x, g, w_in_bf):
    n = x.shape[0]
    tm = min(512, n)
    row = lambda i: (i, 0)
    outs = (
        jax.ShapeDtypeStruct((n, D_ATT), BF16),
        jax.ShapeDtypeStruct((n, A_HEADS, A_DK), F32),
        jax.ShapeDtypeStruct((n, A_HEADS, A_DV), F32),
        jax.ShapeDtypeStruct((n, D_ATT), BF16),
        jax.ShapeDtypeStruct((n, D_ATT), BF16),
        jax.ShapeDtypeStruct((n, N_SHIFT), F32),
    )
    return pl.pallas_call(
        _inproj_body,
        grid=(n // tm,),
        in_specs=[
            pl.BlockSpec((tm, D_MODEL), row),
            _const_spec((1, D_MODEL)),
            _const_spec((D_MODEL, N_IN)),
        ],
        out_specs=tuple(pl.BlockSpec((tm,) + s.shape[1:], lambda i, nd=len(s.shape): (i,) + (0,) * (nd - 1))
                        for s in outs),
        out_shape=outs,
        compiler_params=_params(1),
        name="inproj",
    )(x, g, w_in_bf)


def _diff_lambda(lq1, lk1, lq2, lk2, lam_init):
    s1 = jnp.sum(lq1 * lk1, axis=-1, keepdims=True)
    s2 = jnp.sum(lq2 * lk2, axis=-1, keepdims=True)
    return jnp.exp(s1) - jnp.exp(s2) + lam_init


def _stack_maps(q):
    lane = _iota(q.shape, 1)
    zero = jnp.zeros_like(q)
    return jnp.concatenate([jnp.where(lane < A_DH, q, zero), jnp.where(lane >= A_DH, q, zero)], axis=0)


def _subln(o, gsub, lam_init):
    return _rms(o, gsub) * (1.0 - lam_init)


def _attn_prompt_body(lq1, lk1, lq2, lk2, gsub_ref, q_ref, k_ref, v_ref, o_ref, *, tq, lam_init):
    qi = pl.program_id(1)
    lam = _diff_lambda(lq1[...], lk1[...], lq2[...], lk2[...], lam_init)
    heads = [slice(h * A_DK, (h + 1) * A_DK) for h in range(A_HEADS)]
    qs = [_stack_maps(q_ref[:, cols]) for cols in heads]
    causal = _iota((2 * tq, tq), 1) <= _iota((2 * tq, tq), 0) % tq

    def tile(j, carry, group, diagonal):
        off = pl.multiple_of(j * tq, tq)
        out = []
        for h, (m, l, acc) in zip(group, carry):
            s = _dot_nt(qs[h], k_ref[pl.ds(off, tq), heads[h]])
            if diagonal:
                s = jnp.where(causal, s, -jnp.inf)
            m_new = jnp.maximum(m, jnp.max(s, axis=-1, keepdims=True))
            p = jnp.exp2(s - m_new)
            alpha = jnp.exp2(m - m_new)
            l = alpha * l + jnp.sum(p, axis=-1, keepdims=True)
            acc = alpha * acc + _dot(p.astype(BF16), v_ref[pl.ds(off, tq), heads[h]])
            out.append((m_new, l, acc))
        return tuple(out)

    for first in range(0, A_HEADS, ATTN_HEADS_PER_LOOP):
        group = tuple(range(first, first + ATTN_HEADS_PER_LOOP))
        init = tuple((jnp.full((2 * tq, 1), -jnp.inf, F32), jnp.zeros((2 * tq, 1), F32),
                      jnp.zeros((2 * tq, A_DV), F32)) for _ in group)
        carry = lax.fori_loop(0, qi, lambda j, c, group=group: tile(j, c, group, False), init)
        carry = tile(qi, carry, group, True)
        for h, (_, l, acc) in zip(group, carry):
            o = acc / l
            o = o[:tq] - lam * o[tq:]
            o_ref[:, heads[h]] = _subln(o, gsub_ref[...], lam_init).astype(o_ref.dtype)


def _attn_prompt(lams, gsub, q, kb, vb, b, s, lam_init):
    tq = min(512, s)
    nq = s // tq
    lam_spec = _const_spec((1, A_DH))
    return pl.pallas_call(
        functools.partial(_attn_prompt_body, tq=tq, lam_init=lam_init),
        grid=(b, nq),
        in_specs=[
            lam_spec, lam_spec, lam_spec, lam_spec,
            _const_spec((1, A_DV)),
            pl.BlockSpec((tq, D_ATT), lambda bi, qi: (bi * nq + qi, 0)),
            pl.BlockSpec((s, D_ATT), lambda bi, qi: (bi, 0)),
            pl.BlockSpec((s, D_ATT), lambda bi, qi: (bi, 0)),
        ],
        out_specs=pl.BlockSpec((tq, D_ATT), lambda bi, qi: (bi * nq + qi, 0)),
        out_shape=jax.ShapeDtypeStruct((b * s, D_ATT), BF16),
        compiler_params=_params(2),
        name="attn_prompt",
    )(*lams, gsub, q, kb, vb)


def _attn_sample_body(pt_ref, lq1, lk1, lq2, lk2, gsub_ref, q_ref, kn_ref, vn_ref, *rest,
                      n_pages, t, lam_init):
    del pt_ref
    kp_refs = rest[:n_pages]
    vp_refs = rest[n_pages:2 * n_pages]
    o_ref = rest[2 * n_pages]
    lam = _diff_lambda(lq1[...], lk1[...], lq2[...], lk2[...], lam_init)
    nq = A_HEADS * t
    qs = _stack_maps(q_ref[...])
    q_head = (_iota((2 * nq, 1), 0) % nq) // t
    q_tok = _iota((2 * nq, 1), 0) % t
    width = kp_refs[0].shape[0]
    past_ok = (_iota((2 * nq, width), 1) % A_HEADS) == q_head
    new_lane = _iota((2 * nq, nq), 1)
    new_ok = ((new_lane % A_HEADS) == q_head) & ((new_lane // A_HEADS) <= q_tok)
    s_past = [jnp.where(past_ok, _dot_nt(qs, kp[...].astype(BF16)), -jnp.inf) for kp in kp_refs]
    s_new = jnp.where(new_ok, _dot_nt(qs, kn_ref[...].astype(BF16)), -jnp.inf)
    m = jnp.max(s_new, axis=-1, keepdims=True)
    for sp in s_past:
        m = jnp.maximum(m, jnp.max(sp, axis=-1, keepdims=True))
    p_new = jnp.exp2(s_new - m)
    l = jnp.sum(p_new, axis=-1, keepdims=True)
    acc = _dot(p_new.astype(BF16), vn_ref[...].astype(BF16))
    for sp, vp in zip(s_past, vp_refs):
        p = jnp.exp2(sp - m)
        l = l + jnp.sum(p, axis=-1, keepdims=True)
        acc = acc + _dot(p.astype(BF16), vp[...].astype(BF16))
    o = acc / l
    o = o[:nq] - lam * o[nq:]
    o_ref[...] = _subln(o, gsub_ref[...], lam_init).astype(o_ref.dtype)


def _attn_sample(lams, gsub, q, k_new, v_new, cache_k, cache_v, layer, page_table, t, lam_init):
    bd, n_pages = page_table.shape
    nq = A_HEADS * t
    rows = cache_k.shape[2]
    lam_spec = pl.BlockSpec((1, A_DH), lambda b, pt: (0, 0))
    tok_spec = pl.BlockSpec((nq, A_DK), lambda b, pt: (b, 0))

    def page_spec(p):
        return pl.BlockSpec((None, None, rows, A_DK), lambda b, pt, p=p: (layer, pt[b, p], 0, 0))

    grid_spec = pltpu.PrefetchScalarGridSpec(
        num_scalar_prefetch=1,
        grid=(bd,),
        in_specs=[lam_spec] * 4 + [pl.BlockSpec((1, A_DV), lambda b, pt: (0, 0)), tok_spec, tok_spec, tok_spec]
        + [page_spec(p) for p in range(n_pages)] * 2,
        out_specs=tok_spec,
    )
    return pl.pallas_call(
        functools.partial(_attn_sample_body, n_pages=n_pages, t=t, lam_init=lam_init),
        grid_spec=grid_spec,
        out_shape=jax.ShapeDtypeStruct((bd * nq, A_DV), BF16),
        compiler_params=_params(1),
        name="attn_sample",
    )(page_table, *lams, gsub, q, k_new, v_new, *([cache_k] * n_pages), *([cache_v] * n_pages))


def _head_blocks(n):
    return (_iota((n, n), 0) // R_HEAD) == (_iota((n, n), 1) // R_HEAD)


def _seg_sum(x, seg_ones):
    w = seg_ones.shape[0]
    xb = x.astype(BF16)
    return jnp.concatenate([_dot(xb[:, i:i + w], seg_ones) for i in range(0, x.shape[1], w)], axis=1)


def _softplus(z):
    return jnp.maximum(z, 0.0) + jnp.log1p(jnp.exp(-jnp.abs(z)))


def _rwkv_tokens(cur, prev, prm, seg_ones):
    mix = {c: cur[c] + (prev[c] - cur[c]) * prm["mu_" + c] for c in ("r", "k", "v", "x")}
    r, k, v = mix["r"], mix["k"], mix["v"]
    xwa = mix["x"][:, :LORA_W + LORA_A]
    xg = mix["x"][:, LORA_W + LORA_A:]
    w_in = prm["w0"] + _dot(jnp.tanh(xwa).astype(BF16), prm["w2"])
    lw = -jnp.exp(-_softplus(-w_in) - 0.5)
    a = jax.nn.sigmoid(prm["a0"] + _dot(xwa.astype(BF16), prm["a2"]))
    g = _dot(jax.nn.sigmoid(xg).astype(BF16), prm["g2"])
    kk = k * prm["k_k"]
    nrm = jnp.sqrt(_seg_sum(kk * kk, seg_ones))
    kk = kk / jnp.maximum(nrm, 1e-12)
    k2 = k * (1.0 + (a - 1.0) * prm["k_a"])
    bonus = _seg_sum(r * k2 * prm["r_k"], seg_ones) * v
    return r, lw, k2, v, kk, a, g, bonus


def _rwkv_finish(y, bonus, g, prm, seg_ones):
    mean = _seg_sum(y, seg_ones) * (1.0 / R_HEAD)
    d = y - mean
    var = _seg_sum(d * d, seg_ones) * (1.0 / R_HEAD)
    yn = d * lax.rsqrt(var + GN_EPS) * prm["ln_w"] + prm["ln_b"]
    return (yn + bonus) * g


def _stack_heads(x):
    lane = _iota(x.shape, 1)
    zero = jnp.zeros_like(x)
    return jnp.concatenate([jnp.where(lane < R_HEAD, x, zero), jnp.where(lane >= R_HEAD, x, zero)], axis=0)


def _wkv_front_stages(chunks):
    c = chunks[0][0].shape[0]
    n = 2 * c
    tril = (_iota((c, c), 1) <= _iota((c, c), 0)).astype(BF16)
    row = _iota((n, n), 0)
    col = _iota((n, n), 1)
    same = (row // c) == (col // c)
    strict = same & (col < row)
    incl = same & (col <= row)
    zero = jnp.zeros((n, n), F32)
    eye_b = (_iota((PAIR, PAIR), 0) == _iota((PAIR, PAIR), 1)).astype(BF16)
    env = {}

    def cumsum():
        env["big_l"] = [_sel_dot(tril, ch[1], passes=2) for ch in chunks]

    def operands():
        ops = []
        for (r, lw, k2, v, kk, a), big_l in zip(chunks, env["big_l"]):
            l_end = big_l[c - 1:c, :]
            e_in = jnp.exp(big_l)
            e_ex = jnp.exp(big_l - lw)
            e_neg = jnp.exp(-big_l)
            e_rem = jnp.exp(l_end - big_l)
            b = kk * a
            a_t = _stack_heads(-kk * e_ex)
            r_t = _stack_heads(r * e_in)
            ops.append(dict(
                l_end=l_end, a_t=a_t, r_t=r_t, a_tb=a_t.astype(BF16), r_tb=r_t.astype(BF16),
                b_t=_stack_heads(b * e_neg).astype(BF16), k_t=_stack_heads(k2 * e_neg).astype(BF16),
                b_h=_stack_heads(b * e_rem), k_h=_stack_heads(k2 * e_rem),
                v_s=_stack_heads(v).astype(BF16)))
        env["ops"] = ops

    def pair(name, lhs, rhs, mask):
        def run():
            env[name] = [jnp.where(mask, _dot_nt(o[lhs], o[rhs]), zero) for o in env["ops"]]
        return run

    def transposes():
        env["b_ht"] = [o["b_h"].T.astype(BF16) for o in env["ops"]]
        env["k_ht"] = [o["k_h"].T.astype(BF16) for o in env["ops"]]

    stages = [cumsum, operands, pair("m_ab", "a_tb", "b_t", strict), pair("m_ak", "a_tb", "k_t", strict),
              pair("m_rb", "r_tb", "b_t", incl), pair("m_rk", "r_tb", "k_t", incl), transposes]
    return stages, env


_WKV_FRONT_KEYS = ("ops", "m_ab", "m_ak", "m_rb", "m_rk", "b_ht", "k_ht")


def _wkv_back_stages(fronts, c):
    n = 2 * c
    steps = max(1, int(math.ceil(math.log2(c))))
    eye_n = jnp.where(_iota((n, n), 0) == _iota((n, n), 1), 1.0, 0.0)
    eye = (_iota((PAIR, PAIR), 0) == _iota((PAIR, PAIR), 1))
    env = {}

    def rhs():
        env.update({key: [x for f in fronts for x in f[key]] for key in _WKV_FRONT_KEYS})
        env["x"] = [jnp.concatenate([o["a_t"], _dot(mk.astype(BF16), o["v_s"])], axis=1).astype(BF16)
                    for o, mk in zip(env["ops"], env["m_ak"])]
        env["pb"] = [p.astype(BF16) for p in env["m_ab"]]
        env["t"] = [eye_n + p for p in env["m_ab"]]

    def square():
        env["pb"] = [_dot(pb, pb).astype(BF16) for pb in env["pb"]]

    def extend():
        env["t"] = [t + _dot(t.astype(BF16), pb) for t, pb in zip(env["t"], env["pb"])]

    def solve():
        env["w"] = [_dot(t.astype(BF16), x).astype(BF16) for t, x in zip(env["t"], env["x"])]

    def state_side():
        env["mg"] = [_dot(bt, w) for bt, w in zip(env["b_ht"], env["w"])]
        env["kv"] = [_dot(kt, o["v_s"]) for kt, o in zip(env["k_ht"], env["ops"])]

    def output_side():
        env["qy"] = [_dot(mr.astype(BF16), w) for mr, w in zip(env["m_rb"], env["w"])]
        env["rkv"] = [_dot(mr.astype(BF16), o["v_s"]) for mr, o in zip(env["m_rk"], env["ops"])]

    def both_sides():
        zeros = jnp.zeros((n, PAIR), BF16)
        big = []
        for o, bt, kt, mb, mk, w in zip(env["ops"], env["b_ht"], env["k_ht"], env["m_rb"], env["m_rk"], env["w"]):
            lhs = jnp.concatenate([jnp.concatenate([bt, kt], axis=1),
                                   jnp.concatenate([mb, mk], axis=1).astype(BF16)], axis=0)
            rhs_mat = jnp.concatenate([w, jnp.concatenate([zeros, o["v_s"]], axis=1)], axis=0)
            big.append(_dot(lhs, rhs_mat))
        env["mg"] = [x[:PAIR] for x in big]
        env["qy"] = [x[PAIR:] for x in big]
        env["kv"] = env["rkv"] = [0.0] * len(big)

    def finish():
        out = []
        for o, mg, kv, qy, rk in zip(env["ops"], env["mg"], env["kv"], env["qy"], env["rkv"]):
            m_mat = jnp.where(eye, jnp.exp(o["l_end"]), 0.0) + mg[:, :PAIR]
            g_mat = mg[:, PAIR:] + kv
            q_mat = o["r_t"] + qy[:, :PAIR]
            y0 = qy[:, PAIR:] + rk
            out.append((q_mat.astype(BF16), y0, m_mat.astype(BF16), g_mat))
        env["out"] = out

    stages = [rhs]
    for _ in range(steps - 1):
        stages += [square, extend]
    stages += [solve] + ([both_sides] if n % LANES == 0 else [state_side, output_side]) + [finish]
    return stages, env


def _wkv_prepare(chunks):
    front, env = _wkv_front_stages(chunks)
    for stage in front:
        stage()
    back, env = _wkv_back_stages([env], chunks[0][0].shape[0])
    for stage in back:
        stage()
    return env["out"]


def _wkv_apply(prep, st):
    q_mat, y0, m_mat, g_mat = prep
    c = y0.shape[0] // 2
    stb = st.astype(BF16)
    y_st = _dot(q_mat, stb) + y0
    return y_st[:c] + y_st[c:], _dot(m_mat, stb) + g_mat


def _state_in(s0):
    x = s0.reshape(PAIR, R_HEAD)
    sel = (_iota((PAIR, R_HEAD), 0) % R_HEAD == _iota((PAIR, R_HEAD), 1)).astype(BF16)
    full = _sel_dot_nt(sel, x)
    return jnp.where(_head_blocks(PAIR), full, 0.0)


def _state_out(st):
    folded = st[:R_HEAD] + st[R_HEAD:]
    eye = (_iota((PAIR, PAIR), 0) == _iota((PAIR, PAIR), 1)).astype(BF16)
    return _sel_dot_nt(eye, folded).reshape(2, R_HEAD, R_HEAD)


_RWKV_PARAM_NAMES = ("mu_r", "mu_k", "mu_v", "mu_x", "w0", "w2", "a0", "a2", "g2", "k_k", "k_a", "r_k",
                     "ln_w", "ln_b")


def _split_pairs(z, rows):
    return [z[rows, p * PAIR:(p + 1) * PAIR] for p in range(z.shape[1] // PAIR)]


def _run_chains(preps, states, between=None):
    states = list(states)
    ys = [[] for _ in preps]
    for i in range(len(preps[0])):
        for p, chain in enumerate(preps):
            y, states[p] = _wkv_apply(chain[i], states[p])
            ys[p].append(y)
        if between is not None:
            between()
    return jnp.concatenate([jnp.concatenate(col, axis=0) for col in ys], axis=1), states


def _rwkv_prompt_body(*refs, n_chunks):
    n_prm = len(_RWKV_PARAM_NAMES)
    pr = dict(zip(("r", "k", "v", "x"), refs[0:4]))
    sh = dict(zip(("r", "k", "v", "x"), refs[4:8]))
    prm = {nm: ref[...] for nm, ref in zip(_RWKV_PARAM_NAMES, refs[8:8 + n_prm])}
    s0_ref = refs[8 + n_prm]
    o_ref, s_out_ref = refs[9 + n_prm], refs[10 + n_prm]
    scratch = refs[11 + n_prm:]
    carry = dict(zip(("r", "k", "v", "x"), scratch[0:4]))
    st_ref, q_sc, y0_sc, m_sc, g_sc, bonus_sc, gate_sc = scratch[4:11]
    g = pl.program_id(2)
    tc, width = bonus_sc.shape
    n_pairs = width // PAIR
    cl = tc // n_chunks
    seg_ones = _head_blocks(min(width, 2 * LANES)).astype(BF16)

    @pl.when(g == 0)
    def _():
        for c in carry:
            carry[c][...] = jnp.broadcast_to(sh[c][...], carry[c].shape)
        for p in range(n_pairs):
            st_ref[p] = _state_in(s0_ref[2 * p:2 * p + 2])
        for ref in (q_sc, y0_sc, m_sc, g_sc, bonus_sc, gate_sc):
            ref[...] = jnp.zeros(ref.shape, ref.dtype)

    def stored():
        return [[(q_sc[p * n_chunks + i], y0_sc[p * n_chunks + i], m_sc[p * n_chunks + i],
                  g_sc[p * n_chunks + i]) for i in range(n_chunks)] for p in range(n_pairs)]

    def emit(y, bonus, gate, tile):
        rows = pl.ds(pl.multiple_of(tile * tc, tc), tc)
        o_ref[rows, :] = _rwkv_finish(y, bonus, gate, prm, seg_ones).astype(o_ref.dtype)

    old = stored()
    old_bonus, old_gate = bonus_sc[...], gate_sc[...]
    st_in = [st_ref[p] for p in range(n_pairs)]

    cur = {c: pr[c][...] for c in pr}
    prev = {}
    for c in cur:
        first = _iota(cur[c].shape, 0) == 0
        prev[c] = jnp.where(first, carry[c][0:1, :], pltpu.roll(cur[c], 1, 0))
    for c in cur:
        carry[c][...] = jnp.broadcast_to(cur[c][tc - 1:tc, :], carry[c].shape)
    r, lw, k2, v, kk, a, gate, bonus = _rwkv_tokens(cur, prev, prm, seg_ones)
    per_chunk = [list(zip(*(_split_pairs(z, slice(i * cl, (i + 1) * cl)) for z in (r, lw, k2, v, kk, a))))
                 for i in range(n_chunks)]
    fronts = [_wkv_front_stages([per_chunk[i][p] for i in range(n_chunks)]) for p in range(n_pairs)]
    todo = []
    for p, (stages, _) in enumerate(fronts):
        todo += stages[:WKV_HEAD_STAGES]
        if p > 0:
            todo += fronts[p - 1][0][WKV_HEAD_STAGES:]
    todo += fronts[-1][0][WKV_HEAD_STAGES:]
    back_stages, back_env = _wkv_back_stages([env for _, env in fronts], cl)
    todo += back_stages
    per = max(1, len(todo) // (n_chunks + 1))

    def between():
        for stage in todo[:per]:
            stage()
        del todo[:per]

    between()
    y_old, st = _run_chains(old, st_in, between)
    for stage in todo:
        stage()
    emit(y_old, old_bonus, old_gate, jnp.maximum(g - 1, 0))
    for p in range(n_pairs):
        st_ref[p] = jnp.where(g == 0, st_in[p], st[p])
    prepared = back_env["out"]
    for i, (q_mat, y0, m_mat, g_mat) in enumerate(prepared):
        q_sc[i], y0_sc[i], m_sc[i], g_sc[i] = q_mat, y0, m_mat, g_mat
    bonus_sc[...] = bonus
    gate_sc[...] = gate

    @pl.when(g == pl.num_programs(2) - 1)
    def _():
        y_new, st_end = _run_chains(stored(), [st_ref[p] for p in range(n_pairs)])
        emit(y_new, bonus_sc[...], gate_sc[...], g)
        for p in range(n_pairs):
            s_out_ref[2 * p:2 * p + 2] = _state_out(st_end[p])


def _rwkv_sample_body(*refs, n_seq, t_pad, t_valid):
    n_prm = len(_RWKV_PARAM_NAMES)
    pr = dict(zip(("r", "k", "v", "x"), refs[0:4]))
    pv = dict(zip(("r", "k", "v", "x"), refs[4:8]))
    prm = {nm: ref[...] for nm, ref in zip(_RWKV_PARAM_NAMES, refs[8:8 + n_prm])}
    s0_ref = refs[8 + n_prm]
    o_ref, s_out_ref = refs[9 + n_prm], refs[10 + n_prm]
    cur = {c: pr[c][...] for c in pr}
    prev = {c: pv[c][...] for c in pv}
    width = o_ref.shape[1]
    n_pairs = width // PAIR
    seg_ones = _head_blocks(min(width, 2 * LANES)).astype(BF16)
    r, lw, k2, v, kk, a, gate, bonus = _rwkv_tokens(cur, prev, prm, seg_ones)
    valid = (_iota(r.shape, 0) % t_pad) < t_valid
    zero = jnp.zeros_like(r)
    r, lw, k2, v, kk = (jnp.where(valid, z, zero) for z in (r, lw, k2, v, kk))
    per_seq = [list(zip(*(_split_pairs(z, slice(i * t_pad, (i + 1) * t_pad)) for z in (r, lw, k2, v, kk, a))))
               for i in range(n_seq)]
    chunks = [per_seq[i][p] for i in range(n_seq) for p in range(n_pairs)]
    states = [_state_in(s0_ref[i, 2 * p:2 * p + 2]) for i in range(n_seq) for p in range(n_pairs)]
    applied = [_wkv_apply(prep, st) for prep, st in zip(_wkv_prepare(chunks), states)]
    rows = []
    for i in range(n_seq):
        for p in range(n_pairs):
            s_out_ref[i, 2 * p:2 * p + 2] = _state_out(applied[i * n_pairs + p][1])
        rows.append(jnp.concatenate([applied[i * n_pairs + p][0] for p in range(n_pairs)], axis=1))
    y = jnp.concatenate(rows, axis=0)
    o_ref[...] = _rwkv_finish(y, bonus, gate, prm, seg_ones).astype(o_ref.dtype)


def _rwkv_param_arrays(p):
    z_w = jnp.zeros((LORA_A, D_RWKV), BF16)
    z_a = jnp.zeros((LORA_W, D_RWKV), BF16)
    return dict(
        mu=p["mu_shift"].reshape(1, N_SHIFT),
        w0=p["w0"].reshape(1, D_RWKV),
        w2=jnp.concatenate([p["w2"].astype(BF16), z_w], axis=0),
        a0=p["a0"].reshape(1, D_RWKV),
        a2=jnp.concatenate([z_a, p["a2"].astype(BF16)], axis=0),
        g2=p["g2"].astype(BF16),
        k_k=p["k_k"].reshape(1, D_RWKV),
        k_a=p["k_a"].reshape(1, D_RWKV),
        r_k=p["r_k"].reshape(1, D_RWKV),
        ln_w=p["ln_x_w"].reshape(1, D_RWKV),
        ln_b=p["ln_x_b"].reshape(1, D_RWKV),
    )


def _rwkv_param_specs(block_of):
    x_blk = (3 * D_RWKV) // X_COLS

    def vec(off):
        return pl.BlockSpec((1, RWKV_COLS), lambda *i: (0, off + block_of(*i)))

    def mat(rows):
        return pl.BlockSpec((rows, RWKV_COLS), lambda *i: (0, block_of(*i)))

    return [
        vec(0), vec(N_COL_BLOCKS), vec(2 * N_COL_BLOCKS),
        pl.BlockSpec((1, X_COLS), lambda *i: (0, x_blk)),
        vec(0), mat(LORA_W + LORA_A), vec(0), mat(LORA_W + LORA_A), mat(LORA_G),
        vec(0), vec(0), vec(0), vec(0), vec(0),
    ]


def _rwkv_param_operands(pa):
    return [pa["mu"], pa["mu"], pa["mu"], pa["mu"], pa["w0"], pa["w2"], pa["a0"], pa["a2"], pa["g2"],
            pa["k_k"], pa["k_a"], pa["r_k"], pa["ln_w"], pa["ln_b"]]


def _col_specs(rows, row_of, block_of):
    x_blk = (3 * D_RWKV) // X_COLS
    specs = [pl.BlockSpec((rows, RWKV_COLS), lambda *i, o=o: (row_of(*i), o * N_COL_BLOCKS + block_of(*i)))
             for o in range(3)]
    specs.append(pl.BlockSpec((rows, X_COLS), lambda *i: (row_of(*i), x_blk)))
    return specs


def _rwkv_prompt(pr, shift0, s0, pa, b, s):
    tc = min(512, s)
    nt = s // tc
    n_chunks = max(1, tc // CHUNK)
    cl = tc // n_chunks
    pairs = RWKV_COLS // PAIR
    block_of = lambda bi, ci, gi: ci
    row_of = lambda bi, ci, gi: bi * nt + gi
    x_blk = (3 * D_RWKV) // X_COLS
    shift_specs = [pl.BlockSpec((None, 1, RWKV_COLS), lambda bi, ci, gi, o=o: (bi, 0, o * N_COL_BLOCKS + ci))
                   for o in range(3)]
    shift_specs.append(pl.BlockSpec((None, 1, X_COLS), lambda bi, ci, gi: (bi, 0, x_blk)))
    state_spec = pl.BlockSpec((None, 2 * pairs, R_HEAD, R_HEAD), lambda bi, ci, gi: (bi, ci, 0, 0))
    shift3 = shift0.reshape(b, 1, N_SHIFT)
    return pl.pallas_call(
        functools.partial(_rwkv_prompt_body, n_chunks=n_chunks),
        grid=(b, N_COL_BLOCKS, nt),
        in_specs=_col_specs(tc, row_of, block_of) + shift_specs + _rwkv_param_specs(block_of) + [state_spec],
        out_specs=(pl.BlockSpec((s, RWKV_COLS), lambda bi, ci, gi: (bi, ci)), state_spec),
        out_shape=(jax.ShapeDtypeStruct((b * s, D_RWKV), BF16),
                   jax.ShapeDtypeStruct((b, R_HEADS, R_HEAD, R_HEAD), F32)),
        scratch_shapes=[pltpu.VMEM((SUBLANES, RWKV_COLS), F32)] * 3 + [
            pltpu.VMEM((SUBLANES, X_COLS), F32),
            pltpu.VMEM((pairs, PAIR, PAIR), F32),
            pltpu.VMEM((pairs * n_chunks, 2 * cl, PAIR), BF16),
            pltpu.VMEM((pairs * n_chunks, 2 * cl, PAIR), F32),
            pltpu.VMEM((pairs * n_chunks, PAIR, PAIR), BF16),
            pltpu.VMEM((pairs * n_chunks, PAIR, PAIR), F32),
            pltpu.VMEM((tc, RWKV_COLS), F32),
            pltpu.VMEM((tc, RWKV_COLS), F32),
        ],
        compiler_params=_params(3),
        name="rwkv_prompt",
    )(pr, pr, pr, pr, shift3, shift3, shift3, shift3, *_rwkv_param_operands(pa), s0)


def _rwkv_sample(pr_pad, prev_pad, s0, pa, bd, t_pad, t_valid):
    n_seq = min(8, bd)
    rows = n_seq * t_pad
    pairs = RWKV_COLS // PAIR
    block_of = lambda gi, ci: ci
    row_of = lambda gi, ci: gi
    state_spec = pl.BlockSpec((n_seq, 2 * pairs, R_HEAD, R_HEAD), lambda gi, ci: (gi, ci, 0, 0))
    return pl.pallas_call(
        functools.partial(_rwkv_sample_body, n_seq=n_seq, t_pad=t_pad, t_valid=t_valid),
        grid=(bd // n_seq, N_COL_BLOCKS),
        in_specs=_col_specs(rows, row_of, block_of) * 2 + _rwkv_param_specs(block_of) + [state_spec],
        out_specs=(pl.BlockSpec((rows, RWKV_COLS), lambda gi, ci: (gi, ci)), state_spec),
        out_shape=(jax.ShapeDtypeStruct((bd * t_pad, D_RWKV), BF16),
                   jax.ShapeDtypeStruct((bd, R_HEADS, R_HEAD, R_HEAD), F32)),
        compiler_params=_params(2),
        name="rwkv_sample",
    )(pr_pad, pr_pad, pr_pad, pr_pad, prev_pad, prev_pad, prev_pad, prev_pad, *_rwkv_param_operands(pa), s0)


def _shifted(u, carry_ref, shift, j):
    n_state = (CONV_W - 1) * shift
    if shift == 1:
        out = pltpu.roll(u, j, 0)
        row = _iota(u.shape, 0)
        for i in range(j):
            out = jnp.where(row == i, carry_ref[n_state - j + i:n_state - j + i + 1, :], out)
        return out
    keep = u.shape[0] - j * shift
    return jnp.concatenate([carry_ref[n_state - j * shift:n_state, :], u[:keep]], axis=0)


def _post_body(x_ref, oa_ref, or_ref, pe_ref, wo_ref, gmp_ref, gfp_ref, wg_ref, wv_ref, cwg_ref, cwv_ref,
               cbg_ref, cbv_ref, wd_ref, gfo_ref, wple_ref, wgate_ref, gple_ref, c0g_ref, c0v_ref,
               y_ref, cng_ref, cnv_ref, cg_ref, cv_ref, *, shift, n_fchunks):
    ti = pl.program_id(1)
    n_state = (CONV_W - 1) * shift
    tm = x_ref.shape[0]

    @pl.when(ti == 0)
    def _():
        cg_ref[0:n_state, :] = c0g_ref[...]
        cv_ref[0:n_state, :] = c0v_ref[...]

    mix = _dot(jnp.concatenate([oa_ref[...], or_ref[...]], axis=1), wo_ref[...])
    x1 = x_ref[...] + _rms(mix, gmp_ref[...])
    h = _rms(x1, gfp_ref[...]).astype(BF16)
    tiles = D_FF_PAD // (2 * LANES)
    bounds = [((c * tiles) // n_fchunks) * 2 * LANES for c in range(n_fchunks)] + [D_FF_PAD]
    acc = jnp.zeros((tm, D_MODEL), F32)
    for c in range(n_fchunks):
        cols = slice(bounds[c], bounds[c + 1])
        halves = []
        for w_ref, cw_ref, cb_ref, carry_ref in ((wg_ref, cwg_ref, cbg_ref, cg_ref),
                                                 (wv_ref, cwv_ref, cbv_ref, cv_ref)):
            u = _dot(h, w_ref[:, cols])
            carry = carry_ref.at[:, cols]
            conv = cb_ref[:, cols] + cw_ref[0:1, cols] * _shifted(u, carry, shift, 2)
            conv = conv + cw_ref[1:2, cols] * _shifted(u, carry, shift, 1)
            conv = conv + cw_ref[2:3, cols] * u
            carry_ref[0:n_state, cols] = u[tm - n_state:, :]
            halves.append(conv)
        act = jax.nn.gelu(halves[0], approximate=True) * halves[1]
        acc = acc + _dot(act.astype(BF16), wd_ref[cols, :])
    x2 = x1 + _rms(acc, gfo_ref[...])
    gate = jax.nn.sigmoid(_dot(x2.astype(BF16), wgate_ref[...]))
    ple = _dot(pe_ref[...].astype(BF16), wple_ref[...]) * gate
    y_ref[...] = x2 + _rms(ple, gple_ref[...])

    @pl.when(ti == pl.num_programs(1) - 1)
    def _():
        cng_ref[...] = cg_ref[0:n_state, :]
        cnv_ref[...] = cv_ref[0:n_state, :]


def _post(x, oa, orw, pe, wts, conv0_g, conv0_v, n_seq, shift):
    n = x.shape[0]
    rows_per_seq = n // n_seq
    tm = min(512, rows_per_seq) if shift == 1 else rows_per_seq
    nt = rows_per_seq // tm
    n_state = (CONV_W - 1) * shift
    n_carry = max(SUBLANES, n_state)
    row = lambda si, ti: (si * nt + ti, 0)
    state_spec = pl.BlockSpec((None, n_state, D_FF_PAD), lambda si, ti: (si, 0, 0))
    w_arrays = [wts[k] for k in ("w_o", "g_mix_post", "g_ffn_pre", "w_gate_up", "w_val_up", "cw_g", "cw_v",
                                 "cb_g", "cb_v", "w_down", "g_ffn_post", "w_ple", "w_ple_gate", "g_ple")]
    return pl.pallas_call(
        functools.partial(_post_body, shift=shift, n_fchunks=1),
        grid=(n_seq, nt),
        in_specs=[
            pl.BlockSpec((tm, D_MODEL), row),
            pl.BlockSpec((tm, D_ATT), row),
            pl.BlockSpec((tm, D_RWKV), row),
            pl.BlockSpec((tm, D_PLE), row),
        ] + [_const_spec(w.shape) for w in w_arrays] + [state_spec, state_spec],
        out_specs=(pl.BlockSpec((tm, D_MODEL), row), state_spec, state_spec),
        out_shape=(jax.ShapeDtypeStruct((n, D_MODEL), F32),
                   jax.ShapeDtypeStruct((n_seq, n_state, D_FF_PAD), F32),
                   jax.ShapeDtypeStruct((n_seq, n_state, D_FF_PAD), F32)),
        scratch_shapes=[pltpu.VMEM((n_carry, D_FF_PAD), F32)] * 2,
        compiler_params=_params(2),
        name="post",
    )(x, oa, orw, pe, *w_arrays, conv0_g, conv0_v)


def _post_weights(p):
    pad_c = D_FF_PAD - D_FF

    def halves(a):
        widths = [(0, 0)] * (a.ndim - 1) + [(0, pad_c)]
        return jnp.pad(a[..., :D_FF], widths), jnp.pad(a[..., D_FF:], widths)

    w_g, w_v = halves(p["w_up"].astype(BF16))
    cw_g, cw_v = halves(p["conv_w"])
    cb_g, cb_v = halves(p["conv_b"].reshape(1, 2 * D_FF))
    return dict(
        w_o=p["w_o"].astype(BF16),
        g_mix_post=p["g_mix_post"].reshape(1, D_MODEL),
        g_ffn_pre=p["g_ffn_pre"].reshape(1, D_MODEL),
        w_gate_up=w_g, w_val_up=w_v, cw_g=cw_g, cw_v=cw_v, cb_g=cb_g, cb_v=cb_v,
        w_down=jnp.pad(p["w_down"].astype(BF16), ((0, pad_c), (0, 0))),
        g_ffn_post=p["g_ffn_post"].reshape(1, D_MODEL),
        w_ple=p["w_ple"].astype(BF16),
        w_ple_gate=p["w_ple_gate"].astype(BF16),
        g_ple=p["g_ple"].reshape(1, D_MODEL),
    ), halves


def kernel(x_prompt, x_sample, p_prompt, p_sample, cache_k, cache_v, page_table, state_shift, state_wkv,
           state_conv, g_mix_pre, w_in, lam_q1, lam_k1, lam_q2, lam_k2, g_subln, mu_shift, w0, w2, a0, a2,
           g2, k_k, k_a, r_k, ln_x_w, ln_x_b, w_o, g_mix_post, g_ffn_pre, w_up, conv_w, conv_b, w_down,
           g_ffn_post, w_ple, w_ple_gate, g_ple):
    depth = w_in.shape[0]
    bp, sp, _ = x_prompt.shape
    bd, td, _ = x_sample.shape
    n_pool, page = cache_k.shape[1], cache_k.shape[2]
    xp = x_prompt.reshape(bp * sp, D_MODEL)
    xs = x_sample.reshape(bd * td, D_MODEL)
    t_pad = SUBLANES
    outs = {k: [] for k in ("kp", "vp", "ks", "vs", "shp", "wkp", "cvp", "shs", "wks", "cvs")}
    for l in range(depth):
        lam_init = 0.8 - 0.6 * math.exp(-0.3 * l)
        p = dict(mu_shift=mu_shift[l], w0=w0[l], w2=w2[l], a0=a0[l], a2=a2[l], g2=g2[l], k_k=k_k[l],
                 k_a=k_a[l], r_k=r_k[l], ln_x_w=ln_x_w[l], ln_x_b=ln_x_b[l], w_o=w_o[l],
                 g_mix_post=g_mix_post[l], g_ffn_pre=g_ffn_pre[l], w_up=w_up[l], conv_w=conv_w[l],
                 conv_b=conv_b[l], w_down=w_down[l], g_ffn_post=g_ffn_post[l], w_ple=w_ple[l],
                 w_ple_gate=w_ple_gate[l], g_ple=g_ple[l])
        w_in_bf = w_in[l].astype(BF16)
        g_pre = g_mix_pre[l].reshape(1, D_MODEL)
        lams = [z[l].reshape(1, A_DH) for z in (lam_q1, lam_k1, lam_q2, lam_k2)]
        gsub = g_subln[l].reshape(1, A_DV)
        pa = _rwkv_param_arrays(p)
        wts, halves = _post_weights(p)

        q, k, v, kb, vb, pr = _inproj(xp, g_pre, w_in_bf)
        oa = _attn_prompt(lams, gsub, q, kb, vb, bp, sp, lam_init)
        orw, wkv_p = _rwkv_prompt(pr, jnp.zeros((bp, N_SHIFT), F32),
                                  jnp.zeros((bp, R_HEADS, R_HEAD, R_HEAD), F32), pa, bp, sp)
        zc = jnp.zeros((bp, CONV_W - 1, D_FF_PAD), F32)
        xp, cng, cnv = _post(xp, oa, orw, p_prompt[l].reshape(bp * sp, D_PLE), wts, zc, zc, bp, 1)
        outs["kp"].append(k.reshape(bp, sp, A_HEADS, A_DK))
        outs["vp"].append(v.reshape(bp, sp, A_HEADS, A_DV))
        outs["shp"].append(pr.reshape(bp, sp, N_SHIFT)[:, sp - 1])
        outs["wkp"].append(wkv_p)
        outs["cvp"].append(jnp.concatenate([cng[..., :D_FF], cnv[..., :D_FF]], axis=-1))

        q, k, v, kb, vb, pr = _inproj(xs, g_pre, w_in_bf)
        q_ht = jnp.swapaxes(q.reshape(bd, td, A_HEADS, A_DK), 1, 2).reshape(bd * A_HEADS * td, A_DK)
        oa = _attn_sample(lams, gsub, q_ht, k.reshape(bd * td * A_HEADS, A_DK), v.reshape(bd * td * A_HEADS, A_DV),
                          cache_k.reshape(depth, n_pool, page * A_HEADS, A_DK),
                          cache_v.reshape(depth, n_pool, page * A_HEADS, A_DV), l, page_table, td, lam_init)
        oa = jnp.swapaxes(oa.reshape(bd, A_HEADS, td, A_DV), 1, 2).reshape(bd, td, D_ATT)
        pr3 = pr.reshape(bd, td, N_SHIFT)
        prev3 = jnp.concatenate([state_shift[l][:, None, :], pr3[:, :td - 1]], axis=1)
        pad_t = ((0, 0), (0, t_pad - td), (0, 0))
        orw, wkv_s = _rwkv_sample(jnp.pad(pr3, pad_t).reshape(bd * t_pad, N_SHIFT),
                                  jnp.pad(prev3, pad_t).reshape(bd * t_pad, N_SHIFT),
                                  state_wkv[l], pa, bd, t_pad, td)
        orw = orw.reshape(bd, t_pad, D_RWKV)[:, :td]
        tmaj = lambda z: jnp.swapaxes(z, 0, 1).reshape(td * bd, z.shape[-1])
        c0g, c0v = halves(jnp.swapaxes(state_conv[l], 0, 1).reshape(1, (CONV_W - 1) * bd, 2 * D_FF))
        ys, cng, cnv = _post(tmaj(xs.reshape(bd, td, D_MODEL)), tmaj(oa), tmaj(orw), tmaj(p_sample[l]),
                             wts, c0g, c0v, 1, bd)
        xs = jnp.swapaxes(ys.reshape(td, bd, D_MODEL), 0, 1).reshape(bd * td, D_MODEL)
        cvs = jnp.concatenate([cng[..., :D_FF], cnv[..., :D_FF]], axis=-1).reshape(CONV_W - 1, bd, 2 * D_FF)
        outs["ks"].append(k.reshape(bd, td, A_HEADS, A_DK))
        outs["vs"].append(v.reshape(bd, td, A_HEADS, A_DV))
        outs["shs"].append(pr3[:, td - 1])
        outs["wks"].append(wkv_s)
        outs["cvs"].append(jnp.swapaxes(cvs, 0, 1))
    st = lambda key: jnp.stack(outs[key])
    return (xp.reshape(bp, sp, D_MODEL), xs.reshape(bd, td, D_MODEL), st("kp"), st("vp"), st("ks"), st("vs"),
            st("shp"), st("wkp"), st("cvp"), st("shs"), st("wks"), st("cvs"))
```

```python
import functools
import math

import jax
import jax.numpy as jnp
from jax import lax
from jax.experimental import pallas as pl
from jax.experimental.pallas import tpu as pltpu

F32 = jnp.float32
BF16 = jnp.bfloat16

D_MODEL = 1024
A_HEADS = 4
A_DH = 64
A_DK = 2 * A_DH
A_DV = 2 * A_DH
D_ATT = A_HEADS * A_DK
R_HEAD = 64
R_HEADS = 8
D_RWKV = R_HEADS * R_HEAD
LORA_W = 64
LORA_A = 64
LORA_G = 128
N_SHIFT = 3 * D_RWKV + LORA_W + LORA_A + LORA_G
N_IN = 3 * D_ATT + N_SHIFT
D_FF = 2752
CONV_W = 3
D_PLE = 256
NORM_EPS = 1e-6
GN_EPS = 64e-5
ATT_SCALE = A_DH ** -0.5
LOG2E = math.log2(math.e)

LANES = 128
SUBLANES = 8
VMEM_LIMIT_BYTES = 56 * 1024 * 1024

D_FF_PAD = ((D_FF + LANES - 1) // LANES) * LANES
PAIR = 2 * R_HEAD
RWKV_COLS = 4 * PAIR
N_COL_BLOCKS = D_RWKV // RWKV_COLS
X_COLS = LORA_W + LORA_A + LORA_G
CHUNK = 64
WKV_HEAD_STAGES = 2
ATTN_HEADS_PER_LOOP = 4
SAMPLE_BATCH_PER_STEP = 2


def _params(n_axes):
    return pltpu.CompilerParams(
        dimension_semantics=("arbitrary",) * n_axes,
        vmem_limit_bytes=VMEM_LIMIT_BYTES,
    )


def _const_spec(shape):
    zeros = (0,) * len(shape)
    return pl.BlockSpec(shape, lambda *_: zeros, pipeline_mode=pl.Buffered(1))


def _rms(x, g):
    return x * lax.rsqrt(jnp.mean(x * x, axis=-1, keepdims=True) + NORM_EPS) * g


def _dot(a, b):
    return jnp.dot(a, b, preferred_element_type=F32)


def _dot_nt(a, b):
    return lax.dot_general(a, b, (((1,), (1,)), ((), ())), preferred_element_type=F32)


def _split3(x):
    hi = x.astype(BF16)
    r1 = x - hi.astype(F32)
    mid = r1.astype(BF16)
    lo = (r1 - mid.astype(F32)).astype(BF16)
    return hi, mid, lo


def _dot_sel(x, sel, passes=3):
    out = None
    for part in _split3(x)[:passes]:
        t = _dot(part, sel)
        out = t if out is None else out + t
    return out


def _sel_dot(sel, x, passes=3):
    out = None
    for part in _split3(x)[:passes]:
        t = _dot(sel, part)
        out = t if out is None else out + t
    return out


def _sel_dot_nt(sel, x, passes=3):
    out = None
    for part in _split3(x)[:passes]:
        t = _dot_nt(sel, part)
        out = t if out is None else out + t
    return out


def _iota(shape, dim):
    return lax.broadcasted_iota(jnp.int32, shape, dim)


def _inproj_body(x_ref, g_ref, w_ref, q_ref, k_ref, v_ref, kb_ref, vb_ref, pr_ref):
    h = _rms(x_ref[...], g_ref[...]).astype(BF16)
    q = _dot(h, w_ref[:, 0:D_ATT])
    q_ref[...] = (q * (ATT_SCALE * LOG2E)).astype(BF16)
    k = _dot(h, w_ref[:, D_ATT:2 * D_ATT])
    kb_ref[...] = k.astype(BF16)
    v = _dot(h, w_ref[:, 2 * D_ATT:3 * D_ATT])
    vb_ref[...] = v.astype(BF16)
    for hd in range(A_HEADS):
        k_ref[:, hd, :] = k[:, hd * A_DK:(hd + 1) * A_DK]
        v_ref[:, hd, :] = v[:, hd * A_DV:(hd + 1) * A_DV]
    pr_ref[...] = _dot(h, w_ref[:, 3 * D_ATT:N_IN])


def _inproj(x, g, w_in_bf):
    n = x.shape[0]
    tm = min(512, n)
    row = lambda i: (i, 0)
    outs = (
        jax.ShapeDtypeStruct((n, D_ATT), BF16),
        jax.ShapeDtypeStruct((n, A_HEADS, A_DK), F32),
        jax.ShapeDtypeStruct((n, A_HEADS, A_DV), F32),
        jax.ShapeDtypeStruct((n, D_ATT), BF16),
        jax.ShapeDtypeStruct((n, D_ATT), BF16),
        jax.ShapeDtypeStruct((n, N_SHIFT), F32),
    )
    return pl.pallas_call(
        _inproj_body,
        grid=(n // tm,),
        in_specs=[
            pl.BlockSpec((tm, D_MODEL), row),
            _const_spec((1, D_MODEL)),
            _const_spec((D_MODEL, N_IN)),
        ],
        out_specs=tuple(pl.BlockSpec((tm,) + s.shape[1:], lambda i, nd=len(s.shape): (i,) + (0,) * (nd - 1))
                        for s in outs),
        out_shape=outs,
        compiler_params=_params(1),
        name="inproj",
    )(x, g, w_in_bf)


def _diff_lambda(lq1, lk1, lq2, lk2, lam_init):
    s1 = jnp.sum(lq1 * lk1, axis=-1, keepdims=True)
    s2 = jnp.sum(lq2 * lk2, axis=-1, keepdims=True)
    return jnp.exp(s1) - jnp.exp(s2) + lam_init


def _stack_maps(q):
    lane = _iota(q.shape, 1)
    zero = jnp.zeros_like(q)
    return jnp.concatenate([jnp.where(lane < A_DH, q, zero), jnp.where(lane >= A_DH, q, zero)], axis=0)


def _subln(o, gsub, lam_init):
    return _rms(o, gsub) * (1.0 - lam_init)


def _attn_prompt_body(lq1, lk1, lq2, lk2, gsub_ref, q_ref, k_ref, v_ref, o_ref, *, tq, lam_init):
    qi = pl.program_id(1)
    lam = _diff_lambda(lq1[...], lk1[...], lq2[...], lk2[...], lam_init)
    heads = [slice(h * A_DK, (h + 1) * A_DK) for h in range(A_HEADS)]
    qs = [_stack_maps(q_ref[:, cols]) for cols in heads]
    causal = _iota((2 * tq, tq), 1) <= _iota((2 * tq, tq), 0) % tq

    def tile(j, carry, group, diagonal):
        off = pl.multiple_of(j * tq, tq)
        out = []
        for h, (m, l, acc) in zip(group, carry):
            s = _dot_nt(qs[h], k_ref[pl.ds(off, tq), heads[h]])
            if diagonal:
                s = jnp.where(causal, s, -jnp.inf)
            m_new = jnp.maximum(m, jnp.max(s, axis=-1, keepdims=True))
            p = jnp.exp2(s - m_new)
            alpha = jnp.exp2(m - m_new)
            l = alpha * l + jnp.sum(p, axis=-1, keepdims=True)
            acc = alpha * acc + _dot(p.astype(BF16), v_ref[pl.ds(off, tq), heads[h]])
            out.append((m_new, l, acc))
        return tuple(out)

    for first in range(0, A_HEADS, ATTN_HEADS_PER_LOOP):
        group = tuple(range(first, first + ATTN_HEADS_PER_LOOP))
        init = tuple((jnp.full((2 * tq, 1), -jnp.inf, F32), jnp.zeros((2 * tq, 1), F32),
                      jnp.zeros((2 * tq, A_DV), F32)) for _ in group)
        carry = lax.fori_loop(0, qi, lambda j, c, group=group: tile(j, c, group, False), init)
        carry = tile(qi, carry, group, True)
        for h, (_, l, acc) in zip(group, carry):
            o = acc / l
            o = o[:tq] - lam * o[tq:]
            o_ref[:, heads[h]] = _subln(o, gsub_ref[...], lam_init).astype(o_ref.dtype)


def _attn_prompt(lams, gsub, q, kb, vb, b, s, lam_init):
    tq = min(512, s)
    nq = s // tq
    lam_spec = _const_spec((1, A_DH))
    return pl.pallas_call(
        functools.partial(_attn_prompt_body, tq=tq, lam_init=lam_init),
        grid=(b, nq),
        in_specs=[
            lam_spec, lam_spec, lam_spec, lam_spec,
            _const_spec((1, A_DV)),
            pl.BlockSpec((tq, D_ATT), lambda bi, qi: (bi * nq + qi, 0)),
            pl.BlockSpec((s, D_ATT), lambda bi, qi: (bi, 0)),
            pl.BlockSpec((s, D_ATT), lambda bi, qi: (bi, 0)),
        ],
        out_specs=pl.BlockSpec((tq, D_ATT), lambda bi, qi: (bi * nq + qi, 0)),
        out_shape=jax.ShapeDtypeStruct((b * s, D_ATT), BF16),
        compiler_params=_params(2),
        name="attn_prompt",
    )(*lams, gsub, q, kb, vb)


def _attn_sample_body(pt_ref, lq1, lk1, lq2, lk2, gsub_ref, q_ref, kn_ref, vn_ref, *rest,
                      n_batch, n_pages, t, lam_init):
    del pt_ref
    n_blk = n_batch * n_pages
    kp_refs, vp_refs, o_ref = rest[:n_blk], rest[n_blk:2 * n_blk], rest[2 * n_blk]
    lam = _diff_lambda(lq1[...], lk1[...], lq2[...], lk2[...], lam_init)
    nq = A_HEADS * t
    q_head = (_iota((2 * nq, 1), 0) % nq) // t
    q_tok = _iota((2 * nq, 1), 0) % t
    width = kp_refs[0].shape[0]
    past_ok = (_iota((2 * nq, width), 1) % A_HEADS) == q_head
    new_lane = _iota((2 * nq, nq), 1)
    new_ok = ((new_lane % A_HEADS) == q_head) & ((new_lane // A_HEADS) <= q_tok)
    for j in range(n_batch):
        rows = slice(j * nq, (j + 1) * nq)
        kps = kp_refs[j * n_pages:(j + 1) * n_pages]
        vps = vp_refs[j * n_pages:(j + 1) * n_pages]
        qs = _stack_maps(q_ref[rows, :])
        s_past = [jnp.where(past_ok, _dot_nt(qs, kp[...].astype(BF16)), -jnp.inf) for kp in kps]
        s_new = jnp.where(new_ok, _dot_nt(qs, kn_ref[rows, :].astype(BF16)), -jnp.inf)
        m = jnp.max(s_new, axis=-1, keepdims=True)
        for sp in s_past:
            m = jnp.maximum(m, jnp.max(sp, axis=-1, keepdims=True))
        p_new = jnp.exp2(s_new - m)
        l = jnp.sum(p_new, axis=-1, keepdims=True)
        acc = _dot(p_new.astype(BF16), vn_ref[rows, :].astype(BF16))
        for sp, vp in zip(s_past, vps):
            p = jnp.exp2(sp - m)
            l = l + jnp.sum(p, axis=-1, keepdims=True)
            acc = acc + _dot(p.astype(BF16), vp[...].astype(BF16))
        o = acc / l
        o = o[:nq] - lam * o[nq:]
        o_ref[rows, :] = _subln(o, gsub_ref[...], lam_init).astype(o_ref.dtype)


def _attn_sample(lams, gsub, q, k_new, v_new, cache_k, cache_v, layer, page_table, t, lam_init):
    bd, n_pages = page_table.shape
    nb = math.gcd(bd, SAMPLE_BATCH_PER_STEP)
    nq = A_HEADS * t
    rows = cache_k.shape[2]
    lam_spec = pl.BlockSpec((1, A_DH), lambda b, pt: (0, 0))
    tok_spec = pl.BlockSpec((nb * nq, A_DK), lambda b, pt: (b, 0))

    def page_spec(j, p):
        return pl.BlockSpec((None, None, rows, A_DK), lambda b, pt: (layer, pt[b * nb + j, p], 0, 0))

    page_specs = [page_spec(j, p) for j in range(nb) for p in range(n_pages)]
    grid_spec = pltpu.PrefetchScalarGridSpec(
        num_scalar_prefetch=1,
        grid=(bd // nb,),
        in_specs=[lam_spec] * 4 + [pl.BlockSpec((1, A_DV), lambda b, pt: (0, 0)), tok_spec, tok_spec, tok_spec]
        + page_specs * 2,
        out_specs=tok_spec,
    )
    n_blk = nb * n_pages
    return pl.pallas_call(
        functools.partial(_attn_sample_body, n_batch=nb, n_pages=n_pages, t=t, lam_init=lam_init),
        grid_spec=grid_spec,
        out_shape=jax.ShapeDtypeStruct((bd * nq, A_DV), BF16),
        compiler_params=_params(1),
        name="attn_sample",
    )(page_table, *lams, gsub, q, k_new, v_new, *([cache_k] * n_blk), *([cache_v] * n_blk))


def _head_blocks(n):
    return (_iota((n, n), 0) // R_HEAD) == (_iota((n, n), 1) // R_HEAD)


def _seg_sum(x, seg_ones):
    w = seg_ones.shape[0]
    xb = x.astype(BF16)
    return jnp.concatenate([_dot(xb[:, i:i + w], seg_ones) for i in range(0, x.shape[1], w)], axis=1)


def _softplus(z):
    return jnp.maximum(z, 0.0) + jnp.log1p(jnp.exp(-jnp.abs(z)))


def _rwkv_tokens(cur, prev, prm, seg_ones):
    mix = {c: cur[c] + (prev[c] - cur[c]) * prm["mu_" + c] for c in ("r", "k", "v", "x")}
    r, k, v = mix["r"], mix["k"], mix["v"]
    xwa = mix["x"][:, :LORA_W + LORA_A]
    xg = mix["x"][:, LORA_W + LORA_A:]
    w_in = prm["w0"] + _dot(jnp.tanh(xwa).astype(BF16), prm["w2"])
    lw = -jnp.exp(-_softplus(-w_in) - 0.5)
    a = jax.nn.sigmoid(prm["a0"] + _dot(xwa.astype(BF16), prm["a2"]))
    g = _dot(jax.nn.sigmoid(xg).astype(BF16), prm["g2"])
    kk = k * prm["k_k"]
    nrm = jnp.sqrt(_seg_sum(kk * kk, seg_ones))
    kk = kk / jnp.maximum(nrm, 1e-12)
    k2 = k * (1.0 + (a - 1.0) * prm["k_a"])
    bonus = _seg_sum(r * k2 * prm["r_k"], seg_ones) * v
    return r, lw, k2, v, kk, a, g, bonus


def _rwkv_finish(y, bonus, g, prm, seg_ones):
    mean = _seg_sum(y, seg_ones) * (1.0 / R_HEAD)
    d = y - mean
    var = _seg_sum(d * d, seg_ones) * (1.0 / R_HEAD)
    yn = d * lax.rsqrt(var + GN_EPS) * prm["ln_w"] + prm["ln_b"]
    return (yn + bonus) * g


def _stack_heads(x):
    lane = _iota(x.shape, 1)
    zero = jnp.zeros_like(x)
    return jnp.concatenate([jnp.where(lane < R_HEAD, x, zero), jnp.where(lane >= R_HEAD, x, zero)], axis=0)


def _wkv_front_stages(chunks):
    c = chunks[0][0].shape[0]
    n = 2 * c
    tril = (_iota((c, c), 1) <= _iota((c, c), 0)).astype(BF16)
    row = _iota((n, n), 0)
    col = _iota((n, n), 1)
    same = (row // c) == (col // c)
    strict = same & (col < row)
    incl = same & (col <= row)
    zero = jnp.zeros((n, n), F32)
    eye_b = (_iota((PAIR, PAIR), 0) == _iota((PAIR, PAIR), 1)).astype(BF16)
    env = {}

    def cumsum():
        env["big_l"] = [_sel_dot(tril, ch[1], passes=2) for ch in chunks]

    def operands():
        ops = []
        for (r, lw, k2, v, kk, a), big_l in zip(chunks, env["big_l"]):
            l_end = big_l[c - 1:c, :]
            e_in = jnp.exp(big_l)
            e_ex = jnp.exp(big_l - lw)
            e_neg = jnp.exp(-big_l)
            e_rem = jnp.exp(l_end - big_l)
            b = kk * a
            a_t = _stack_heads(-kk * e_ex)
            r_t = _stack_heads(r * e_in)
            ops.append(dict(
                l_end=l_end, a_t=a_t, r_t=r_t, a_tb=a_t.astype(BF16), r_tb=r_t.astype(BF16),
                b_t=_stack_heads(b * e_neg).astype(BF16), k_t=_stack_heads(k2 * e_neg).astype(BF16),
                b_h=_stack_heads(b * e_rem), k_h=_stack_heads(k2 * e_rem),
                v_s=_stack_heads(v).astype(BF16)))
        env["ops"] = ops

    def pair(name, lhs, rhs, mask):
        def run():
            env[name] = [jnp.where(mask, _dot_nt(o[lhs], o[rhs]), zero) for o in env["ops"]]
        return run

    def transposes():
        env["b_ht"] = [o["b_h"].T.astype(BF16) for o in env["ops"]]
        env["k_ht"] = [o["k_h"].T.astype(BF16) for o in env["ops"]]

    stages = [cumsum, operands, pair("m_ab", "a_tb", "b_t", strict), pair("m_ak", "a_tb", "k_t", strict),
              pair("m_rb", "r_tb", "b_t", incl), pair("m_rk", "r_tb", "k_t", incl), transposes]
    return stages, env


_WKV_FRONT_KEYS = ("ops", "m_ab", "m_ak", "m_rb", "m_rk", "b_ht", "k_ht")


def _wkv_back_stages(fronts, c):
    n = 2 * c
    steps = max(1, int(math.ceil(math.log2(c))))
    eye_n = jnp.where(_iota((n, n), 0) == _iota((n, n), 1), 1.0, 0.0)
    eye = (_iota((PAIR, PAIR), 0) == _iota((PAIR, PAIR), 1))
    env = {}

    def rhs():
        env.update({key: [x for f in fronts for x in f[key]] for key in _WKV_FRONT_KEYS})
        env["x"] = [jnp.concatenate([o["a_t"], _dot(mk.astype(BF16), o["v_s"])], axis=1).astype(BF16)
                    for o, mk in zip(env["ops"], env["m_ak"])]
        env["pb"] = [p.astype(BF16) for p in env["m_ab"]]
        env["t"] = [eye_n + p for p in env["m_ab"]]

    def square():
        env["pb"] = [_dot(pb, pb).astype(BF16) for pb in env["pb"]]

    def extend():
        env["t"] = [t + _dot(t.astype(BF16), pb) for t, pb in zip(env["t"], env["pb"])]

    def solve():
        env["w"] = [_dot(t.astype(BF16), x).astype(BF16) for t, x in zip(env["t"], env["x"])]

    def state_side():
        env["mg"] = [_dot(bt, w) for bt, w in zip(env["b_ht"], env["w"])]
        env["kv"] = [_dot(kt, o["v_s"]) for kt, o in zip(env["k_ht"], env["ops"])]

    def output_side():
        env["qy"] = [_dot(mr.astype(BF16), w) for mr, w in zip(env["m_rb"], env["w"])]
        env["rkv"] = [_dot(mr.astype(BF16), o["v_s"]) for mr, o in zip(env["m_rk"], env["ops"])]

    def both_sides():
        zeros = jnp.zeros((n, PAIR), BF16)
        big = []
        for o, bt, kt, mb, mk, w in zip(env["ops"], env["b_ht"], env["k_ht"], env["m_rb"], env["m_rk"], env["w"]):
            lhs = jnp.concatenate([jnp.concatenate([bt, kt], axis=1),
                                   jnp.concatenate([mb, mk], axis=1).astype(BF16)], axis=0)
            rhs_mat = jnp.concatenate([w, jnp.concatenate([zeros, o["v_s"]], axis=1)], axis=0)
            big.append(_dot(lhs, rhs_mat))
        env["mg"] = [x[:PAIR] for x in big]
        env["qy"] = [x[PAIR:] for x in big]
        env["kv"] = env["rkv"] = [0.0] * len(big)

    def finish():
        out = []
        for o, mg, kv, qy, rk in zip(env["ops"], env["mg"], env["kv"], env["qy"], env["rkv"]):
            m_mat = jnp.where(eye, jnp.exp(o["l_end"]), 0.0) + mg[:, :PAIR]
            g_mat = mg[:, PAIR:] + kv
            q_mat = o["r_t"] + qy[:, :PAIR]
            y0 = qy[:, PAIR:] + rk
            out.append((q_mat.astype(BF16), y0, m_mat.astype(BF16), g_mat))
        env["out"] = out

    stages = [rhs]
    for _ in range(steps - 1):
        stages += [square, extend]
    stages += [solve] + ([both_sides] if n % LANES == 0 else [state_side, output_side]) + [finish]
    return stages, env


def _wkv_prepare(chunks):
    front, env = _wkv_front_stages(chunks)
    for stage in front:
        stage()
    back, env = _wkv_back_stages([env], chunks[0][0].shape[0])
    for stage in back:
        stage()
    return env["out"]


def _wkv_apply(prep, st):
    q_mat, y0, m_mat, g_mat = prep
    c = y0.shape[0] // 2
    stb = st.astype(BF16)
    y_st = _dot(q_mat, stb) + y0
    return y_st[:c] + y_st[c:], _dot(m_mat, stb) + g_mat


def _state_in(s0):
    x = s0.reshape(PAIR, R_HEAD)
    sel = (_iota((PAIR, R_HEAD), 0) % R_HEAD == _iota((PAIR, R_HEAD), 1)).astype(BF16)
    full = _sel_dot_nt(sel, x)
    return jnp.where(_head_blocks(PAIR), full, 0.0)


def _state_out(st):
    folded = st[:R_HEAD] + st[R_HEAD:]
    eye = (_iota((PAIR, PAIR), 0) == _iota((PAIR, PAIR), 1)).astype(BF16)
    return _sel_dot_nt(eye, folded).reshape(2, R_HEAD, R_HEAD)


_RWKV_PARAM_NAMES = ("mu_r", "mu_k", "mu_v", "mu_x", "w0", "w2", "a0", "a2", "g2", "k_k", "k_a", "r_k",
                     "ln_w", "ln_b")


def _split_pairs(z, rows):
    return [z[rows, p * PAIR:(p + 1) * PAIR] for p in range(z.shape[1] // PAIR)]


def _run_chains(preps, states, between=None):
    states = list(states)
    ys = [[] for _ in preps]
    for i in range(len(preps[0])):
        for p, chain in enumerate(preps):
            y, states[p] = _wkv_apply(chain[i], states[p])
            ys[p].append(y)
        if between is not None:
            between()
    return jnp.concatenate([jnp.concatenate(col, axis=0) for col in ys], axis=1), states


def _rwkv_prompt_body(*refs, n_chunks):
    n_prm = len(_RWKV_PARAM_NAMES)
    pr = dict(zip(("r", "k", "v", "x"), refs[0:4]))
    sh = dict(zip(("r", "k", "v", "x"), refs[4:8]))
    prm = {nm: ref[...] for nm, ref in zip(_RWKV_PARAM_NAMES, refs[8:8 + n_prm])}
    s0_ref = refs[8 + n_prm]
    o_ref, s_out_ref = refs[9 + n_prm], refs[10 + n_prm]
    scratch = refs[11 + n_prm:]
    carry = dict(zip(("r", "k", "v", "x"), scratch[0:4]))
    st_ref, q_sc, y0_sc, m_sc, g_sc, bonus_sc, gate_sc = scratch[4:11]
    g = pl.program_id(2)
    tc, width = bonus_sc.shape
    n_pairs = width // PAIR
    cl = tc // n_chunks
    seg_ones = _head_blocks(min(width, 2 * LANES)).astype(BF16)

    @pl.when(g == 0)
    def _():
        for c in carry:
            carry[c][...] = jnp.broadcast_to(sh[c][...], carry[c].shape)
        for p in range(n_pairs):
            st_ref[p] = _state_in(s0_ref[2 * p:2 * p + 2])
        for ref in (q_sc, y0_sc, m_sc, g_sc, bonus_sc, gate_sc):
            ref[...] = jnp.zeros(ref.shape, ref.dtype)

    def stored():
        return [[(q_sc[p * n_chunks + i], y0_sc[p * n_chunks + i], m_sc[p * n_chunks + i],
                  g_sc[p * n_chunks + i]) for i in range(n_chunks)] for p in range(n_pairs)]

    def emit(y, bonus, gate, tile):
        rows = pl.ds(pl.multiple_of(tile * tc, tc), tc)
        o_ref[rows, :] = _rwkv_finish(y, bonus, gate, prm, seg_ones).astype(o_ref.dtype)

    old = stored()
    old_bonus, old_gate = bonus_sc[...], gate_sc[...]
    st_in = [st_ref[p] for p in range(n_pairs)]

    cur = {c: pr[c][...] for c in pr}
    prev = {}
    for c in cur:
        first = _iota(cur[c].shape, 0) == 0
        prev[c] = jnp.where(first, carry[c][0:1, :], pltpu.roll(cur[c], 1, 0))
    for c in cur:
        carry[c][...] = jnp.broadcast_to(cur[c][tc - 1:tc, :], carry[c].shape)
    r, lw, k2, v, kk, a, gate, bonus = _rwkv_tokens(cur, prev, prm, seg_ones)
    per_chunk = [list(zip(*(_split_pairs(z, slice(i * cl, (i + 1) * cl)) for z in (r, lw, k2, v, kk, a))))
                 for i in range(n_chunks)]
    fronts = [_wkv_front_stages([per_chunk[i][p] for i in range(n_chunks)]) for p in range(n_pairs)]
    todo = []
    for p, (stages, _) in enumerate(fronts):
        todo += stages[:WKV_HEAD_STAGES]
        if p > 0:
            todo += fronts[p - 1][0][WKV_HEAD_STAGES:]
    todo += fronts[-1][0][WKV_HEAD_STAGES:]
    back_stages, back_env = _wkv_back_stages([env for _, env in fronts], cl)
    todo += back_stages
    per = max(1, len(todo) // (n_chunks + 1))

    def between():
        for stage in todo[:per]:
            stage()
        del todo[:per]

    between()
    y_old, st = _run_chains(old, st_in, between)
    for stage in todo:
        stage()
    emit(y_old, old_bonus, old_gate, jnp.maximum(g - 1, 0))
    for p in range(n_pairs):
        st_ref[p] = jnp.where(g == 0, st_in[p], st[p])
    prepared = back_env["out"]
    for i, (q_mat, y0, m_mat, g_mat) in enumerate(prepared):
        q_sc[i], y0_sc[i], m_sc[i], g_sc[i] = q_mat, y0, m_mat, g_mat
    bonus_sc[...] = bonus
    gate_sc[...] = gate

    @pl.when(g == pl.num_programs(2) - 1)
    def _():
        y_new, st_end = _run_chains(stored(), [st_ref[p] for p in range(n_pairs)])
        emit(y_new, bonus_sc[...], gate_sc[...], g)
        for p in range(n_pairs):
            s_out_ref[2 * p:2 * p + 2] = _state_out(st_end[p])


def _rwkv_sample_body(*refs, n_seq, t_pad, t_valid):
    n_prm = len(_RWKV_PARAM_NAMES)
    pr = dict(zip(("r", "k", "v", "x"), refs[0:4]))
    pv = dict(zip(("r", "k", "v", "x"), refs[4:8]))
    prm = {nm: ref[...] for nm, ref in zip(_RWKV_PARAM_NAMES, refs[8:8 + n_prm])}
    s0_ref = refs[8 + n_prm]
    o_ref, s_out_ref = refs[9 + n_prm], refs[10 + n_prm]
    cur = {c: pr[c][...] for c in pr}
    prev = {c: pv[c][...] for c in pv}
    width = o_ref.shape[1]
    n_pairs = width // PAIR
    seg_ones = _head_blocks(min(width, 2 * LANES)).astype(BF16)
    r, lw, k2, v, kk, a, gate, bonus = _rwkv_tokens(cur, prev, prm, seg_ones)
    valid = (_iota(r.shape, 0) % t_pad) < t_valid
    zero = jnp.zeros_like(r)
    r, lw, k2, v, kk = (jnp.where(valid, z, zero) for z in (r, lw, k2, v, kk))
    per_seq = [list(zip(*(_split_pairs(z, slice(i * t_pad, (i + 1) * t_pad)) for z in (r, lw, k2, v, kk, a))))
               for i in range(n_seq)]
    chunks = [per_seq[i][p] for i in range(n_seq) for p in range(n_pairs)]
    states = [_state_in(s0_ref[i, 2 * p:2 * p + 2]) for i in range(n_seq) for p in range(n_pairs)]
    applied = [_wkv_apply(prep, st) for prep, st in zip(_wkv_prepare(chunks), states)]
    rows = []
    for i in range(n_seq):
        for p in range(n_pairs):
            s_out_ref[i, 2 * p:2 * p + 2] = _state_out(applied[i * n_pairs + p][1])
        rows.append(jnp.concatenate([applied[i * n_pairs + p][0] for p in range(n_pairs)], axis=1))
    y = jnp.concatenate(rows, axis=0)
    o_ref[...] = _rwkv_finish(y, bonus, gate, prm, seg_ones).astype(o_ref.dtype)


def _rwkv_param_arrays(p):
    z_w = jnp.zeros((LORA_A, D_RWKV), BF16)
    z_a = jnp.zeros((LORA_W, D_RWKV), BF16)
    return dict(
        mu=p["mu_shift"].reshape(1, N_SHIFT),
        w0=p["w0"].reshape(1, D_RWKV),
        w2=jnp.concatenate([p["w2"].astype(BF16), z_w], axis=0),
        a0=p["a0"].reshape(1, D_RWKV),
        a2=jnp.concatenate([z_a, p["a2"].astype(BF16)], axis=0),
        g2=p["g2"].astype(BF16),
        k_k=p["k_k"].reshape(1, D_RWKV),
        k_a=p["k_a"].reshape(1, D_RWKV),
        r_k=p["r_k"].reshape(1, D_RWKV),
        ln_w=p["ln_x_w"].reshape(1, D_RWKV),
        ln_b=p["ln_x_b"].reshape(1, D_RWKV),
    )


def _rwkv_param_specs(block_of):
    x_blk = (3 * D_RWKV) // X_COLS

    def vec(off):
        return pl.BlockSpec((1, RWKV_COLS), lambda *i: (0, off + block_of(*i)))

    def mat(rows):
        return pl.BlockSpec((rows, RWKV_COLS), lambda *i: (0, block_of(*i)))

    return [
        vec(0), vec(N_COL_BLOCKS), vec(2 * N_COL_BLOCKS),
        pl.BlockSpec((1, X_COLS), lambda *i: (0, x_blk)),
        vec(0), mat(LORA_W + LORA_A), vec(0), mat(LORA_W + LORA_A), mat(LORA_G),
        vec(0), vec(0), vec(0), vec(0), vec(0),
    ]


def _rwkv_param_operands(pa):
    return [pa["mu"], pa["mu"], pa["mu"], pa["mu"], pa["w0"], pa["w2"], pa["a0"], pa["a2"], pa["g2"],
            pa["k_k"], pa["k_a"], pa["r_k"], pa["ln_w"], pa["ln_b"]]


def _col_specs(rows, row_of, block_of):
    x_blk = (3 * D_RWKV) // X_COLS
    specs = [pl.BlockSpec((rows, RWKV_COLS), lambda *i, o=o: (row_of(*i), o * N_COL_BLOCKS + block_of(*i)))
             for o in range(3)]
    specs.append(pl.BlockSpec((rows, X_COLS), lambda *i: (row_of(*i), x_blk)))
    return specs


def _rwkv_prompt(pr, shift0, s0, pa, b, s):
    tc = min(512, s)
    nt = s // tc
    n_chunks = max(1, tc // CHUNK)
    cl = tc // n_chunks
    pairs = RWKV_COLS // PAIR
    block_of = lambda bi, ci, gi: ci
    row_of = lambda bi, ci, gi: bi * nt + gi
    x_blk = (3 * D_RWKV) // X_COLS
    shift_specs = [pl.BlockSpec((None, 1, RWKV_COLS), lambda bi, ci, gi, o=o: (bi, 0, o * N_COL_BLOCKS + ci))
                   for o in range(3)]
    shift_specs.append(pl.BlockSpec((None, 1, X_COLS), lambda bi, ci, gi: (bi, 0, x_blk)))
    state_spec = pl.BlockSpec((None, 2 * pairs, R_HEAD, R_HEAD), lambda bi, ci, gi: (bi, ci, 0, 0))
    shift3 = shift0.reshape(b, 1, N_SHIFT)
    return pl.pallas_call(
        functools.partial(_rwkv_prompt_body, n_chunks=n_chunks),
        grid=(b, N_COL_BLOCKS, nt),
        in_specs=_col_specs(tc, row_of, block_of) + shift_specs + _rwkv_param_specs(block_of) + [state_spec],
        out_specs=(pl.BlockSpec((s, RWKV_COLS), lambda bi, ci, gi: (bi, ci)), state_spec),
        out_shape=(jax.ShapeDtypeStruct((b * s, D_RWKV), BF16),
                   jax.ShapeDtypeStruct((b, R_HEADS, R_HEAD, R_HEAD), F32)),
        scratch_shapes=[pltpu.VMEM((SUBLANES, RWKV_COLS), F32)] * 3 + [
            pltpu.VMEM((SUBLANES, X_COLS), F32),
            pltpu.VMEM((pairs, PAIR, PAIR), F32),
            pltpu.VMEM((pairs * n_chunks, 2 * cl, PAIR), BF16),
            pltpu.VMEM((pairs * n_chunks, 2 * cl, PAIR), F32),
            pltpu.VMEM((pairs * n_chunks, PAIR, PAIR), BF16),
            pltpu.VMEM((pairs * n_chunks, PAIR, PAIR), F32),
            pltpu.VMEM((tc, RWKV_COLS), F32),
            pltpu.VMEM((tc, RWKV_COLS), F32),
        ],
        compiler_params=_params(3),
        name="rwkv_prompt",
    )(pr, pr, pr, pr, shift3, shift3, shift3, shift3, *_rwkv_param_operands(pa), s0)


def _rwkv_sample(pr_pad, prev_pad, s0, pa, bd, t_pad, t_valid):
    n_seq = min(16, bd)
    rows = n_seq * t_pad
    pairs = RWKV_COLS // PAIR
    block_of = lambda gi, ci: ci
    row_of = lambda gi, ci: gi
    state_spec = pl.BlockSpec((n_seq, 2 * pairs, R_HEAD, R_HEAD), lambda gi, ci: (gi, ci, 0, 0))
    return pl.pallas_call(
        functools.partial(_rwkv_sample_body, n_seq=n_seq, t_pad=t_pad, t_valid=t_valid),
        grid=(bd // n_seq, N_COL_BLOCKS),
        in_specs=_col_specs(rows, row_of, block_of) * 2 + _rwkv_param_specs(block_of) + [state_spec],
        out_specs=(pl.BlockSpec((rows, RWKV_COLS), lambda gi, ci: (gi, ci)), state_spec),
        out_shape=(jax.ShapeDtypeStruct((bd * t_pad, D_RWKV), BF16),
                   jax.ShapeDtypeStruct((bd, R_HEADS, R_HEAD, R_HEAD), F32)),
        compiler_params=_params(2),
        name="rwkv_sample",
    )(pr_pad, pr_pad, pr_pad, pr_pad, prev_pad, prev_pad, prev_pad, prev_pad, *_rwkv_param_operands(pa), s0)


def _gelu_tanh(x):
    k = math.sqrt(2.0 / math.pi)
    return (0.5 * x) * (1.0 + jnp.tanh(x * (k + (k * 0.044715) * (x * x))))


def _post_body(x_ref, oa_ref, or_ref, pe_ref, wo_ref, gmp_ref, gfp_ref, wg_ref, wv_ref, cwg_ref, cwv_ref,
               cbg_ref, cbv_ref, wd_ref, gfo_ref, wple_ref, wgate_ref, gple_ref, c0g_ref, c0v_ref,
               y_ref, cng_ref, cnv_ref, ext_g_ref, ext_v_ref, *, shift):
    ti = pl.program_id(1)
    n_state = (CONV_W - 1) * shift
    tm = x_ref.shape[0]
    pad = ext_g_ref.shape[0] - tm

    @pl.when(ti == 0)
    def _():
        ext_g_ref[pad - n_state:pad, :] = c0g_ref[...]
        ext_v_ref[pad - n_state:pad, :] = c0v_ref[...]

    mix = _dot(jnp.concatenate([oa_ref[...], or_ref[...]], axis=1), wo_ref[...])
    x1 = x_ref[...] + _rms(mix, gmp_ref[...])
    h = _rms(x1, gfp_ref[...]).astype(BF16)
    halves = []
    for w_ref, cw_ref, cb_ref, ext_ref in ((wg_ref, cwg_ref, cbg_ref, ext_g_ref),
                                           (wv_ref, cwv_ref, cbv_ref, ext_v_ref)):
        ext_ref[pad:pad + tm, :] = _dot(h, w_ref[...])
        conv = cb_ref[...]
        for j in range(CONV_W):
            start = pad - (CONV_W - 1 - j) * shift
            conv = conv + cw_ref[j:j + 1, :] * ext_ref[start:start + tm, :]
        halves.append(conv)
    act = _gelu_tanh(halves[0]) * halves[1]
    x2 = x1 + _rms(_dot(act.astype(BF16), wd_ref[...]), gfo_ref[...])
    gate = jax.nn.sigmoid(_dot(x2.astype(BF16), wgate_ref[...]))
    ple = _dot(pe_ref[...].astype(BF16), wple_ref[...]) * gate
    y_ref[...] = x2 + _rms(ple, gple_ref[...])

    for ext_ref, new_ref in ((ext_g_ref, cng_ref), (ext_v_ref, cnv_ref)):
        tail = ext_ref[pad + tm - n_state:pad + tm, :]
        ext_ref[pad - n_state:pad, :] = tail
        new_ref[...] = tail


def _post(x, oa, orw, pe, wts, conv0_g, conv0_v, n_seq, shift):
    n = x.shape[0]
    rows_per_seq = n // n_seq
    tm = min(512, rows_per_seq) if shift == 1 else rows_per_seq
    nt = rows_per_seq // tm
    n_state = (CONV_W - 1) * shift
    pad = max(SUBLANES, n_state)
    row = lambda si, ti: (si * nt + ti, 0)
    state_spec = pl.BlockSpec((None, n_state, D_FF_PAD), lambda si, ti: (si, 0, 0))
    w_arrays = [wts[k] for k in ("w_o", "g_mix_post", "g_ffn_pre", "w_gate_up", "w_val_up", "cw_g", "cw_v",
                                 "cb_g", "cb_v", "w_down", "g_ffn_post", "w_ple", "w_ple_gate", "g_ple")]
    return pl.pallas_call(
        functools.partial(_post_body, shift=shift),
        grid=(n_seq, nt),
        in_specs=[
            pl.BlockSpec((tm, D_MODEL), row),
            pl.BlockSpec((tm, D_ATT), row),
            pl.BlockSpec((tm, D_RWKV), row),
            pl.BlockSpec((tm, D_PLE), row),
        ] + [_const_spec(w.shape) for w in w_arrays] + [state_spec, state_spec],
        out_specs=(pl.BlockSpec((tm, D_MODEL), row), state_spec, state_spec),
        out_shape=(jax.ShapeDtypeStruct((n, D_MODEL), F32),
                   jax.ShapeDtypeStruct((n_seq, n_state, D_FF_PAD), F32),
                   jax.ShapeDtypeStruct((n_seq, n_state, D_FF_PAD), F32)),
        scratch_shapes=[pltpu.VMEM((pad + tm, D_FF_PAD), F32)] * 2,
        compiler_params=_params(2),
        name="post",
    )(x, oa, orw, pe, *w_arrays, conv0_g, conv0_v)


def _post_weights(p):
    pad_c = D_FF_PAD - D_FF

    def halves(a):
        widths = [(0, 0)] * (a.ndim - 1) + [(0, pad_c)]
        return jnp.pad(a[..., :D_FF], widths), jnp.pad(a[..., D_FF:], widths)

    w_g, w_v = halves(p["w_up"].astype(BF16))
    cw_g, cw_v = halves(p["conv_w"])
    cb_g, cb_v = halves(p["conv_b"].reshape(1, 2 * D_FF))
    return dict(
        w_o=p["w_o"].astype(BF16),
        g_mix_post=p["g_mix_post"].reshape(1, D_MODEL),
        g_ffn_pre=p["g_ffn_pre"].reshape(1, D_MODEL),
        w_gate_up=w_g, w_val_up=w_v, cw_g=cw_g, cw_v=cw_v, cb_g=cb_g, cb_v=cb_v,
        w_down=jnp.pad(p["w_down"].astype(BF16), ((0, pad_c), (0, 0))),
        g_ffn_post=p["g_ffn_post"].reshape(1, D_MODEL),
        w_ple=p["w_ple"].astype(BF16),
        w_ple_gate=p["w_ple_gate"].astype(BF16),
        g_ple=p["g_ple"].reshape(1, D_MODEL),
    ), halves


def kernel(x_prompt, x_sample, p_prompt, p_sample, cache_k, cache_v, page_table, state_shift, state_wkv,
           state_conv, g_mix_pre, w_in, lam_q1, lam_k1, lam_q2, lam_k2, g_subln, mu_shift, w0, w2, a0, a2,
           g2, k_k, k_a, r_k, ln_x_w, ln_x_b, w_o, g_mix_post, g_ffn_pre, w_up, conv_w, conv_b, w_down,
           g_ffn_post, w_ple, w_ple_gate, g_ple):
    depth = w_in.shape[0]
    bp, sp, _ = x_prompt.shape
    bd, td, _ = x_sample.shape
    n_pool, page = cache_k.shape[1], cache_k.shape[2]
    xp = x_prompt.reshape(bp * sp, D_MODEL)
    xs = x_sample.reshape(bd * td, D_MODEL)
    t_pad = SUBLANES
    outs = {k: [] for k in ("kp", "vp", "ks", "vs", "shp", "wkp", "cvp", "shs", "wks", "cvs")}
    for l in range(depth):
        lam_init = 0.8 - 0.6 * math.exp(-0.3 * l)
        p = dict(mu_shift=mu_shift[l], w0=w0[l], w2=w2[l], a0=a0[l], a2=a2[l], g2=g2[l], k_k=k_k[l],
                 k_a=k_a[l], r_k=r_k[l], ln_x_w=ln_x_w[l], ln_x_b=ln_x_b[l], w_o=w_o[l],
                 g_mix_post=g_mix_post[l], g_ffn_pre=g_ffn_pre[l], w_up=w_up[l], conv_w=conv_w[l],
                 conv_b=conv_b[l], w_down=w_down[l], g_ffn_post=g_ffn_post[l], w_ple=w_ple[l],
                 w_ple_gate=w_ple_gate[l], g_ple=g_ple[l])
        w_in_bf = w_in[l].astype(BF16)
        g_pre = g_mix_pre[l].reshape(1, D_MODEL)
        lams = [z[l].reshape(1, A_DH) for z in (lam_q1, lam_k1, lam_q2, lam_k2)]
        gsub = g_subln[l].reshape(1, A_DV)
        pa = _rwkv_param_arrays(p)
        wts, halves = _post_weights(p)

        q, k, v, kb, vb, pr = _inproj(xp, g_pre, w_in_bf)
        oa = _attn_prompt(lams, gsub, q, kb, vb, bp, sp, lam_init)
        orw, wkv_p = _rwkv_prompt(pr, jnp.zeros((bp, N_SHIFT), F32),
                                  jnp.zeros((bp, R_HEADS, R_HEAD, R_HEAD), F32), pa, bp, sp)
        zc = jnp.zeros((bp, CONV_W - 1, D_FF_PAD), F32)
        xp, cng, cnv = _post(xp, oa, orw, p_prompt[l].reshape(bp * sp, D_PLE), wts, zc, zc, bp, 1)
        outs["kp"].append(k.reshape(bp, sp, A_HEADS, A_DK))
        outs["vp"].append(v.reshape(bp, sp, A_HEADS, A_DV))
        outs["shp"].append(pr.reshape(bp, sp, N_SHIFT)[:, sp - 1])
        outs["wkp"].append(wkv_p)
        outs["cvp"].append(jnp.concatenate([cng[..., :D_FF], cnv[..., :D_FF]], axis=-1))

        q, k, v, kb, vb, pr = _inproj(xs, g_pre, w_in_bf)
        q_ht = jnp.swapaxes(q.reshape(bd, td, A_HEADS, A_DK), 1, 2).reshape(bd * A_HEADS * td, A_DK)
        oa = _attn_sample(lams, gsub, q_ht, k.reshape(bd * td * A_HEADS, A_DK), v.reshape(bd * td * A_HEADS, A_DV),
                          cache_k.reshape(depth, n_pool, page * A_HEADS, A_DK),
                          cache_v.reshape(depth, n_pool, page * A_HEADS, A_DV), l, page_table, td, lam_init)
        oa = jnp.swapaxes(oa.reshape(bd, A_HEADS, td, A_DV), 1, 2).reshape(bd, td, D_ATT)
        pr3 = pr.reshape(bd, td, N_SHIFT)
        prev3 = jnp.concatenate([state_shift[l][:, None, :], pr3[:, :td - 1]], axis=1)
        pad_t = ((0, 0), (0, t_pad - td), (0, 0))
        orw, wkv_s = _rwkv_sample(jnp.pad(pr3, pad_t).reshape(bd * t_pad, N_SHIFT),
                                  jnp.pad(prev3, pad_t).reshape(bd * t_pad, N_SHIFT),
                                  state_wkv[l], pa, bd, t_pad, td)
        orw = orw.reshape(bd, t_pad, D_RWKV)[:, :td]
        tmaj = lambda z: jnp.swapaxes(z, 0, 1).reshape(td * bd, z.shape[-1])
        c0g, c0v = halves(jnp.swapaxes(state_conv[l], 0, 1).reshape(1, (CONV_W - 1) * bd, 2 * D_FF))
        ys, cng, cnv = _post(tmaj(xs.reshape(bd, td, D_MODEL)), tmaj(oa), tmaj(orw), tmaj(p_sample[l]),
                             wts, c0g, c0v, 1, bd)
        xs = jnp.swapaxes(ys.reshape(td, bd, D_MODEL), 0, 1).reshape(bd * td, D_MODEL)
        cvs = jnp.concatenate([cng[..., :D_FF], cnv[..., :D_FF]], axis=-1).reshape(CONV_W - 1, bd, 2 * D_FF)
        outs["ks"].append(k.reshape(bd, td, A_HEADS, A_DK))
        outs["vs"].append(v.reshape(bd, td, A_HEADS, A_DV))
        outs["shs"].append(pr3[:, td - 1])
        outs["wks"].append(wkv_s)
        outs["cvs"].append(jnp.swapaxes(cvs, 0, 1))
    st = lambda key: jnp.stack(outs[key])
    return (xp.reshape(bp, sp, D_MODEL), xs.reshape(bd, td, D_MODEL), st("kp"), st("vp"), st("ks"), st("vs"),
            st("shp"), st("wkp"), st("cvp"), st("shs"), st("wks"), st("cvs"))
```

```python
import functools
import math

import jax
import jax.numpy as jnp
from jax import lax
from jax.experimental import pallas as pl
from jax.experimental.pallas import tpu as pltpu

F32 = jnp.float32
BF16 = jnp.bfloat16

D_MODEL = 1024
A_HEADS = 4
A_DH = 64
A_DK = 2 * A_DH
A_DV = 2 * A_DH
D_ATT = A_HEADS * A_DK
R_HEAD = 64
R_HEADS = 8
D_RWKV = R_HEADS * R_HEAD
LORA_W = 64
LORA_A = 64
LORA_G = 128
N_SHIFT = 3 * D_RWKV + LORA_W + LORA_A + LORA_G
N_IN = 3 * D_ATT + N_SHIFT
D_FF = 2752
CONV_W = 3
D_PLE = 256
NORM_EPS = 1e-6
GN_EPS = 64e-5
ATT_SCALE = A_DH ** -0.5
LOG2E = math.log2(math.e)

LANES = 128
SUBLANES = 8
VMEM_LIMIT_BYTES = 56 * 1024 * 1024

D_FF_PAD = ((D_FF + LANES - 1) // LANES) * LANES
PAIR = 2 * R_HEAD
RWKV_COLS = 4 * PAIR
N_COL_BLOCKS = D_RWKV // RWKV_COLS
X_COLS = LORA_W + LORA_A + LORA_G
CHUNK = 64
WKV_HEAD_STAGES = 2
ATTN_HEADS_PER_LOOP = 4
SAMPLE_BATCH_PER_STEP = 2


def _params(n_axes):
    return pltpu.CompilerParams(
        dimension_semantics=("arbitrary",) * n_axes,
        vmem_limit_bytes=VMEM_LIMIT_BYTES,
    )


def _const_spec(shape):
    zeros = (0,) * len(shape)
    return pl.BlockSpec(shape, lambda *_: zeros, pipeline_mode=pl.Buffered(1))


def _rms(x, g):
    return x * lax.rsqrt(jnp.mean(x * x, axis=-1, keepdims=True) + NORM_EPS) * g


def _dot(a, b):
    return jnp.dot(a, b, preferred_element_type=F32)


def _dot_nt(a, b):
    return lax.dot_general(a, b, (((1,), (1,)), ((), ())), preferred_element_type=F32)


def _split3(x):
    hi = x.astype(BF16)
    r1 = x - hi.astype(F32)
    mid = r1.astype(BF16)
    lo = (r1 - mid.astype(F32)).astype(BF16)
    return hi, mid, lo


def _sel_dot(sel, x, passes=3):
    out = None
    for part in _split3(x)[:passes]:
        t = _dot(sel, part)
        out = t if out is None else out + t
    return out


def _sel_dot_nt(sel, x, passes=3):
    out = None
    for part in _split3(x)[:passes]:
        t = _dot_nt(sel, part)
        out = t if out is None else out + t
    return out


def _iota(shape, dim):
    return lax.broadcasted_iota(jnp.int32, shape, dim)


def _inproj_body(x_ref, g_ref, w_ref, q_ref, k_ref, v_ref, kb_ref, vb_ref, pr_ref):
    h = _rms(x_ref[...], g_ref[...]).astype(BF16)
    q = _dot(h, w_ref[:, 0:D_ATT])
    q_ref[...] = (q * (ATT_SCALE * LOG2E)).astype(BF16)
    k = _dot(h, w_ref[:, D_ATT:2 * D_ATT])
    kb_ref[...] = k.astype(BF16)
    v = _dot(h, w_ref[:, 2 * D_ATT:3 * D_ATT])
    vb_ref[...] = v.astype(BF16)
    for hd in range(A_HEADS):
        k_ref[:, hd, :] = k[:, hd * A_DK:(hd + 1) * A_DK]
        v_ref[:, hd, :] = v[:, hd * A_DV:(hd + 1) * A_DV]
    pr_ref[...] = _dot(h, w_ref[:, 3 * D_ATT:N_IN])


def _inproj(x, g, w_in_bf):
    n = x.shape[0]
    tm = min(512, n)
    row = lambda i: (i, 0)
    outs = (
        jax.ShapeDtypeStruct((n, D_ATT), BF16),
        jax.ShapeDtypeStruct((n, A_HEADS, A_DK), F32),
        jax.ShapeDtypeStruct((n, A_HEADS, A_DV), F32),
        jax.ShapeDtypeStruct((n, D_ATT), BF16),
        jax.ShapeDtypeStruct((n, D_ATT), BF16),
        jax.ShapeDtypeStruct((n, N_SHIFT), F32),
    )
    return pl.pallas_call(
        _inproj_body,
        grid=(n // tm,),
        in_specs=[
            pl.BlockSpec((tm, D_MODEL), row),
            _const_spec((1, D_MODEL)),
            _const_spec((D_MODEL, N_IN)),
        ],
        out_specs=tuple(pl.BlockSpec((tm,) + s.shape[1:], lambda i, nd=len(s.shape): (i,) + (0,) * (nd - 1))
                        for s in outs),
        out_shape=outs,
        compiler_params=_params(1),
        name="inproj",
    )(x, g, w_in_bf)


def _diff_lambda(lq1, lk1, lq2, lk2, lam_init):
    s1 = jnp.sum(lq1 * lk1, axis=-1, keepdims=True)
    s2 = jnp.sum(lq2 * lk2, axis=-1, keepdims=True)
    return jnp.exp(s1) - jnp.exp(s2) + lam_init


def _stack_maps(q):
    lane = _iota(q.shape, 1)
    zero = jnp.zeros_like(q)
    return jnp.concatenate([jnp.where(lane < A_DH, q, zero), jnp.where(lane >= A_DH, q, zero)], axis=0)


def _subln(o, gsub, lam_init):
    return _rms(o, gsub) * (1.0 - lam_init)


def _attn_prompt_body(lq1, lk1, lq2, lk2, gsub_ref, q_ref, k_ref, v_ref, o_ref, *, tq, lam_init):
    qi = pl.program_id(1)
    lam = _diff_lambda(lq1[...], lk1[...], lq2[...], lk2[...], lam_init)
    heads = [slice(h * A_DK, (h + 1) * A_DK) for h in range(A_HEADS)]
    qs = [_stack_maps(q_ref[:, cols]) for cols in heads]
    causal = _iota((2 * tq, tq), 1) <= _iota((2 * tq, tq), 0) % tq

    def tile(j, carry, group, diagonal):
        off = pl.multiple_of(j * tq, tq)
        out = []
        for h, (m, l, acc) in zip(group, carry):
            s = _dot_nt(qs[h], k_ref[pl.ds(off, tq), heads[h]])
            if diagonal:
                s = jnp.where(causal, s, -jnp.inf)
            m_new = jnp.maximum(m, jnp.max(s, axis=-1, keepdims=True))
            p = jnp.exp2(s - m_new)
            alpha = jnp.exp2(m - m_new)
            l = alpha * l + jnp.sum(p, axis=-1, keepdims=True)
            acc = alpha * acc + _dot(p.astype(BF16), v_ref[pl.ds(off, tq), heads[h]])
            out.append((m_new, l, acc))
        return tuple(out)

    for first in range(0, A_HEADS, ATTN_HEADS_PER_LOOP):
        group = tuple(range(first, first + ATTN_HEADS_PER_LOOP))
        init = tuple((jnp.full((2 * tq, 1), -jnp.inf, F32), jnp.zeros((2 * tq, 1), F32),
                      jnp.zeros((2 * tq, A_DV), F32)) for _ in group)
        carry = lax.fori_loop(0, qi, lambda j, c, group=group: tile(j, c, group, False), init)
        carry = tile(qi, carry, group, True)
        for h, (_, l, acc) in zip(group, carry):
            o = acc / l
            o = o[:tq] - lam * o[tq:]
            o_ref[:, heads[h]] = _subln(o, gsub_ref[...], lam_init).astype(o_ref.dtype)


def _attn_prompt(lams, gsub, q, kb, vb, b, s, lam_init):
    tq = min(512, s)
    nq = s // tq
    lam_spec = _const_spec((1, A_DH))
    return pl.pallas_call(
        functools.partial(_attn_prompt_body, tq=tq, lam_init=lam_init),
        grid=(b, nq),
        in_specs=[
            lam_spec, lam_spec, lam_spec, lam_spec,
            _const_spec((1, A_DV)),
            pl.BlockSpec((tq, D_ATT), lambda bi, qi: (bi * nq + qi, 0)),
            pl.BlockSpec((s, D_ATT), lambda bi, qi: (bi, 0)),
            pl.BlockSpec((s, D_ATT), lambda bi, qi: (bi, 0)),
        ],
        out_specs=pl.BlockSpec((tq, D_ATT), lambda bi, qi: (bi * nq + qi, 0)),
        out_shape=jax.ShapeDtypeStruct((b * s, D_ATT), BF16),
        compiler_params=_params(2),
        name="attn_prompt",
    )(*lams, gsub, q, kb, vb)


def _attn_sample_body(pt_ref, lq1, lk1, lq2, lk2, gsub_ref, q_ref, kn_ref, vn_ref, *rest,
                      n_batch, n_pages, t, lam_init):
    del pt_ref
    n_blk = n_batch * n_pages
    kp_refs, vp_refs, o_ref = rest[:n_blk], rest[n_blk:2 * n_blk], rest[2 * n_blk]
    lam = _diff_lambda(lq1[...], lk1[...], lq2[...], lk2[...], lam_init)
    nq = A_HEADS * t
    q_head = (_iota((2 * nq, 1), 0) % nq) // t
    q_tok = _iota((2 * nq, 1), 0) % t
    width = kp_refs[0].shape[0]
    past_ok = (_iota((2 * nq, width), 1) % A_HEADS) == q_head
    new_lane = _iota((2 * nq, nq), 1)
    new_ok = ((new_lane % A_HEADS) == q_head) & ((new_lane // A_HEADS) <= q_tok)
    for j in range(n_batch):
        rows = slice(j * nq, (j + 1) * nq)
        kps = kp_refs[j * n_pages:(j + 1) * n_pages]
        vps = vp_refs[j * n_pages:(j + 1) * n_pages]
        qs = _stack_maps(q_ref[rows, :])
        s_past = [jnp.where(past_ok, _dot_nt(qs, kp[...].astype(BF16)), -jnp.inf) for kp in kps]
        s_new = jnp.where(new_ok, _dot_nt(qs, kn_ref[rows, :].astype(BF16)), -jnp.inf)
        m = jnp.max(s_new, axis=-1, keepdims=True)
        for sp in s_past:
            m = jnp.maximum(m, jnp.max(sp, axis=-1, keepdims=True))
        p_new = jnp.exp2(s_new - m)
        l = jnp.sum(p_new, axis=-1, keepdims=True)
        acc = _dot(p_new.astype(BF16), vn_ref[rows, :].astype(BF16))
        for sp, vp in zip(s_past, vps):
            p = jnp.exp2(sp - m)
            l = l + jnp.sum(p, axis=-1, keepdims=True)
            acc = acc + _dot(p.astype(BF16), vp[...].astype(BF16))
        o = acc / l
        o = o[:nq] - lam * o[nq:]
        o_ref[rows, :] = _subln(o, gsub_ref[...], lam_init).astype(o_ref.dtype)


def _attn_sample(lams, gsub, q, k_new, v_new, cache_k, cache_v, layer, page_table, t, lam_init):
    bd, n_pages = page_table.shape
    nb = math.gcd(bd, SAMPLE_BATCH_PER_STEP)
    nq = A_HEADS * t
    rows = cache_k.shape[2]
    lam_spec = pl.BlockSpec((1, A_DH), lambda b, pt: (0, 0))
    tok_spec = pl.BlockSpec((nb * nq, A_DK), lambda b, pt: (b, 0))

    def page_spec(j, p):
        return pl.BlockSpec((None, None, rows, A_DK), lambda b, pt: (layer, pt[b * nb + j, p], 0, 0))

    page_specs = [page_spec(j, p) for j in range(nb) for p in range(n_pages)]
    grid_spec = pltpu.PrefetchScalarGridSpec(
        num_scalar_prefetch=1,
        grid=(bd // nb,),
        in_specs=[lam_spec] * 4 + [pl.BlockSpec((1, A_DV), lambda b, pt: (0, 0)), tok_spec, tok_spec, tok_spec]
        + page_specs * 2,
        out_specs=tok_spec,
    )
    n_blk = nb * n_pages
    return pl.pallas_call(
        functools.partial(_attn_sample_body, n_batch=nb, n_pages=n_pages, t=t, lam_init=lam_init),
        grid_spec=grid_spec,
        out_shape=jax.ShapeDtypeStruct((bd * nq, A_DV), BF16),
        compiler_params=_params(1),
        name="attn_sample",
    )(page_table, *lams, gsub, q, k_new, v_new, *([cache_k] * n_blk), *([cache_v] * n_blk))


def _head_blocks(n):
    return (_iota((n, n), 0) // R_HEAD) == (_iota((n, n), 1) // R_HEAD)


def _seg_sum(x, seg_ones):
    w = seg_ones.shape[0]
    xb = x.astype(BF16)
    return jnp.concatenate([_dot(xb[:, i:i + w], seg_ones) for i in range(0, x.shape[1], w)], axis=1)


def _softplus(z):
    return jnp.maximum(z, 0.0) + jnp.log1p(jnp.exp(-jnp.abs(z)))


def _rwkv_tokens(cur, prev, prm, seg_ones):
    mix = {c: cur[c] + (prev[c] - cur[c]) * prm["mu_" + c] for c in ("r", "k", "v", "x")}
    r, k, v = mix["r"], mix["k"], mix["v"]
    xwa = mix["x"][:, :LORA_W + LORA_A]
    xg = mix["x"][:, LORA_W + LORA_A:]
    w_in = prm["w0"] + _dot(jnp.tanh(xwa).astype(BF16), prm["w2"])
    lw = -jnp.exp(-_softplus(-w_in) - 0.5)
    a = jax.nn.sigmoid(prm["a0"] + _dot(xwa.astype(BF16), prm["a2"]))
    g = _dot(jax.nn.sigmoid(xg).astype(BF16), prm["g2"])
    kk = k * prm["k_k"]
    nrm = jnp.sqrt(_seg_sum(kk * kk, seg_ones))
    kk = kk / jnp.maximum(nrm, 1e-12)
    k2 = k * (1.0 + (a - 1.0) * prm["k_a"])
    bonus = _seg_sum(r * k2 * prm["r_k"], seg_ones) * v
    return r, lw, k2, v, kk, a, g, bonus


def _rwkv_finish(y, bonus, g, prm, seg_ones):
    mean = _seg_sum(y, seg_ones) * (1.0 / R_HEAD)
    d = y - mean
    var = _seg_sum(d * d, seg_ones) * (1.0 / R_HEAD)
    yn = d * lax.rsqrt(var + GN_EPS) * prm["ln_w"] + prm["ln_b"]
    return (yn + bonus) * g


def _stack_heads(x):
    lane = _iota(x.shape, 1)
    zero = jnp.zeros_like(x)
    return jnp.concatenate([jnp.where(lane < R_HEAD, x, zero), jnp.where(lane >= R_HEAD, x, zero)], axis=0)


def _wkv_front_stages(chunks):
    c = chunks[0][0].shape[0]
    n = 2 * c
    tril = (_iota((c, c), 1) <= _iota((c, c), 0)).astype(BF16)
    row = _iota((n, n), 0)
    col = _iota((n, n), 1)
    same = (row // c) == (col // c)
    strict = same & (col < row)
    incl = same & (col <= row)
    zero = jnp.zeros((n, n), F32)
    env = {}

    def cumsum():
        env["big_l"] = [_sel_dot(tril, ch[1], passes=2) for ch in chunks]

    def operands():
        ops = []
        for (r, lw, k2, v, kk, a), big_l in zip(chunks, env["big_l"]):
            l_end = big_l[c - 1:c, :]
            e_in = jnp.exp(big_l)
            e_ex = jnp.exp(big_l - lw)
            e_neg = jnp.exp(-big_l)
            e_rem = jnp.exp(l_end - big_l)
            b = kk * a
            a_t = _stack_heads(-kk * e_ex)
            r_t = _stack_heads(r * e_in)
            ops.append(dict(
                l_end=l_end, a_t=a_t, r_t=r_t, a_tb=a_t.astype(BF16), r_tb=r_t.astype(BF16),
                b_t=_stack_heads(b * e_neg).astype(BF16), k_t=_stack_heads(k2 * e_neg).astype(BF16),
                b_h=_stack_heads(b * e_rem), k_h=_stack_heads(k2 * e_rem),
                v_s=_stack_heads(v).astype(BF16)))
        env["ops"] = ops

    def pair(name, lhs, rhs, mask):
        def run():
            env[name] = [jnp.where(mask, _dot_nt(o[lhs], o[rhs]), zero) for o in env["ops"]]
        return run

    def transposes():
        env["b_ht"] = [o["b_h"].T.astype(BF16) for o in env["ops"]]
        env["k_ht"] = [o["k_h"].T.astype(BF16) for o in env["ops"]]

    stages = [cumsum, operands, pair("m_ab", "a_tb", "b_t", strict), pair("m_ak", "a_tb", "k_t", strict),
              pair("m_rb", "r_tb", "b_t", incl), pair("m_rk", "r_tb", "k_t", incl), transposes]
    return stages, env


_WKV_FRONT_KEYS = ("ops", "m_ab", "m_ak", "m_rb", "m_rk", "b_ht", "k_ht")


def _wkv_back_stages(fronts, c):
    n = 2 * c
    steps = max(1, int(math.ceil(math.log2(c))))
    eye_n = jnp.where(_iota((n, n), 0) == _iota((n, n), 1), 1.0, 0.0)
    eye = (_iota((PAIR, PAIR), 0) == _iota((PAIR, PAIR), 1))
    env = {}

    def rhs():
        env.update({key: [x for f in fronts for x in f[key]] for key in _WKV_FRONT_KEYS})
        env["x"] = [jnp.concatenate([o["a_t"], _dot(mk.astype(BF16), o["v_s"])], axis=1).astype(BF16)
                    for o, mk in zip(env["ops"], env["m_ak"])]
        env["pb"] = [p.astype(BF16) for p in env["m_ab"]]
        env["t"] = [eye_n + p for p in env["m_ab"]]

    def square():
        env["pb"] = [_dot(pb, pb).astype(BF16) for pb in env["pb"]]

    def extend():
        env["t"] = [t + _dot(t.astype(BF16), pb) for t, pb in zip(env["t"], env["pb"])]

    def solve():
        env["w"] = [_dot(t.astype(BF16), x).astype(BF16) for t, x in zip(env["t"], env["x"])]

    def state_side():
        env["mg"] = [_dot(bt, w) for bt, w in zip(env["b_ht"], env["w"])]
        env["kv"] = [_dot(kt, o["v_s"]) for kt, o in zip(env["k_ht"], env["ops"])]

    def output_side():
        env["qy"] = [_dot(mr.astype(BF16), w) for mr, w in zip(env["m_rb"], env["w"])]
        env["rkv"] = [_dot(mr.astype(BF16), o["v_s"]) for mr, o in zip(env["m_rk"], env["ops"])]

    def both_sides():
        zeros = jnp.zeros((n, PAIR), BF16)
        big = []
        for o, bt, kt, mb, mk, w in zip(env["ops"], env["b_ht"], env["k_ht"], env["m_rb"], env["m_rk"], env["w"]):
            lhs = jnp.concatenate([jnp.concatenate([bt, kt], axis=1),
                                   jnp.concatenate([mb, mk], axis=1).astype(BF16)], axis=0)
            rhs_mat = jnp.concatenate([w, jnp.concatenate([zeros, o["v_s"]], axis=1)], axis=0)
            big.append(_dot(lhs, rhs_mat))
        env["mg"] = [x[:PAIR] for x in big]
        env["qy"] = [x[PAIR:] for x in big]
        env["kv"] = env["rkv"] = [0.0] * len(big)

    def finish():
        out = []
        for o, mg, kv, qy, rk in zip(env["ops"], env["mg"], env["kv"], env["qy"], env["rkv"]):
            m_mat = jnp.where(eye, jnp.exp(o["l_end"]), 0.0) + mg[:, :PAIR]
            g_mat = mg[:, PAIR:] + kv
            q_mat = o["r_t"] + qy[:, :PAIR]
            y0 = qy[:, PAIR:] + rk
            out.append((q_mat.astype(BF16), y0, m_mat.astype(BF16), g_mat))
        env["out"] = out

    stages = [rhs]
    for _ in range(steps - 1):
        stages += [square, extend]
    stages += [solve] + ([both_sides] if n % LANES == 0 else [state_side, output_side]) + [finish]
    return stages, env


def _wkv_prepare(chunks):
    front, env = _wkv_front_stages(chunks)
    for stage in front:
        stage()
    back, env = _wkv_back_stages([env], chunks[0][0].shape[0])
    for stage in back:
        stage()
    return env["out"]


def _wkv_apply(prep, st):
    q_mat, y0, m_mat, g_mat = prep
    c = y0.shape[0] // 2
    stb = st.astype(BF16)
    y_st = _dot(q_mat, stb) + y0
    return y_st[:c] + y_st[c:], _dot(m_mat, stb) + g_mat


def _state_in(s0):
    x = s0.reshape(PAIR, R_HEAD)
    sel = (_iota((PAIR, R_HEAD), 0) % R_HEAD == _iota((PAIR, R_HEAD), 1)).astype(BF16)
    full = _sel_dot_nt(sel, x)
    return jnp.where(_head_blocks(PAIR), full, 0.0)


def _state_out(st):
    folded = st[:R_HEAD] + st[R_HEAD:]
    eye = (_iota((PAIR, PAIR), 0) == _iota((PAIR, PAIR), 1)).astype(BF16)
    return _sel_dot_nt(eye, folded).reshape(2, R_HEAD, R_HEAD)


_RWKV_PARAM_NAMES = ("mu_r", "mu_k", "mu_v", "mu_x", "w0", "w2", "a0", "a2", "g2", "k_k", "k_a", "r_k",
                     "ln_w", "ln_b")


def _split_pairs(z, rows):
    return [z[rows, p * PAIR:(p + 1) * PAIR] for p in range(z.shape[1] // PAIR)]


def _run_chains(preps, states, between=None):
    states = list(states)
    ys = [[] for _ in preps]
    for i in range(len(preps[0])):
        for p, chain in enumerate(preps):
            y, states[p] = _wkv_apply(chain[i], states[p])
            ys[p].append(y)
        if between is not None:
            between()
    return jnp.concatenate([jnp.concatenate(col, axis=0) for col in ys], axis=1), states


def _rwkv_prompt_body(*refs, n_chunks):
    n_prm = len(_RWKV_PARAM_NAMES)
    pr = dict(zip(("r", "k", "v", "x"), refs[0:4]))
    sh = dict(zip(("r", "k", "v", "x"), refs[4:8]))
    prm = {nm: ref[...] for nm, ref in zip(_RWKV_PARAM_NAMES, refs[8:8 + n_prm])}
    s0_ref = refs[8 + n_prm]
    o_ref, s_out_ref = refs[9 + n_prm], refs[10 + n_prm]
    scratch = refs[11 + n_prm:]
    carry = dict(zip(("r", "k", "v", "x"), scratch[0:4]))
    st_ref, q_sc, y0_sc, m_sc, g_sc, bonus_sc, gate_sc = scratch[4:11]
    g = pl.program_id(2)
    tc, width = bonus_sc.shape
    n_pairs = width // PAIR
    cl = tc // n_chunks
    seg_ones = _head_blocks(min(width, 2 * LANES)).astype(BF16)

    @pl.when(g == 0)
    def _():
        for c in carry:
            carry[c][...] = jnp.broadcast_to(sh[c][...], carry[c].shape)
        for p in range(n_pairs):
            st_ref[p] = _state_in(s0_ref[2 * p:2 * p + 2])
        for ref in (q_sc, y0_sc, m_sc, g_sc, bonus_sc, gate_sc):
            ref[...] = jnp.zeros(ref.shape, ref.dtype)

    def stored():
        return [[(q_sc[p * n_chunks + i], y0_sc[p * n_chunks + i], m_sc[p * n_chunks + i],
                  g_sc[p * n_chunks + i]) for i in range(n_chunks)] for p in range(n_pairs)]

    def emit(y, bonus, gate, tile):
        rows = pl.ds(pl.multiple_of(tile * tc, tc), tc)
        o_ref[rows, :] = _rwkv_finish(y, bonus, gate, prm, seg_ones).astype(o_ref.dtype)

    old = stored()
    old_bonus, old_gate = bonus_sc[...], gate_sc[...]
    st_in = [st_ref[p] for p in range(n_pairs)]

    cur = {c: pr[c][...] for c in pr}
    prev = {}
    for c in cur:
        first = _iota(cur[c].shape, 0) == 0
        prev[c] = jnp.where(first, carry[c][0:1, :], pltpu.roll(cur[c], 1, 0))
    for c in cur:
        carry[c][...] = jnp.broadcast_to(cur[c][tc - 1:tc, :], carry[c].shape)
    r, lw, k2, v, kk, a, gate, bonus = _rwkv_tokens(cur, prev, prm, seg_ones)
    per_chunk = [list(zip(*(_split_pairs(z, slice(i * cl, (i + 1) * cl)) for z in (r, lw, k2, v, kk, a))))
                 for i in range(n_chunks)]
    fronts = [_wkv_front_stages([per_chunk[i][p] for i in range(n_chunks)]) for p in range(n_pairs)]
    todo = []
    for p, (stages, _) in enumerate(fronts):
        todo += stages[:WKV_HEAD_STAGES]
        if p > 0:
            todo += fronts[p - 1][0][WKV_HEAD_STAGES:]
    todo += fronts[-1][0][WKV_HEAD_STAGES:]
    back_stages, back_env = _wkv_back_stages([env for _, env in fronts], cl)
    todo += back_stages
    per = max(1, len(todo) // (n_chunks + 1))

    def between():
        for stage in todo[:per]:
            stage()
        del todo[:per]

    between()
    y_old, st = _run_chains(old, st_in, between)
    for stage in todo:
        stage()
    emit(y_old, old_bonus, old_gate, jnp.maximum(g - 1, 0))
    for p in range(n_pairs):
        st_ref[p] = jnp.where(g == 0, st_in[p], st[p])
    prepared = back_env["out"]
    for i, (q_mat, y0, m_mat, g_mat) in enumerate(prepared):
        q_sc[i], y0_sc[i], m_sc[i], g_sc[i] = q_mat, y0, m_mat, g_mat
    bonus_sc[...] = bonus
    gate_sc[...] = gate

    @pl.when(g == pl.num_programs(2) - 1)
    def _():
        y_new, st_end = _run_chains(stored(), [st_ref[p] for p in range(n_pairs)])
        emit(y_new, bonus_sc[...], gate_sc[...], g)
        for p in range(n_pairs):
            s_out_ref[2 * p:2 * p + 2] = _state_out(st_end[p])


def _rwkv_sample_body(*refs, n_seq, t_pad, t_valid):
    n_prm = len(_RWKV_PARAM_NAMES)
    pr = dict(zip(("r", "k", "v", "x"), refs[0:4]))
    pv = dict(zip(("r", "k", "v", "x"), refs[4:8]))
    prm = {nm: ref[...] for nm, ref in zip(_RWKV_PARAM_NAMES, refs[8:8 + n_prm])}
    s0_ref = refs[8 + n_prm]
    o_ref, s_out_ref = refs[9 + n_prm], refs[10 + n_prm]
    cur = {c: pr[c][...] for c in pr}
    prev = {c: pv[c][...] for c in pv}
    width = o_ref.shape[1]
    n_pairs = width // PAIR
    seg_ones = _head_blocks(min(width, 2 * LANES)).astype(BF16)
    r, lw, k2, v, kk, a, gate, bonus = _rwkv_tokens(cur, prev, prm, seg_ones)
    valid = (_iota(r.shape, 0) % t_pad) < t_valid
    zero = jnp.zeros_like(r)
    r, lw, k2, v, kk = (jnp.where(valid, z, zero) for z in (r, lw, k2, v, kk))
    per_seq = [list(zip(*(_split_pairs(z, slice(i * t_pad, (i + 1) * t_pad)) for z in (r, lw, k2, v, kk, a))))
               for i in range(n_seq)]
    chunks = [per_seq[i][p] for i in range(n_seq) for p in range(n_pairs)]
    states = [_state_in(s0_ref[i, 2 * p:2 * p + 2]) for i in range(n_seq) for p in range(n_pairs)]
    applied = [_wkv_apply(prep, st) for prep, st in zip(_wkv_prepare(chunks), states)]
    rows = []
    for i in range(n_seq):
        for p in range(n_pairs):
            s_out_ref[i, 2 * p:2 * p + 2] = _state_out(applied[i * n_pairs + p][1])
        rows.append(jnp.concatenate([applied[i * n_pairs + p][0] for p in range(n_pairs)], axis=1))
    y = jnp.concatenate(rows, axis=0)
    o_ref[...] = _rwkv_finish(y, bonus, gate, prm, seg_ones).astype(o_ref.dtype)


def _rwkv_param_arrays(p):
    z_w = jnp.zeros((LORA_A, D_RWKV), BF16)
    z_a = jnp.zeros((LORA_W, D_RWKV), BF16)
    return dict(
        mu=p["mu_shift"].reshape(1, N_SHIFT),
        w0=p["w0"].reshape(1, D_RWKV),
        w2=jnp.concatenate([p["w2"].astype(BF16), z_w], axis=0),
        a0=p["a0"].reshape(1, D_RWKV),
        a2=jnp.concatenate([z_a, p["a2"].astype(BF16)], axis=0),
        g2=p["g2"].astype(BF16),
        k_k=p["k_k"].reshape(1, D_RWKV),
        k_a=p["k_a"].reshape(1, D_RWKV),
        r_k=p["r_k"].reshape(1, D_RWKV),
        ln_w=p["ln_x_w"].reshape(1, D_RWKV),
        ln_b=p["ln_x_b"].reshape(1, D_RWKV),
    )


def _rwkv_param_specs(block_of):
    x_blk = (3 * D_RWKV) // X_COLS

    def vec(off):
        return pl.BlockSpec((1, RWKV_COLS), lambda *i: (0, off + block_of(*i)))

    def mat(rows):
        return pl.BlockSpec((rows, RWKV_COLS), lambda *i: (0, block_of(*i)))

    return [
        vec(0), vec(N_COL_BLOCKS), vec(2 * N_COL_BLOCKS),
        pl.BlockSpec((1, X_COLS), lambda *i: (0, x_blk)),
        vec(0), mat(LORA_W + LORA_A), vec(0), mat(LORA_W + LORA_A), mat(LORA_G),
        vec(0), vec(0), vec(0), vec(0), vec(0),
    ]


def _rwkv_param_operands(pa):
    return [pa["mu"], pa["mu"], pa["mu"], pa["mu"], pa["w0"], pa["w2"], pa["a0"], pa["a2"], pa["g2"],
            pa["k_k"], pa["k_a"], pa["r_k"], pa["ln_w"], pa["ln_b"]]


def _col_specs(rows, row_of, block_of):
    x_blk = (3 * D_RWKV) // X_COLS
    specs = [pl.BlockSpec((rows, RWKV_COLS), lambda *i, o=o: (row_of(*i), o * N_COL_BLOCKS + block_of(*i)))
             for o in range(3)]
    specs.append(pl.BlockSpec((rows, X_COLS), lambda *i: (row_of(*i), x_blk)))
    return specs


def _rwkv_prompt(pr, shift0, s0, pa, b, s):
    tc = min(512, s)
    nt = s // tc
    n_chunks = max(1, tc // CHUNK)
    cl = tc // n_chunks
    pairs = RWKV_COLS // PAIR
    block_of = lambda bi, ci, gi: ci
    row_of = lambda bi, ci, gi: bi * nt + gi
    x_blk = (3 * D_RWKV) // X_COLS
    shift_specs = [pl.BlockSpec((None, 1, RWKV_COLS), lambda bi, ci, gi, o=o: (bi, 0, o * N_COL_BLOCKS + ci))
                   for o in range(3)]
    shift_specs.append(pl.BlockSpec((None, 1, X_COLS), lambda bi, ci, gi: (bi, 0, x_blk)))
    state_spec = pl.BlockSpec((None, 2 * pairs, R_HEAD, R_HEAD), lambda bi, ci, gi: (bi, ci, 0, 0))
    shift3 = shift0.reshape(b, 1, N_SHIFT)
    return pl.pallas_call(
        functools.partial(_rwkv_prompt_body, n_chunks=n_chunks),
        grid=(b, N_COL_BLOCKS, nt),
        in_specs=_col_specs(tc, row_of, block_of) + shift_specs + _rwkv_param_specs(block_of) + [state_spec],
        out_specs=(pl.BlockSpec((s, RWKV_COLS), lambda bi, ci, gi: (bi, ci)), state_spec),
        out_shape=(jax.ShapeDtypeStruct((b * s, D_RWKV), BF16),
                   jax.ShapeDtypeStruct((b, R_HEADS, R_HEAD, R_HEAD), F32)),
        scratch_shapes=[pltpu.VMEM((SUBLANES, RWKV_COLS), F32)] * 3 + [
            pltpu.VMEM((SUBLANES, X_COLS), F32),
            pltpu.VMEM((pairs, PAIR, PAIR), F32),
            pltpu.VMEM((pairs * n_chunks, 2 * cl, PAIR), BF16),
            pltpu.VMEM((pairs * n_chunks, 2 * cl, PAIR), F32),
            pltpu.VMEM((pairs * n_chunks, PAIR, PAIR), BF16),
            pltpu.VMEM((pairs * n_chunks, PAIR, PAIR), F32),
            pltpu.VMEM((tc, RWKV_COLS), F32),
            pltpu.VMEM((tc, RWKV_COLS), F32),
        ],
        compiler_params=_params(3),
        name="rwkv_prompt",
    )(pr, pr, pr, pr, shift3, shift3, shift3, shift3, *_rwkv_param_operands(pa), s0)


def _rwkv_sample(pr_pad, prev_pad, s0, pa, bd, t_pad, t_valid):
    n_seq = min(16, bd)
    rows = n_seq * t_pad
    pairs = RWKV_COLS // PAIR
    block_of = lambda gi, ci: ci
    row_of = lambda gi, ci: gi
    state_spec = pl.BlockSpec((n_seq, 2 * pairs, R_HEAD, R_HEAD), lambda gi, ci: (gi, ci, 0, 0))
    return pl.pallas_call(
        functools.partial(_rwkv_sample_body, n_seq=n_seq, t_pad=t_pad, t_valid=t_valid),
        grid=(bd // n_seq, N_COL_BLOCKS),
        in_specs=_col_specs(rows, row_of, block_of) * 2 + _rwkv_param_specs(block_of) + [state_spec],
        out_specs=(pl.BlockSpec((rows, RWKV_COLS), lambda gi, ci: (gi, ci)), state_spec),
        out_shape=(jax.ShapeDtypeStruct((bd * t_pad, D_RWKV), BF16),
                   jax.ShapeDtypeStruct((bd, R_HEADS, R_HEAD, R_HEAD), F32)),
        compiler_params=_params(2),
        name="rwkv_sample",
    )(pr_pad, pr_pad, pr_pad, pr_pad, prev_pad, prev_pad, prev_pad, prev_pad, *_rwkv_param_operands(pa), s0)


def _gelu_tanh(x):
    k = math.sqrt(2.0 / math.pi)
    return (0.5 * x) * (1.0 + jnp.tanh(x * (k + (k * 0.044715) * (x * x))))


def _post_body(x_ref, oa_ref, or_ref, pe_ref, wo_ref, gmp_ref, gfp_ref, wg_ref, wv_ref, cwg_ref, cwv_ref,
               cbg_ref, cbv_ref, wd_ref, gfo_ref, wple_ref, wgate_ref, gple_ref, c0g_ref, c0v_ref,
               y_ref, cng_ref, cnv_ref, ext_g_ref, ext_v_ref, *, shift):
    ti = pl.program_id(1)
    n_state = (CONV_W - 1) * shift
    tm = x_ref.shape[0]
    pad = ext_g_ref.shape[0] - tm

    @pl.when(ti == 0)
    def _():
        ext_g_ref[pad - n_state:pad, :] = c0g_ref[...]
        ext_v_ref[pad - n_state:pad, :] = c0v_ref[...]

    mix = _dot(jnp.concatenate([oa_ref[...], or_ref[...]], axis=1), wo_ref[...])
    x1 = x_ref[...] + _rms(mix, gmp_ref[...])
    h = _rms(x1, gfp_ref[...]).astype(BF16)
    halves = []
    for w_ref, cw_ref, cb_ref, ext_ref in ((wg_ref, cwg_ref, cbg_ref, ext_g_ref),
                                           (wv_ref, cwv_ref, cbv_ref, ext_v_ref)):
        ext_ref[pad:pad + tm, :] = _dot(h, w_ref[...])
        conv = cb_ref[...]
        for j in range(CONV_W):
            start = pad - (CONV_W - 1 - j) * shift
            conv = conv + cw_ref[j:j + 1, :] * ext_ref[start:start + tm, :]
        halves.append(conv)
    act = _gelu_tanh(halves[0]) * halves[1]
    x2 = x1 + _rms(_dot(act.astype(BF16), wd_ref[...]), gfo_ref[...])
    gate = jax.nn.sigmoid(_dot(x2.astype(BF16), wgate_ref[...]))
    ple = _dot(pe_ref[...].astype(BF16), wple_ref[...]) * gate
    y_ref[...] = x2 + _rms(ple, gple_ref[...])

    for ext_ref, new_ref in ((ext_g_ref, cng_ref), (ext_v_ref, cnv_ref)):
        tail = ext_ref[pad + tm - n_state:pad + tm, :]
        ext_ref[pad - n_state:pad, :] = tail
        new_ref[...] = tail


def _post(x, oa, orw, pe, wts, conv0_g, conv0_v, n_seq, shift):
    n = x.shape[0]
    rows_per_seq = n // n_seq
    tm = min(512, rows_per_seq) if shift == 1 else rows_per_seq
    nt = rows_per_seq // tm
    n_state = (CONV_W - 1) * shift
    pad = max(SUBLANES, n_state)
    row = lambda si, ti: (si * nt + ti, 0)
    state_spec = pl.BlockSpec((None, n_state, D_FF_PAD), lambda si, ti: (si, 0, 0))
    w_arrays = [wts[k] for k in ("w_o", "g_mix_post", "g_ffn_pre", "w_gate_up", "w_val_up", "cw_g", "cw_v",
                                 "cb_g", "cb_v", "w_down", "g_ffn_post", "w_ple", "w_ple_gate", "g_ple")]
    return pl.pallas_call(
        functools.partial(_post_body, shift=shift),
        grid=(n_seq, nt),
        in_specs=[
            pl.BlockSpec((tm, D_MODEL), row),
            pl.BlockSpec((tm, D_ATT), row),
            pl.BlockSpec((tm, D_RWKV), row),
            pl.BlockSpec((tm, D_PLE), row),
        ] + [_const_spec(w.shape) for w in w_arrays] + [state_spec, state_spec],
        out_specs=(pl.BlockSpec((tm, D_MODEL), row), state_spec, state_spec),
        out_shape=(jax.ShapeDtypeStruct((n, D_MODEL), F32),
                   jax.ShapeDtypeStruct((n_seq, n_state, D_FF_PAD), F32),
                   jax.ShapeDtypeStruct((n_seq, n_state, D_FF_PAD), F32)),
        scratch_shapes=[pltpu.VMEM((pad + tm, D_FF_PAD), F32)] * 2,
        compiler_params=_params(2),
        name="post",
    )(x, oa, orw, pe, *w_arrays, conv0_g, conv0_v)


def _post_weights(p):
    pad_c = D_FF_PAD - D_FF

    def halves(a):
        widths = [(0, 0)] * (a.ndim - 1) + [(0, pad_c)]
        return jnp.pad(a[..., :D_FF], widths), jnp.pad(a[..., D_FF:], widths)

    w_g, w_v = halves(p["w_up"].astype(BF16))
    cw_g, cw_v = halves(p["conv_w"])
    cb_g, cb_v = halves(p["conv_b"].reshape(1, 2 * D_FF))
    return dict(
        w_o=p["w_o"].astype(BF16),
        g_mix_post=p["g_mix_post"].reshape(1, D_MODEL),
        g_ffn_pre=p["g_ffn_pre"].reshape(1, D_MODEL),
        w_gate_up=w_g, w_val_up=w_v, cw_g=cw_g, cw_v=cw_v, cb_g=cb_g, cb_v=cb_v,
        w_down=jnp.pad(p["w_down"].astype(BF16), ((0, pad_c), (0, 0))),
        g_ffn_post=p["g_ffn_post"].reshape(1, D_MODEL),
        w_ple=p["w_ple"].astype(BF16),
        w_ple_gate=p["w_ple_gate"].astype(BF16),
        g_ple=p["g_ple"].reshape(1, D_MODEL),
    ), halves


def kernel(x_prompt, x_sample, p_prompt, p_sample, cache_k, cache_v, page_table, state_shift, state_wkv,
           state_conv, g_mix_pre, w_in, lam_q1, lam_k1, lam_q2, lam_k2, g_subln, mu_shift, w0, w2, a0, a2,
           g2, k_k, k_a, r_k, ln_x_w, ln_x_b, w_o, g_mix_post, g_ffn_pre, w_up, conv_w, conv_b, w_down,
           g_ffn_post, w_ple, w_ple_gate, g_ple):
    depth = w_in.shape[0]
    bp, sp, _ = x_prompt.shape
    bd, td, _ = x_sample.shape
    n_pool, page = cache_k.shape[1], cache_k.shape[2]
    xp = x_prompt.reshape(bp * sp, D_MODEL)
    xs = x_sample.reshape(bd * td, D_MODEL)
    t_pad = SUBLANES
    outs = {k: [] for k in ("kp", "vp", "ks", "vs", "shp", "wkp", "cvp", "shs", "wks", "cvs")}
    for l in range(depth):
        lam_init = 0.8 - 0.6 * math.exp(-0.3 * l)
        p = dict(mu_shift=mu_shift[l], w0=w0[l], w2=w2[l], a0=a0[l], a2=a2[l], g2=g2[l], k_k=k_k[l],
                 k_a=k_a[l], r_k=r_k[l], ln_x_w=ln_x_w[l], ln_x_b=ln_x_b[l], w_o=w_o[l],
                 g_mix_post=g_mix_post[l], g_ffn_pre=g_ffn_pre[l], w_up=w_up[l], conv_w=conv_w[l],
                 conv_b=conv_b[l], w_down=w_down[l], g_ffn_post=g_ffn_post[l], w_ple=w_ple[l],
                 w_ple_gate=w_ple_gate[l], g_ple=g_ple[l])
        w_in_bf = w_in[l].astype(BF16)
        g_pre = g_mix_pre[l].reshape(1, D_MODEL)
        lams = [z[l].reshape(1, A_DH) for z in (lam_q1, lam_k1, lam_q2, lam_k2)]
        gsub = g_subln[l].reshape(1, A_DV)
        pa = _rwkv_param_arrays(p)
        wts, halves = _post_weights(p)

        q, k, v, kb, vb, pr = _inproj(xp, g_pre, w_in_bf)
        oa = _attn_prompt(lams, gsub, q, kb, vb, bp, sp, lam_init)
        orw, wkv_p = _rwkv_prompt(pr, jnp.zeros((bp, N_SHIFT), F32),
                                  jnp.zeros((bp, R_HEADS, R_HEAD, R_HEAD), F32), pa, bp, sp)
        zc = jnp.zeros((bp, CONV_W - 1, D_FF_PAD), F32)
        xp, cng, cnv = _post(xp, oa, orw, p_prompt[l].reshape(bp * sp, D_PLE), wts, zc, zc, bp, 1)
        outs["kp"].append(k.reshape(bp, sp, A_HEADS, A_DK))
        outs["vp"].append(v.reshape(bp, sp, A_HEADS, A_DV))
        outs["shp"].append(pr.reshape(bp, sp, N_SHIFT)[:, sp - 1])
        outs["wkp"].append(wkv_p)
        outs["cvp"].append(jnp.concatenate([cng[..., :D_FF], cnv[..., :D_FF]], axis=-1))

        q, k, v, kb, vb, pr = _inproj(xs, g_pre, w_in_bf)
        q_ht = jnp.swapaxes(q.reshape(bd, td, A_HEADS, A_DK), 1, 2).reshape(bd * A_HEADS * td, A_DK)
        oa = _attn_sample(lams, gsub, q_ht, k.reshape(bd * td * A_HEADS, A_DK), v.reshape(bd * td * A_HEADS, A_DV),
                          cache_k.reshape(depth, n_pool, page * A_HEADS, A_DK),
                          cache_v.reshape(depth, n_pool, page * A_HEADS, A_DV), l, page_table, td, lam_init)
        oa = jnp.swapaxes(oa.reshape(bd, A_HEADS, td, A_DV), 1, 2).reshape(bd, td, D_ATT)
        pr3 = pr.reshape(bd, td, N_SHIFT)
        prev3 = jnp.concatenate([state_shift[l][:, None, :], pr3[:, :td - 1]], axis=1)
        pad_t = ((0, 0), (0, t_pad - td), (0, 0))
        orw, wkv_s = _rwkv_sample(jnp.pad(pr3, pad_t).reshape(bd * t_pad, N_SHIFT),
                                  jnp.pad(prev3, pad_t).reshape(bd * t_pad, N_SHIFT),
                                  state_wkv[l], pa, bd, t_pad, td)
        orw = orw.reshape(bd, t_pad, D_RWKV)[:, :td]
        tmaj = lambda z: jnp.swapaxes(z, 0, 1).reshape(td * bd, z.shape[-1])
        c0g, c0v = halves(jnp.swapaxes(state_conv[l], 0, 1).reshape(1, (CONV_W - 1) * bd, 2 * D_FF))
        ys, cng, cnv = _post(tmaj(xs.reshape(bd, td, D_MODEL)), tmaj(oa), tmaj(orw), tmaj(p_sample[l]),
                             wts, c0g, c0v, 1, bd)
        xs = jnp.swapaxes(ys.reshape(td, bd, D_MODEL), 0, 1).reshape(bd * td, D_MODEL)
        cvs = jnp.concatenate([cng[..., :D_FF], cnv[..., :D_FF]], axis=-1).reshape(CONV_W - 1, bd, 2 * D_FF)
        outs["ks"].append(k.reshape(bd, td, A_HEADS, A_DK))
        outs["vs"].append(v.reshape(bd, td, A_HEADS, A_DV))
        outs["shs"].append(pr3[:, td - 1])
        outs["wks"].append(wkv_s)
        outs["cvs"].append(jnp.swapaxes(cvs, 0, 1))
    st = lambda key: jnp.stack(outs[key])
    return (xp.reshape(bp, sp, D_MODEL), xs.reshape(bd, td, D_MODEL), st("kp"), st("vp"), st("ks"), st("vs"),
            st("shp"), st("wkp"), st("cvp"), st("shs"), st("wks"), st("cvs"))
```

```python
import functools
import math

import jax
import jax.numpy as jnp
from jax import lax
from jax.experimental import pallas as pl
from jax.experimental.pallas import tpu as pltpu

F32 = jnp.float32
BF16 = jnp.bfloat16

D_MODEL = 1024
A_HEADS = 4
A_DH = 64
A_DK = 2 * A_DH
A_DV = 2 * A_DH
D_ATT = A_HEADS * A_DK
R_HEAD = 64
R_HEADS = 8
D_RWKV = R_HEADS * R_HEAD
LORA_W = 64
LORA_A = 64
LORA_G = 128
N_SHIFT = 3 * D_RWKV + LORA_W + LORA_A + LORA_G
N_IN = 3 * D_ATT + N_SHIFT
D_FF = 2752
CONV_W = 3
D_PLE = 256
NORM_EPS = 1e-6
GN_EPS = 64e-5
ATT_SCALE = A_DH ** -0.5
LOG2E = math.log2(math.e)

LANES = 128
SUBLANES = 8
VMEM_LIMIT_BYTES = 56 * 1024 * 1024

D_FF_PAD = ((D_FF + LANES - 1) // LANES) * LANES
PAIR = 2 * R_HEAD
RWKV_COLS = 4 * PAIR
N_COL_BLOCKS = D_RWKV // RWKV_COLS
X_COLS = LORA_W + LORA_A + LORA_G
CHUNK = 64
WKV_HEAD_STAGES = 2
ATTN_HEADS_PER_LOOP = 4
SAMPLE_BATCH_PER_STEP = 2


def _params(n_axes):
    return pltpu.CompilerParams(
        dimension_semantics=("arbitrary",) * n_axes,
        vmem_limit_bytes=VMEM_LIMIT_BYTES,
    )


def _const_spec(shape):
    zeros = (0,) * len(shape)
    return pl.BlockSpec(shape, lambda *_: zeros, pipeline_mode=pl.Buffered(1))


def _rms(x, g):
    return x * lax.rsqrt(jnp.mean(x * x, axis=-1, keepdims=True) + NORM_EPS) * g


def _dot(a, b):
    return jnp.dot(a, b, preferred_element_type=F32)


def _dot_nt(a, b):
    return lax.dot_general(a, b, (((1,), (1,)), ((), ())), preferred_element_type=F32)


def _split3(x):
    hi = x.astype(BF16)
    r1 = x - hi.astype(F32)
    mid = r1.astype(BF16)
    lo = (r1 - mid.astype(F32)).astype(BF16)
    return hi, mid, lo


def _sel_dot(sel, x, passes=3):
    out = None
    for part in _split3(x)[:passes]:
        t = _dot(sel, part)
        out = t if out is None else out + t
    return out


def _sel_dot_nt(sel, x, passes=3):
    out = None
    for part in _split3(x)[:passes]:
        t = _dot_nt(sel, part)
        out = t if out is None else out + t
    return out


def _iota(shape, dim):
    return lax.broadcasted_iota(jnp.int32, shape, dim)


def _inproj_body(x_ref, g_ref, w_ref, q_ref, k_ref, v_ref, kb_ref, vb_ref, pr_ref, tail_ref):
    h = _rms(x_ref[...], g_ref[...]).astype(BF16)
    q = _dot(h, w_ref[:, 0:D_ATT])
    q_ref[...] = (q * (ATT_SCALE * LOG2E)).astype(BF16)
    k = _dot(h, w_ref[:, D_ATT:2 * D_ATT])
    kb_ref[...] = k.astype(BF16)
    v = _dot(h, w_ref[:, 2 * D_ATT:3 * D_ATT])
    vb_ref[...] = v.astype(BF16)
    for hd in range(A_HEADS):
        k_ref[:, hd, :] = k[:, hd * A_DK:(hd + 1) * A_DK]
        v_ref[:, hd, :] = v[:, hd * A_DV:(hd + 1) * A_DV]
    pr = _dot(h, w_ref[:, 3 * D_ATT:N_IN])
    pr_ref[...] = pr.astype(pr_ref.dtype)
    tail_ref[...] = pr[pr.shape[0] - SUBLANES:, :]


def _inproj(x, g, w_in_bf, tm, pr_dtype):
    n = x.shape[0]
    row = lambda i: (i, 0)
    outs = (
        jax.ShapeDtypeStruct((n, D_ATT), BF16),
        jax.ShapeDtypeStruct((n, A_HEADS, A_DK), F32),
        jax.ShapeDtypeStruct((n, A_HEADS, A_DV), F32),
        jax.ShapeDtypeStruct((n, D_ATT), BF16),
        jax.ShapeDtypeStruct((n, D_ATT), BF16),
        jax.ShapeDtypeStruct((n, N_SHIFT), pr_dtype),
        jax.ShapeDtypeStruct((n // tm * SUBLANES, N_SHIFT), F32),
    )
    return pl.pallas_call(
        _inproj_body,
        grid=(n // tm,),
        in_specs=[
            pl.BlockSpec((tm, D_MODEL), row),
            _const_spec((1, D_MODEL)),
            _const_spec((D_MODEL, N_IN)),
        ],
        out_specs=tuple(pl.BlockSpec((tm * s.shape[0] // n,) + s.shape[1:],
                                     lambda i, nd=len(s.shape): (i,) + (0,) * (nd - 1)) for s in outs),
        out_shape=outs,
        compiler_params=_params(1),
        name="inproj",
    )(x, g, w_in_bf)


def _diff_lambda(lq1, lk1, lq2, lk2, lam_init):
    s1 = jnp.sum(lq1 * lk1, axis=-1, keepdims=True)
    s2 = jnp.sum(lq2 * lk2, axis=-1, keepdims=True)
    return jnp.exp(s1) - jnp.exp(s2) + lam_init


def _stack_maps(q):
    lane = _iota(q.shape, 1)
    zero = jnp.zeros_like(q)
    return jnp.concatenate([jnp.where(lane < A_DH, q, zero), jnp.where(lane >= A_DH, q, zero)], axis=0)


def _subln(o, gsub, lam_init):
    return _rms(o, gsub) * (1.0 - lam_init)


def _attn_prompt_body(lq1, lk1, lq2, lk2, gsub_ref, q_ref, k_ref, v_ref, o_ref, *, tq, lam_init):
    qi = pl.program_id(1)
    lam = _diff_lambda(lq1[...], lk1[...], lq2[...], lk2[...], lam_init)
    heads = [slice(h * A_DK, (h + 1) * A_DK) for h in range(A_HEADS)]
    qs = [_stack_maps(q_ref[:, cols]) for cols in heads]
    causal = _iota((2 * tq, tq), 1) <= _iota((2 * tq, tq), 0) % tq

    def tile(j, carry, group, diagonal):
        off = pl.multiple_of(j * tq, tq)
        out = []
        for h, (m, l, acc) in zip(group, carry):
            s = _dot_nt(qs[h], k_ref[pl.ds(off, tq), heads[h]])
            if diagonal:
                s = jnp.where(causal, s, -jnp.inf)
            m_new = jnp.maximum(m, jnp.max(s, axis=-1, keepdims=True))
            p = jnp.exp2(s - m_new)
            alpha = jnp.exp2(m - m_new)
            l = alpha * l + jnp.sum(p, axis=-1, keepdims=True)
            acc = alpha * acc + _dot(p.astype(BF16), v_ref[pl.ds(off, tq), heads[h]])
            out.append((m_new, l, acc))
        return tuple(out)

    for first in range(0, A_HEADS, ATTN_HEADS_PER_LOOP):
        group = tuple(range(first, first + ATTN_HEADS_PER_LOOP))
        init = tuple((jnp.full((2 * tq, 1), -jnp.inf, F32), jnp.zeros((2 * tq, 1), F32),
                      jnp.zeros((2 * tq, A_DV), F32)) for _ in group)
        carry = lax.fori_loop(0, qi, lambda j, c, group=group: tile(j, c, group, False), init)
        carry = tile(qi, carry, group, True)
        for h, (_, l, acc) in zip(group, carry):
            o = acc / l
            o = o[:tq] - lam * o[tq:]
            o_ref[:, heads[h]] = _subln(o, gsub_ref[...], lam_init).astype(o_ref.dtype)


def _attn_prompt(lams, gsub, q, kb, vb, b, s, lam_init):
    tq = min(512, s)
    nq = s // tq
    lam_spec = _const_spec((1, A_DH))
    return pl.pallas_call(
        functools.partial(_attn_prompt_body, tq=tq, lam_init=lam_init),
        grid=(b, nq),
        in_specs=[
            lam_spec, lam_spec, lam_spec, lam_spec,
            _const_spec((1, A_DV)),
            pl.BlockSpec((tq, D_ATT), lambda bi, qi: (bi * nq + qi, 0)),
            pl.BlockSpec((s, D_ATT), lambda bi, qi: (bi, 0)),
            pl.BlockSpec((s, D_ATT), lambda bi, qi: (bi, 0)),
        ],
        out_specs=pl.BlockSpec((tq, D_ATT), lambda bi, qi: (bi * nq + qi, 0)),
        out_shape=jax.ShapeDtypeStruct((b * s, D_ATT), BF16),
        compiler_params=_params(2),
        name="attn_prompt",
    )(*lams, gsub, q, kb, vb)


def _attn_sample_body(pt_ref, lq1, lk1, lq2, lk2, gsub_ref, q_ref, kn_ref, vn_ref, *rest,
                      n_batch, n_pages, t, lam_init):
    del pt_ref
    n_blk = n_batch * n_pages
    kp_refs, vp_refs, o_ref = rest[:n_blk], rest[n_blk:2 * n_blk], rest[2 * n_blk]
    lam = _diff_lambda(lq1[...], lk1[...], lq2[...], lk2[...], lam_init)
    nq = A_HEADS * t
    q_head = (_iota((2 * nq, 1), 0) % nq) // t
    q_tok = _iota((2 * nq, 1), 0) % t
    width = kp_refs[0].shape[0]
    past_ok = (_iota((2 * nq, width), 1) % A_HEADS) == q_head
    new_lane = _iota((2 * nq, nq), 1)
    new_ok = ((new_lane % A_HEADS) == q_head) & ((new_lane // A_HEADS) <= q_tok)
    for j in range(n_batch):
        rows = slice(j * nq, (j + 1) * nq)
        kps = kp_refs[j * n_pages:(j + 1) * n_pages]
        vps = vp_refs[j * n_pages:(j + 1) * n_pages]
        qs = _stack_maps(q_ref[rows, :])
        s_past = [jnp.where(past_ok, _dot_nt(qs, kp[...].astype(BF16)), -jnp.inf) for kp in kps]
        s_new = jnp.where(new_ok, _dot_nt(qs, kn_ref[rows, :].astype(BF16)), -jnp.inf)
        m = jnp.max(s_new, axis=-1, keepdims=True)
        for sp in s_past:
            m = jnp.maximum(m, jnp.max(sp, axis=-1, keepdims=True))
        p_new = jnp.exp2(s_new - m)
        l = jnp.sum(p_new, axis=-1, keepdims=True)
        acc = _dot(p_new.astype(BF16), vn_ref[rows, :].astype(BF16))
        for sp, vp in zip(s_past, vps):
            p = jnp.exp2(sp - m)
            l = l + jnp.sum(p, axis=-1, keepdims=True)
            acc = acc + _dot(p.astype(BF16), vp[...].astype(BF16))
        o = acc / l
        o = o[:nq] - lam * o[nq:]
        o_ref[rows, :] = _subln(o, gsub_ref[...], lam_init).astype(o_ref.dtype)


def _attn_sample(lams, gsub, q, k_new, v_new, cache_k, cache_v, layer, page_table, t, lam_init):
    bd, n_pages = page_table.shape
    nb = math.gcd(bd, SAMPLE_BATCH_PER_STEP)
    nq = A_HEADS * t
    rows = cache_k.shape[2]
    lam_spec = pl.BlockSpec((1, A_DH), lambda b, pt: (0, 0))
    tok_spec = pl.BlockSpec((nb * nq, A_DK), lambda b, pt: (b, 0))

    def page_spec(j, p):
        return pl.BlockSpec((None, None, rows, A_DK), lambda b, pt: (layer, pt[b * nb + j, p], 0, 0))

    page_specs = [page_spec(j, p) for j in range(nb) for p in range(n_pages)]
    grid_spec = pltpu.PrefetchScalarGridSpec(
        num_scalar_prefetch=1,
        grid=(bd // nb,),
        in_specs=[lam_spec] * 4 + [pl.BlockSpec((1, A_DV), lambda b, pt: (0, 0)), tok_spec, tok_spec, tok_spec]
        + page_specs * 2,
        out_specs=tok_spec,
    )
    n_blk = nb * n_pages
    return pl.pallas_call(
        functools.partial(_attn_sample_body, n_batch=nb, n_pages=n_pages, t=t, lam_init=lam_init),
        grid_spec=grid_spec,
        out_shape=jax.ShapeDtypeStruct((bd * nq, A_DV), BF16),
        compiler_params=_params(1),
        name="attn_sample",
    )(page_table, *lams, gsub, q, k_new, v_new, *([cache_k] * n_blk), *([cache_v] * n_blk))


def _head_blocks(n):
    return (_iota((n, n), 0) // R_HEAD) == (_iota((n, n), 1) // R_HEAD)


def _seg_sum(x, seg_ones):
    w = seg_ones.shape[0]
    xb = x.astype(BF16)
    return jnp.concatenate([_dot(xb[:, i:i + w], seg_ones) for i in range(0, x.shape[1], w)], axis=1)


def _softplus(z):
    return jnp.maximum(z, 0.0) + jnp.log1p(jnp.exp(-jnp.abs(z)))


def _rwkv_tokens(cur, prev, prm, seg_ones):
    mix = {c: cur[c] + (prev[c] - cur[c]) * prm["mu_" + c] for c in ("r", "k", "v", "x")}
    r, k, v = mix["r"], mix["k"], mix["v"]
    xwa = mix["x"][:, :LORA_W + LORA_A]
    xg = mix["x"][:, LORA_W + LORA_A:]
    w_in = prm["w0"] + _dot(jnp.tanh(xwa).astype(BF16), prm["w2"])
    lw = -jnp.exp(-_softplus(-w_in) - 0.5)
    a = jax.nn.sigmoid(prm["a0"] + _dot(xwa.astype(BF16), prm["a2"]))
    g = _dot(jax.nn.sigmoid(xg).astype(BF16), prm["g2"])
    kk = k * prm["k_k"]
    nrm = jnp.sqrt(_seg_sum(kk * kk, seg_ones))
    kk = kk / jnp.maximum(nrm, 1e-12)
    k2 = k * (1.0 + (a - 1.0) * prm["k_a"])
    bonus = _seg_sum(r * k2 * prm["r_k"], seg_ones) * v
    return r, lw, k2, v, kk, a, g, bonus


def _rwkv_finish(y, bonus, g, prm, seg_ones):
    mean = _seg_sum(y, seg_ones) * (1.0 / R_HEAD)
    d = y - mean
    var = _seg_sum(d * d, seg_ones) * (1.0 / R_HEAD)
    yn = d * lax.rsqrt(var + GN_EPS) * prm["ln_w"] + prm["ln_b"]
    return (yn + bonus) * g


def _stack_heads(x):
    lane = _iota(x.shape, 1)
    zero = jnp.zeros_like(x)
    return jnp.concatenate([jnp.where(lane < R_HEAD, x, zero), jnp.where(lane >= R_HEAD, x, zero)], axis=0)


def _wkv_front_stages(chunks):
    c = chunks[0][0].shape[0]
    n = 2 * c
    tril = (_iota((c, c), 1) <= _iota((c, c), 0)).astype(BF16)
    row = _iota((n, n), 0)
    col = _iota((n, n), 1)
    same = (row // c) == (col // c)
    strict = same & (col < row)
    incl = same & (col <= row)
    zero = jnp.zeros((n, n), F32)
    env = {}

    def cumsum():
        env["big_l"] = [_sel_dot(tril, ch[1], passes=2) for ch in chunks]

    def operands():
        ops = []
        for (r, lw, k2, v, kk, a), big_l in zip(chunks, env["big_l"]):
            l_end = big_l[c - 1:c, :]
            e_in = jnp.exp(big_l)
            e_ex = jnp.exp(big_l - lw)
            e_neg = jnp.exp(-big_l)
            e_rem = jnp.exp(l_end - big_l)
            b = kk * a
            a_t = _stack_heads(-kk * e_ex)
            r_t = _stack_heads(r * e_in)
            ops.append(dict(
                l_end=l_end, a_t=a_t, r_t=r_t, a_tb=a_t.astype(BF16), r_tb=r_t.astype(BF16),
                b_t=_stack_heads(b * e_neg).astype(BF16), k_t=_stack_heads(k2 * e_neg).astype(BF16),
                b_h=_stack_heads(b * e_rem), k_h=_stack_heads(k2 * e_rem),
                v_s=_stack_heads(v).astype(BF16)))
        env["ops"] = ops

    def pair(name, lhs, rhs, mask):
        def run():
            env[name] = [jnp.where(mask, _dot_nt(o[lhs], o[rhs]), zero) for o in env["ops"]]
        return run

    def transposes():
        env["b_ht"] = [o["b_h"].T.astype(BF16) for o in env["ops"]]
        env["k_ht"] = [o["k_h"].T.astype(BF16) for o in env["ops"]]

    stages = [cumsum, operands, pair("m_ab", "a_tb", "b_t", strict), pair("m_ak", "a_tb", "k_t", strict),
              pair("m_rb", "r_tb", "b_t", incl), pair("m_rk", "r_tb", "k_t", incl), transposes]
    return stages, env


_WKV_FRONT_KEYS = ("ops", "m_ab", "m_ak", "m_rb", "m_rk", "b_ht", "k_ht")


def _wkv_back_stages(fronts, c):
    n = 2 * c
    steps = max(1, int(math.ceil(math.log2(c))))
    eye_n = jnp.where(_iota((n, n), 0) == _iota((n, n), 1), 1.0, 0.0)
    eye = (_iota((PAIR, PAIR), 0) == _iota((PAIR, PAIR), 1))
    env = {}

    def rhs():
        env.update({key: [x for f in fronts for x in f[key]] for key in _WKV_FRONT_KEYS})
        env["x"] = [jnp.concatenate([o["a_t"], _dot(mk.astype(BF16), o["v_s"])], axis=1).astype(BF16)
                    for o, mk in zip(env["ops"], env["m_ak"])]
        env["pb"] = [p.astype(BF16) for p in env["m_ab"]]
        env["t"] = [eye_n + p for p in env["m_ab"]]

    def square():
        env["pb"] = [_dot(pb, pb).astype(BF16) for pb in env["pb"]]

    def extend():
        env["t"] = [t + _dot(t.astype(BF16), pb) for t, pb in zip(env["t"], env["pb"])]

    def solve():
        env["w"] = [_dot(t.astype(BF16), x).astype(BF16) for t, x in zip(env["t"], env["x"])]

    def state_side():
        env["mg"] = [_dot(bt, w) for bt, w in zip(env["b_ht"], env["w"])]
        env["kv"] = [_dot(kt, o["v_s"]) for kt, o in zip(env["k_ht"], env["ops"])]

    def output_side():
        env["qy"] = [_dot(mr.astype(BF16), w) for mr, w in zip(env["m_rb"], env["w"])]
        env["rkv"] = [_dot(mr.astype(BF16), o["v_s"]) for mr, o in zip(env["m_rk"], env["ops"])]

    def both_sides():
        zeros = jnp.zeros((n, PAIR), BF16)
        big = []
        for o, bt, kt, mb, mk, w in zip(env["ops"], env["b_ht"], env["k_ht"], env["m_rb"], env["m_rk"], env["w"]):
            lhs = jnp.concatenate([jnp.concatenate([bt, kt], axis=1),
                                   jnp.concatenate([mb, mk], axis=1).astype(BF16)], axis=0)
            rhs_mat = jnp.concatenate([w, jnp.concatenate([zeros, o["v_s"]], axis=1)], axis=0)
            big.append(_dot(lhs, rhs_mat))
        env["mg"] = [x[:PAIR] for x in big]
        env["qy"] = [x[PAIR:] for x in big]
        env["kv"] = env["rkv"] = [0.0] * len(big)

    def finish():
        out = []
        for o, mg, kv, qy, rk in zip(env["ops"], env["mg"], env["kv"], env["qy"], env["rkv"]):
            m_mat = jnp.where(eye, jnp.exp(o["l_end"]), 0.0) + mg[:, :PAIR]
            g_mat = mg[:, PAIR:] + kv
            q_mat = o["r_t"] + qy[:, :PAIR]
            y0 = qy[:, PAIR:] + rk
            out.append((q_mat.astype(BF16), y0, m_mat.astype(BF16), g_mat))
        env["out"] = out

    stages = [rhs]
    for _ in range(steps - 1):
        stages += [square, extend]
    stages += [solve] + ([both_sides] if n % LANES == 0 else [state_side, output_side]) + [finish]
    return stages, env


def _wkv_prepare(chunks):
    front, env = _wkv_front_stages(chunks)
    for stage in front:
        stage()
    back, env = _wkv_back_stages([env], chunks[0][0].shape[0])
    for stage in back:
        stage()
    return env["out"]


def _wkv_apply(prep, st):
    q_mat, y0, m_mat, g_mat = prep
    c = y0.shape[0] // 2
    stb = st.astype(BF16)
    y_st = _dot(q_mat, stb) + y0
    return y_st[:c] + y_st[c:], _dot(m_mat, stb) + g_mat


def _state_in(s0):
    x = s0.reshape(PAIR, R_HEAD)
    sel = (_iota((PAIR, R_HEAD), 0) % R_HEAD == _iota((PAIR, R_HEAD), 1)).astype(BF16)
    full = _sel_dot_nt(sel, x)
    return jnp.where(_head_blocks(PAIR), full, 0.0)


def _state_out(st):
    folded = st[:R_HEAD] + st[R_HEAD:]
    eye = (_iota((PAIR, PAIR), 0) == _iota((PAIR, PAIR), 1)).astype(BF16)
    return _sel_dot_nt(eye, folded).reshape(2, R_HEAD, R_HEAD)


_RWKV_PARAM_NAMES = ("mu_r", "mu_k", "mu_v", "mu_x", "w0", "w2", "a0", "a2", "g2", "k_k", "k_a", "r_k",
                     "ln_w", "ln_b")


def _split_pairs(z, rows):
    return [z[rows, p * PAIR:(p + 1) * PAIR] for p in range(z.shape[1] // PAIR)]


def _run_chains(preps, states, between=None):
    states = list(states)
    ys = [[] for _ in preps]
    for i in range(len(preps[0])):
        for p, chain in enumerate(preps):
            y, states[p] = _wkv_apply(chain[i], states[p])
            ys[p].append(y)
        if between is not None:
            between()
    return jnp.concatenate([jnp.concatenate(col, axis=0) for col in ys], axis=1), states


def _rwkv_prompt_body(*refs, n_chunks):
    n_prm = len(_RWKV_PARAM_NAMES)
    pr = dict(zip(("r", "k", "v", "x"), refs[0:4]))
    sh = dict(zip(("r", "k", "v", "x"), refs[4:8]))
    prm = {nm: ref[...] for nm, ref in zip(_RWKV_PARAM_NAMES, refs[8:8 + n_prm])}
    s0_ref = refs[8 + n_prm]
    o_ref, s_out_ref = refs[9 + n_prm], refs[10 + n_prm]
    scratch = refs[11 + n_prm:]
    carry = dict(zip(("r", "k", "v", "x"), scratch[0:4]))
    st_ref, q_sc, y0_sc, m_sc, g_sc, bonus_sc, gate_sc = scratch[4:11]
    g = pl.program_id(2)
    tc, width = bonus_sc.shape
    n_pairs = width // PAIR
    cl = tc // n_chunks
    seg_ones = _head_blocks(min(width, 2 * LANES)).astype(BF16)

    @pl.when(g == 0)
    def _():
        for c in carry:
            carry[c][...] = jnp.broadcast_to(sh[c][...], carry[c].shape)
        for p in range(n_pairs):
            st_ref[p] = _state_in(s0_ref[2 * p:2 * p + 2])
        for ref in (q_sc, y0_sc, m_sc, g_sc, bonus_sc, gate_sc):
            ref[...] = jnp.zeros(ref.shape, ref.dtype)

    def stored():
        return [[(q_sc[p * n_chunks + i], y0_sc[p * n_chunks + i], m_sc[p * n_chunks + i],
                  g_sc[p * n_chunks + i]) for i in range(n_chunks)] for p in range(n_pairs)]

    def emit(y, bonus, gate, tile):
        rows = pl.ds(pl.multiple_of(tile * tc, tc), tc)
        o_ref[rows, :] = _rwkv_finish(y, bonus, gate, prm, seg_ones).astype(o_ref.dtype)

    old = stored()
    old_bonus, old_gate = bonus_sc[...], gate_sc[...]
    st_in = [st_ref[p] for p in range(n_pairs)]

    cur = {c: pr[c][...].astype(F32) for c in pr}
    prev = {}
    for c in cur:
        first = _iota(cur[c].shape, 0) == 0
        prev[c] = jnp.where(first, carry[c][0:1, :], pltpu.roll(cur[c], 1, 0))
    for c in cur:
        carry[c][...] = jnp.broadcast_to(cur[c][tc - 1:tc, :], carry[c].shape)
    r, lw, k2, v, kk, a, gate, bonus = _rwkv_tokens(cur, prev, prm, seg_ones)
    per_chunk = [list(zip(*(_split_pairs(z, slice(i * cl, (i + 1) * cl)) for z in (r, lw, k2, v, kk, a))))
                 for i in range(n_chunks)]
    fronts = [_wkv_front_stages([per_chunk[i][p] for i in range(n_chunks)]) for p in range(n_pairs)]
    todo = []
    for p, (stages, _) in enumerate(fronts):
        todo += stages[:WKV_HEAD_STAGES]
        if p > 0:
            todo += fronts[p - 1][0][WKV_HEAD_STAGES:]
    todo += fronts[-1][0][WKV_HEAD_STAGES:]
    back_stages, back_env = _wkv_back_stages([env for _, env in fronts], cl)
    todo += back_stages
    per = max(1, len(todo) // (n_chunks + 1))

    def between():
        for stage in todo[:per]:
            stage()
        del todo[:per]

    between()
    y_old, st = _run_chains(old, st_in, between)
    for stage in todo:
        stage()
    emit(y_old, old_bonus, old_gate, jnp.maximum(g - 1, 0))
    for p in range(n_pairs):
        st_ref[p] = jnp.where(g == 0, st_in[p], st[p])
    prepared = back_env["out"]
    for i, (q_mat, y0, m_mat, g_mat) in enumerate(prepared):
        q_sc[i], y0_sc[i], m_sc[i], g_sc[i] = q_mat, y0, m_mat, g_mat
    bonus_sc[...] = bonus
    gate_sc[...] = gate

    @pl.when(g == pl.num_programs(2) - 1)
    def _():
        y_new, st_end = _run_chains(stored(), [st_ref[p] for p in range(n_pairs)])
        emit(y_new, bonus_sc[...], gate_sc[...], g)
        for p in range(n_pairs):
            s_out_ref[2 * p:2 * p + 2] = _state_out(st_end[p])


def _rwkv_sample_body(*refs, n_seq, t_pad, t_valid):
    n_prm = len(_RWKV_PARAM_NAMES)
    pr = dict(zip(("r", "k", "v", "x"), refs[0:4]))
    pv = dict(zip(("r", "k", "v", "x"), refs[4:8]))
    prm = {nm: ref[...] for nm, ref in zip(_RWKV_PARAM_NAMES, refs[8:8 + n_prm])}
    s0_ref = refs[8 + n_prm]
    o_ref, s_out_ref = refs[9 + n_prm], refs[10 + n_prm]
    cur = {c: pr[c][...] for c in pr}
    prev = {c: pv[c][...] for c in pv}
    width = o_ref.shape[1]
    n_pairs = width // PAIR
    seg_ones = _head_blocks(min(width, 2 * LANES)).astype(BF16)
    r, lw, k2, v, kk, a, gate, bonus = _rwkv_tokens(cur, prev, prm, seg_ones)
    valid = (_iota(r.shape, 0) % t_pad) < t_valid
    zero = jnp.zeros_like(r)
    r, lw, k2, v, kk = (jnp.where(valid, z, zero) for z in (r, lw, k2, v, kk))
    per_seq = [list(zip(*(_split_pairs(z, slice(i * t_pad, (i + 1) * t_pad)) for z in (r, lw, k2, v, kk, a))))
               for i in range(n_seq)]
    chunks = [per_seq[i][p] for i in range(n_seq) for p in range(n_pairs)]
    states = [_state_in(s0_ref[i, 2 * p:2 * p + 2]) for i in range(n_seq) for p in range(n_pairs)]
    applied = [_wkv_apply(prep, st) for prep, st in zip(_wkv_prepare(chunks), states)]
    rows = []
    for i in range(n_seq):
        for p in range(n_pairs):
            s_out_ref[i, 2 * p:2 * p + 2] = _state_out(applied[i * n_pairs + p][1])
        rows.append(jnp.concatenate([applied[i * n_pairs + p][0] for p in range(n_pairs)], axis=1))
    y = jnp.concatenate(rows, axis=0)
    o_ref[...] = _rwkv_finish(y, bonus, gate, prm, seg_ones).astype(o_ref.dtype)


def _rwkv_param_arrays(p):
    z_w = jnp.zeros((LORA_A, D_RWKV), BF16)
    z_a = jnp.zeros((LORA_W, D_RWKV), BF16)
    return dict(
        mu=p["mu_shift"].reshape(1, N_SHIFT),
        w0=p["w0"].reshape(1, D_RWKV),
        w2=jnp.concatenate([p["w2"].astype(BF16), z_w], axis=0),
        a0=p["a0"].reshape(1, D_RWKV),
        a2=jnp.concatenate([z_a, p["a2"].astype(BF16)], axis=0),
        g2=p["g2"].astype(BF16),
        k_k=p["k_k"].reshape(1, D_RWKV),
        k_a=p["k_a"].reshape(1, D_RWKV),
        r_k=p["r_k"].reshape(1, D_RWKV),
        ln_w=p["ln_x_w"].reshape(1, D_RWKV),
        ln_b=p["ln_x_b"].reshape(1, D_RWKV),
    )


def _rwkv_param_specs(block_of):
    x_blk = (3 * D_RWKV) // X_COLS

    def vec(off):
        return pl.BlockSpec((1, RWKV_COLS), lambda *i: (0, off + block_of(*i)))

    def mat(rows):
        return pl.BlockSpec((rows, RWKV_COLS), lambda *i: (0, block_of(*i)))

    return [
        vec(0), vec(N_COL_BLOCKS), vec(2 * N_COL_BLOCKS),
        pl.BlockSpec((1, X_COLS), lambda *i: (0, x_blk)),
        vec(0), mat(LORA_W + LORA_A), vec(0), mat(LORA_W + LORA_A), mat(LORA_G),
        vec(0), vec(0), vec(0), vec(0), vec(0),
    ]


def _rwkv_param_operands(pa):
    return [pa["mu"], pa["mu"], pa["mu"], pa["mu"], pa["w0"], pa["w2"], pa["a0"], pa["a2"], pa["g2"],
            pa["k_k"], pa["k_a"], pa["r_k"], pa["ln_w"], pa["ln_b"]]


def _col_specs(rows, row_of, block_of):
    x_blk = (3 * D_RWKV) // X_COLS
    specs = [pl.BlockSpec((rows, RWKV_COLS), lambda *i, o=o: (row_of(*i), o * N_COL_BLOCKS + block_of(*i)))
             for o in range(3)]
    specs.append(pl.BlockSpec((rows, X_COLS), lambda *i: (row_of(*i), x_blk)))
    return specs


def _rwkv_prompt(pr, shift0, s0, pa, b, s):
    tc = min(512, s)
    nt = s // tc
    n_chunks = max(1, tc // CHUNK)
    cl = tc // n_chunks
    pairs = RWKV_COLS // PAIR
    block_of = lambda bi, ci, gi: ci
    row_of = lambda bi, ci, gi: bi * nt + gi
    x_blk = (3 * D_RWKV) // X_COLS
    shift_specs = [pl.BlockSpec((None, 1, RWKV_COLS), lambda bi, ci, gi, o=o: (bi, 0, o * N_COL_BLOCKS + ci))
                   for o in range(3)]
    shift_specs.append(pl.BlockSpec((None, 1, X_COLS), lambda bi, ci, gi: (bi, 0, x_blk)))
    state_spec = pl.BlockSpec((None, 2 * pairs, R_HEAD, R_HEAD), lambda bi, ci, gi: (bi, ci, 0, 0))
    shift3 = shift0.reshape(b, 1, N_SHIFT)
    return pl.pallas_call(
        functools.partial(_rwkv_prompt_body, n_chunks=n_chunks),
        grid=(b, N_COL_BLOCKS, nt),
        in_specs=_col_specs(tc, row_of, block_of) + shift_specs + _rwkv_param_specs(block_of) + [state_spec],
        out_specs=(pl.BlockSpec((s, RWKV_COLS), lambda bi, ci, gi: (bi, ci)), state_spec),
        out_shape=(jax.ShapeDtypeStruct((b * s, D_RWKV), BF16),
                   jax.ShapeDtypeStruct((b, R_HEADS, R_HEAD, R_HEAD), F32)),
        scratch_shapes=[pltpu.VMEM((SUBLANES, RWKV_COLS), F32)] * 3 + [
            pltpu.VMEM((SUBLANES, X_COLS), F32),
            pltpu.VMEM((pairs, PAIR, PAIR), F32),
            pltpu.VMEM((pairs * n_chunks, 2 * cl, PAIR), BF16),
            pltpu.VMEM((pairs * n_chunks, 2 * cl, PAIR), F32),
            pltpu.VMEM((pairs * n_chunks, PAIR, PAIR), BF16),
            pltpu.VMEM((pairs * n_chunks, PAIR, PAIR), F32),
            pltpu.VMEM((tc, RWKV_COLS), F32),
            pltpu.VMEM((tc, RWKV_COLS), F32),
        ],
        compiler_params=_params(3),
        name="rwkv_prompt",
    )(pr, pr, pr, pr, shift3, shift3, shift3, shift3, *_rwkv_param_operands(pa), s0)


def _rwkv_sample(pr_pad, prev_pad, s0, pa, bd, t_pad, t_valid):
    n_seq = min(16, bd)
    rows = n_seq * t_pad
    pairs = RWKV_COLS // PAIR
    block_of = lambda gi, ci: ci
    row_of = lambda gi, ci: gi
    state_spec = pl.BlockSpec((n_seq, 2 * pairs, R_HEAD, R_HEAD), lambda gi, ci: (gi, ci, 0, 0))
    return pl.pallas_call(
        functools.partial(_rwkv_sample_body, n_seq=n_seq, t_pad=t_pad, t_valid=t_valid),
        grid=(bd // n_seq, N_COL_BLOCKS),
        in_specs=_col_specs(rows, row_of, block_of) * 2 + _rwkv_param_specs(block_of) + [state_spec],
        out_specs=(pl.BlockSpec((rows, RWKV_COLS), lambda gi, ci: (gi, ci)), state_spec),
        out_shape=(jax.ShapeDtypeStruct((bd * t_pad, D_RWKV), BF16),
                   jax.ShapeDtypeStruct((bd, R_HEADS, R_HEAD, R_HEAD), F32)),
        compiler_params=_params(2),
        name="rwkv_sample",
    )(pr_pad, pr_pad, pr_pad, pr_pad, prev_pad, prev_pad, prev_pad, prev_pad, *_rwkv_param_operands(pa), s0)


def _gelu_tanh(x):
    k = math.sqrt(2.0 / math.pi)
    return (0.5 * x) * (1.0 + jnp.tanh(x * (k + (k * 0.044715) * (x * x))))


def _post_body(x_ref, oa_ref, or_ref, pe_ref, wo_ref, gmp_ref, gfp_ref, wg_ref, wv_ref, cwg_ref, cwv_ref,
               cbg_ref, cbv_ref, wd_ref, gfo_ref, wple_ref, wgate_ref, gple_ref, c0g_ref, c0v_ref,
               y_ref, cng_ref, cnv_ref, ext_g_ref, ext_v_ref, *, shift):
    ti = pl.program_id(1)
    n_state = (CONV_W - 1) * shift
    tm = x_ref.shape[0]
    pad = ext_g_ref.shape[0] - tm

    @pl.when(ti == 0)
    def _():
        ext_g_ref[pad - n_state:pad, :] = c0g_ref[...]
        ext_v_ref[pad - n_state:pad, :] = c0v_ref[...]

    mix = _dot(jnp.concatenate([oa_ref[...], or_ref[...]], axis=1), wo_ref[...])
    x1 = x_ref[...] + _rms(mix, gmp_ref[...])
    h = _rms(x1, gfp_ref[...]).astype(BF16)
    halves = []
    for w_ref, cw_ref, cb_ref, ext_ref in ((wg_ref, cwg_ref, cbg_ref, ext_g_ref),
                                           (wv_ref, cwv_ref, cbv_ref, ext_v_ref)):
        ext_ref[pad:pad + tm, :] = _dot(h, w_ref[...])
        conv = cb_ref[...]
        for j in range(CONV_W):
            start = pad - (CONV_W - 1 - j) * shift
            conv = conv + cw_ref[j:j + 1, :] * ext_ref[start:start + tm, :]
        halves.append(conv)
    act = _gelu_tanh(halves[0]) * halves[1]
    x2 = x1 + _rms(_dot(act.astype(BF16), wd_ref[...]), gfo_ref[...])
    gate = jax.nn.sigmoid(_dot(x2.astype(BF16), wgate_ref[...]))
    ple = _dot(pe_ref[...].astype(BF16), wple_ref[...]) * gate
    y_ref[...] = x2 + _rms(ple, gple_ref[...])

    for ext_ref, new_ref in ((ext_g_ref, cng_ref), (ext_v_ref, cnv_ref)):
        tail = ext_ref[pad + tm - n_state:pad + tm, :]
        ext_ref[pad - n_state:pad, :] = tail
        new_ref[...] = tail


def _post(x, oa, orw, pe, wts, conv0_g, conv0_v, n_seq, shift):
    n = x.shape[0]
    rows_per_seq = n // n_seq
    tm = min(512, rows_per_seq) if shift == 1 else rows_per_seq
    nt = rows_per_seq // tm
    n_state = (CONV_W - 1) * shift
    pad = max(SUBLANES, n_state)
    row = lambda si, ti: (si * nt + ti, 0)
    state_spec = pl.BlockSpec((None, n_state, D_FF_PAD), lambda si, ti: (si, 0, 0))
    w_arrays = [wts[k] for k in ("w_o", "g_mix_post", "g_ffn_pre", "w_gate_up", "w_val_up", "cw_g", "cw_v",
                                 "cb_g", "cb_v", "w_down", "g_ffn_post", "w_ple", "w_ple_gate", "g_ple")]
    return pl.pallas_call(
        functools.partial(_post_body, shift=shift),
        grid=(n_seq, nt),
        in_specs=[
            pl.BlockSpec((tm, D_MODEL), row),
            pl.BlockSpec((tm, D_ATT), row),
            pl.BlockSpec((tm, D_RWKV), row),
            pl.BlockSpec((tm, D_PLE), row),
        ] + [_const_spec(w.shape) for w in w_arrays] + [state_spec, state_spec],
        out_specs=(pl.BlockSpec((tm, D_MODEL), row), state_spec, state_spec),
        out_shape=(jax.ShapeDtypeStruct((n, D_MODEL), F32),
                   jax.ShapeDtypeStruct((n_seq, n_state, D_FF_PAD), F32),
                   jax.ShapeDtypeStruct((n_seq, n_state, D_FF_PAD), F32)),
        scratch_shapes=[pltpu.VMEM((pad + tm, D_FF_PAD), F32)] * 2,
        compiler_params=_params(2),
        name="post",
    )(x, oa, orw, pe, *w_arrays, conv0_g, conv0_v)


def _post_weights(p):
    pad_c = D_FF_PAD - D_FF

    def halves(a):
        widths = [(0, 0)] * (a.ndim - 1) + [(0, pad_c)]
        return jnp.pad(a[..., :D_FF], widths), jnp.pad(a[..., D_FF:], widths)

    w_g, w_v = halves(p["w_up"].astype(BF16))
    cw_g, cw_v = halves(p["conv_w"])
    cb_g, cb_v = halves(p["conv_b"].reshape(1, 2 * D_FF))
    return dict(
        w_o=p["w_o"].astype(BF16),
        g_mix_post=p["g_mix_post"].reshape(1, D_MODEL),
        g_ffn_pre=p["g_ffn_pre"].reshape(1, D_MODEL),
        w_gate_up=w_g, w_val_up=w_v, cw_g=cw_g, cw_v=cw_v, cb_g=cb_g, cb_v=cb_v,
        w_down=jnp.pad(p["w_down"].astype(BF16), ((0, pad_c), (0, 0))),
        g_ffn_post=p["g_ffn_post"].reshape(1, D_MODEL),
        w_ple=p["w_ple"].astype(BF16),
        w_ple_gate=p["w_ple_gate"].astype(BF16),
        g_ple=p["g_ple"].reshape(1, D_MODEL),
    ), halves


def kernel(x_prompt, x_sample, p_prompt, p_sample, cache_k, cache_v, page_table, state_shift, state_wkv,
           state_conv, g_mix_pre, w_in, lam_q1, lam_k1, lam_q2, lam_k2, g_subln, mu_shift, w0, w2, a0, a2,
           g2, k_k, k_a, r_k, ln_x_w, ln_x_b, w_o, g_mix_post, g_ffn_pre, w_up, conv_w, conv_b, w_down,
           g_ffn_post, w_ple, w_ple_gate, g_ple):
    depth = w_in.shape[0]
    bp, sp, _ = x_prompt.shape
    bd, td, _ = x_sample.shape
    n_pool, page = cache_k.shape[1], cache_k.shape[2]
    xp = x_prompt.reshape(bp * sp, D_MODEL)
    xs = x_sample.reshape(bd * td, D_MODEL)
    t_pad = SUBLANES
    outs = {k: [] for k in ("kp", "vp", "ks", "vs", "shp", "wkp", "cvp", "shs", "wks", "cvs")}
    for l in range(depth):
        lam_init = 0.8 - 0.6 * math.exp(-0.3 * l)
        p = dict(mu_shift=mu_shift[l], w0=w0[l], w2=w2[l], a0=a0[l], a2=a2[l], g2=g2[l], k_k=k_k[l],
                 k_a=k_a[l], r_k=r_k[l], ln_x_w=ln_x_w[l], ln_x_b=ln_x_b[l], w_o=w_o[l],
                 g_mix_post=g_mix_post[l], g_ffn_pre=g_ffn_pre[l], w_up=w_up[l], conv_w=conv_w[l],
                 conv_b=conv_b[l], w_down=w_down[l], g_ffn_post=g_ffn_post[l], w_ple=w_ple[l],
                 w_ple_gate=w_ple_gate[l], g_ple=g_ple[l])
        w_in_bf = w_in[l].astype(BF16)
        g_pre = g_mix_pre[l].reshape(1, D_MODEL)
        lams = [z[l].reshape(1, A_DH) for z in (lam_q1, lam_k1, lam_q2, lam_k2)]
        gsub = g_subln[l].reshape(1, A_DV)
        pa = _rwkv_param_arrays(p)
        wts, halves = _post_weights(p)

        tm_p = min(512, sp)
        q, k, v, kb, vb, pr, pr_tail = _inproj(xp, g_pre, w_in_bf, tm_p, BF16)
        oa = _attn_prompt(lams, gsub, q, kb, vb, bp, sp, lam_init)
        orw, wkv_p = _rwkv_prompt(pr, jnp.zeros((bp, N_SHIFT), F32),
                                  jnp.zeros((bp, R_HEADS, R_HEAD, R_HEAD), F32), pa, bp, sp)
        zc = jnp.zeros((bp, CONV_W - 1, D_FF_PAD), F32)
        xp, cng, cnv = _post(xp, oa, orw, p_prompt[l].reshape(bp * sp, D_PLE), wts, zc, zc, bp, 1)
        outs["kp"].append(k.reshape(bp, sp, A_HEADS, A_DK))
        outs["vp"].append(v.reshape(bp, sp, A_HEADS, A_DV))
        outs["shp"].append(pr_tail.reshape(bp, sp // tm_p, SUBLANES, N_SHIFT)[:, -1, -1])
        outs["wkp"].append(wkv_p)
        outs["cvp"].append(jnp.concatenate([cng[..., :D_FF], cnv[..., :D_FF]], axis=-1))

        q, k, v, kb, vb, pr, _ = _inproj(xs, g_pre, w_in_bf, min(512, bd * td), F32)
        q_ht = jnp.swapaxes(q.reshape(bd, td, A_HEADS, A_DK), 1, 2).reshape(bd * A_HEADS * td, A_DK)
        oa = _attn_sample(lams, gsub, q_ht, k.reshape(bd * td * A_HEADS, A_DK), v.reshape(bd * td * A_HEADS, A_DV),
                          cache_k.reshape(depth, n_pool, page * A_HEADS, A_DK),
                          cache_v.reshape(depth, n_pool, page * A_HEADS, A_DV), l, page_table, td, lam_init)
        oa = jnp.swapaxes(oa.reshape(bd, A_HEADS, td, A_DV), 1, 2).reshape(bd, td, D_ATT)
        pr3 = pr.reshape(bd, td, N_SHIFT)
        prev3 = jnp.concatenate([state_shift[l][:, None, :], pr3[:, :td - 1]], axis=1)
        pad_t = ((0, 0), (0, t_pad - td), (0, 0))
        orw, wkv_s = _rwkv_sample(jnp.pad(pr3, pad_t).reshape(bd * t_pad, N_SHIFT),
                                  jnp.pad(prev3, pad_t).reshape(bd * t_pad, N_SHIFT),
                                  state_wkv[l], pa, bd, t_pad, td)
        orw = orw.reshape(bd, t_pad, D_RWKV)[:, :td]
        tmaj = lambda z: jnp.swapaxes(z, 0, 1).reshape(td * bd, z.shape[-1])
        c0g, c0v = halves(jnp.swapaxes(state_conv[l], 0, 1).reshape(1, (CONV_W - 1) * bd, 2 * D_FF))
        ys, cng, cnv = _post(tmaj(xs.reshape(bd, td, D_MODEL)), tmaj(oa), tmaj(orw), tmaj(p_sample[l]),
                             wts, c0g, c0v, 1, bd)
        xs = jnp.swapaxes(ys.reshape(td, bd, D_MODEL), 0, 1).reshape(bd * td, D_MODEL)
        cvs = jnp.concatenate([cng[..., :D_FF], cnv[..., :D_FF]], axis=-1).reshape(CONV_W - 1, bd, 2 * D_FF)
        outs["ks"].append(k.reshape(bd, td, A_HEADS, A_DK))
        outs["vs"].append(v.reshape(bd, td, A_HEADS, A_DV))
        outs["shs"].append(pr3[:, td - 1])
        outs["wks"].append(wkv_s)
        outs["cvs"].append(jnp.swapaxes(cvs, 0, 1))
    st = lambda key: jnp.stack(outs[key])
    return (xp.reshape(bp, sp, D_MODEL), xs.reshape(bd, td, D_MODEL), st("kp"), st("vp"), st("ks"), st("vs"),
            st("shp"), st("wkp"), st("cvp"), st("shs"), st("wks"), st("cvs"))
```
